```python
import jax, jax.numpy as jnp
from jax import lax
import numpy as np

D_MODEL = 1024
BATCH = 2
SEQ = 16384
DEPTH = 1
DEC_BATCH = 16
DEC_SEQ = 32
PAST_LEN = 2048

CHUNK = 64
BAND_CHUNKS = 8
BAND_PAST = BAND_CHUNKS * CHUNK
A_HEADS = 8
A_HEAD_DIM = 64
A_WIDTH = A_HEADS * A_HEAD_DIM
REL_CLIP = 128
B_HEADS = 8
B_HEAD_DIM = 64
B_WIDTH = B_HEADS * B_HEAD_DIM
DECAY_RANK = 64
AAA_RANK = 64
GATE_RANK = 128
LNX_EPS = 64e-5
NORM_EPS = 1e-6
P_HEADS = 8
N_KEYS = 128
N_EXPERTS = N_KEYS * N_KEYS
P_KEY_DIM = 128
P_HALF = P_KEY_DIM // 2
P_TOPK = 16
P_BLOCK = 128

A_COLS = 3 * A_WIDTH
B_COLS = 3 * B_WIDTH + DECAY_RANK + AAA_RANK + GATE_RANK
G_COLS = 2 * D_MODEL
IN_COLS = A_COLS + B_COLS + G_COLS
B_SPLITS = [B_WIDTH, 2 * B_WIDTH, 3 * B_WIDTH, 3 * B_WIDTH + DECAY_RANK, 3 * B_WIDTH + DECAY_RANK + AAA_RANK]

kernel_name = "chunked_relbias_rwkv7_peer_stream_step"


def rmsnorm(x, g):
    xf = x.astype(jnp.float32)
    y = xf * lax.rsqrt(jnp.mean(xf * xf, axis=-1, keepdims=True) + NORM_EPS)
    return (y * g.astype(jnp.float32)).astype(x.dtype)


def rel_bias_lookup(rel_bias, dist):
    return rel_bias[:, jnp.clip(dist, -REL_CLIP, REL_CLIP) + REL_CLIP].astype(jnp.float32)


def band_attention_prompt(q, k, v, rel_bias):
    Bn, T, H, hd = q.shape
    nc = T // CHUNK
    qc = q.reshape(Bn, nc, CHUNK, H, hd)
    pad = ((0, 0), (BAND_CHUNKS, 0), (0, 0), (0, 0), (0, 0))
    kp = jnp.pad(k.reshape(Bn, nc, CHUNK, H, hd), pad)
    vp = jnp.pad(v.reshape(Bn, nc, CHUNK, H, hd), pad)
    kb = jnp.concatenate([kp[:, o:o + nc] for o in range(BAND_CHUNKS + 1)], axis=2)
    vb = jnp.concatenate([vp[:, o:o + nc] for o in range(BAND_CHUNKS + 1)], axis=2)
    s = jnp.einsum('bcqhd,bcnhd->bhcqn', qc, kb).astype(jnp.float32) * (hd ** -0.5)
    i = jnp.arange(CHUNK)
    o = jnp.arange(BAND_CHUNKS + 1)
    dist = i[:, None, None] + (BAND_CHUNKS - o)[None, :, None] * CHUNK - i[None, None, :]
    bias = rel_bias_lookup(rel_bias, dist).reshape(H, CHUNK, (BAND_CHUNKS + 1) * CHUNK)
    valid = (jnp.arange(nc)[:, None] - BAND_CHUNKS + o[None, :]) >= 0
    valid = jnp.repeat(valid, CHUNK, axis=1)
    s = jnp.where(valid[None, None, :, None, :], s + bias[None, :, None], -1e30)
    p = jax.nn.softmax(s, axis=-1).astype(v.dtype)
    out = jnp.einsum('bhcqn,bcnhd->bcqhd', p, vb)
    return out.reshape(Bn, T, H * hd)


def band_attention_sample(q, k, v, k_cache, v_cache, rel_bias):
    Bd, S, H, hd = q.shape
    L = k_cache.shape[2]
    kall = jnp.concatenate([k_cache.astype(k.dtype), k.transpose(0, 2, 1, 3)], axis=2)
    vall = jnp.concatenate([v_cache.astype(v.dtype), v.transpose(0, 2, 1, 3)], axis=2)
    s = jnp.einsum('bqhd,bhkd->bhqk', q, kall).astype(jnp.float32) * (hd ** -0.5)
    dist = jnp.arange(S)[:, None] + L - jnp.arange(L + S)[None, :]
    s = s + rel_bias_lookup(rel_bias, dist)[None]
    p = jax.nn.softmax(s, axis=-1).astype(v.dtype)
    out = jnp.einsum('bhqk,bhkd->bqhd', p, vall)
    return out.reshape(Bd, S, H * hd)


def wkv7_scan(S0, r, w, k, v, a, b):
    xs = tuple(jnp.moveaxis(t.astype(jnp.float32), 1, 0) for t in (r, w, k, v, a, b))

    def step(S, inp):
        rt, wt, kt, vt, at, bt = inp
        sa = jnp.einsum('bhij,bhj->bhi', S, at)
        S = S * wt[:, :, None, :] + sa[..., None] * bt[:, :, None, :] + vt[..., None] * kt[:, :, None, :]
        return S, jnp.einsum('bhij,bhj->bhi', S, rt)

    S, ys = lax.scan(step, S0.astype(jnp.float32), xs)
    return jnp.moveaxis(ys, 0, 1), S


def rwkv7_branch(zb, wkv0, shift0, p):
    Bn, T, _ = zb.shape
    prev = jnp.concatenate([shift0.astype(zb.dtype), zb[:, :-1]], axis=1)
    zm = zb + (prev - zb) * p['shift_mu']
    r, k, v, wd, ad, gd = jnp.split(zm, B_SPLITS, axis=-1)
    w_log = -jax.nn.softplus(-(p['w_decay0'] + jnp.tanh(wd) @ p['w_decay_up'])) - 0.5
    decay = jnp.exp(-jnp.exp(w_log.astype(jnp.float32)))
    a = jax.nn.sigmoid(p['a0'] + ad @ p['w_a_up'])
    g = jax.nn.sigmoid(gd) @ p['w_g_up']
    heads = lambda t: t.reshape(Bn, T, B_HEADS, B_HEAD_DIM)
    kk = heads(k * p['k_k']).astype(jnp.float32)
    kk = kk / jnp.maximum(jnp.sqrt(jnp.sum(kk * kk, axis=-1, keepdims=True)), 1e-12)
    k = k * (1.0 + (a - 1.0) * p['k_a'])
    rh, kh, vh, ah = heads(r), heads(k), heads(v), heads(a)
    y, wkv_new = wkv7_scan(wkv0, rh, heads(decay), kh, vh, -kk, kk * ah.astype(jnp.float32))
    mu = jnp.mean(y, axis=-1, keepdims=True)
    var = jnp.mean(jnp.square(y - mu), axis=-1, keepdims=True)
    yn = ((y - mu) * lax.rsqrt(var + LNX_EPS)).reshape(Bn, T, B_WIDTH)
    yn = yn * p['lnx_g'].astype(jnp.float32) + p['lnx_b'].astype(jnp.float32)
    bonus = jnp.sum((rh * kh * p['r_k']).astype(jnp.float32), axis=-1, keepdims=True) * vh.astype(jnp.float32)
    out = ((yn + bonus.reshape(Bn, T, B_WIDTH)) * g.astype(jnp.float32)).astype(zb.dtype)
    return out, wkv_new, zb[:, -1:]


def peer(xn, w_query, sub_keys, expert_u, expert_v):
    M = xn.shape[0]
    padn = (-M) % P_BLOCK
    xb_all = jnp.pad(xn, ((0, padn), (0, 0))).reshape(-1, P_BLOCK, D_MODEL)

    def block(xb):
        q = (xb @ w_query).reshape(P_BLOCK, P_HEADS, 2, P_HALF)
        s = jnp.einsum('thcd,hcnd->thcn', q, sub_keys).astype(jnp.float32)
        sv, si = lax.top_k(s, P_TOPK)
        cand = (sv[:, :, 0, :, None] + sv[:, :, 1, None, :]).reshape(P_BLOCK, P_HEADS, P_TOPK * P_TOPK)
        cidx = (si[:, :, 0, :, None] * N_KEYS + si[:, :, 1, None, :]).reshape(P_BLOCK, P_HEADS, P_TOPK * P_TOPK)
        fv, fi = lax.top_k(cand, P_TOPK)
        eidx = jnp.take_along_axis(cidx, fi, axis=-1)
        gate = jax.nn.softmax(fv, axis=-1)
        hid = jax.nn.gelu(jnp.einsum('thkd,td->thk', expert_u[eidx], xb).astype(jnp.float32), approximate=False)
        coef = (gate * hid).astype(expert_v.dtype)
        return jnp.einsum('thk,thkd->td', coef, expert_v[eidx])

    return lax.map(block, xb_all).reshape(-1, D_MODEL)[:M]


def stream_layer(x, k_cache, v_cache, wkv0, shift0, p):
    Bn, T, _ = x.shape
    xn = rmsnorm(x, p['norm1_g'])
    z = xn @ p['w_in']
    za = z[..., :A_COLS]
    zb = z[..., A_COLS:A_COLS + B_COLS]
    zg = z[..., A_COLS + B_COLS:]
    q, k, v = [t.reshape(Bn, T, A_HEADS, A_HEAD_DIM) for t in jnp.split(za, 3, axis=-1)]
    if k_cache is None:
        o_a = band_attention_prompt(q, k, v, p['rel_bias'])
        keep = min(BAND_PAST, T)
        k_rows = k[:, T - keep:].transpose(0, 2, 1, 3)
        v_rows = v[:, T - keep:].transpose(0, 2, 1, 3)
    else:
        o_a = band_attention_sample(q, k, v, k_cache, v_cache, p['rel_bias'])
        k_rows = k.transpose(0, 2, 1, 3)
        v_rows = v.transpose(0, 2, 1, 3)
    o_b, wkv_new, shift_new = rwkv7_branch(zb, wkv0, shift0, p)
    merged = (jax.nn.sigmoid(zg[..., :D_MODEL]) * (o_a @ p['w_proj_a'])
              + jax.nn.sigmoid(zg[..., D_MODEL:]) * (o_b @ p['w_proj_b']))
    x = x + merged @ p['w_out']
    hn = rmsnorm(x, p['norm2_g']).reshape(Bn * T, D_MODEL)
    x = x + peer(hn, p['w_query'], p['sub_keys'], p['expert_u'], p['expert_v']).reshape(Bn, T, D_MODEL)
    return x, k_rows, v_rows, wkv_new, shift_new


def setup_inputs(seed: int = 0) -> dict:
    key = jax.random.key(seed)
    ks = jax.random.split(key, 32)
    f32 = jnp.float32
    nrm = lambda kk, shape, s: s * jax.random.normal(kk, shape, f32)
    L_A = min(BAND_PAST, PAST_LEN)
    return {
        "x_prompt": nrm(ks[0], (BATCH, SEQ, D_MODEL), 1.0),
        "x_sample": nrm(ks[1], (DEC_BATCH, DEC_SEQ, D_MODEL), 1.0),
        "cache_attn_k": nrm(ks[2], (DEPTH, DEC_BATCH, A_HEADS, L_A, A_HEAD_DIM), 1.0),
        "cache_attn_v": nrm(ks[3], (DEPTH, DEC_BATCH, A_HEADS, L_A, A_HEAD_DIM), 1.0),
        "state_wkv": nrm(ks[4], (DEPTH, DEC_BATCH, B_HEADS, B_HEAD_DIM, B_HEAD_DIM), 0.3),
        "state_shift": nrm(ks[5], (DEPTH, DEC_BATCH, 1, B_COLS), 1.0),
        "norm1_g": 1.0 + nrm(ks[6], (DEPTH, D_MODEL), 0.02),
        "w_in": nrm(ks[7], (DEPTH, D_MODEL, IN_COLS), D_MODEL ** -0.5),
        "rel_bias": nrm(ks[8], (DEPTH, A_HEADS, 2 * REL_CLIP + 1), 0.1),
        "shift_mu": jax.random.uniform(ks[9], (DEPTH, B_COLS), f32),
        "w_decay0": -2.0 + nrm(ks[10], (DEPTH, B_WIDTH), 0.5),
        "w_decay_up": nrm(ks[11], (DEPTH, DECAY_RANK, B_WIDTH), 0.1),
        "a0": nrm(ks[12], (DEPTH, B_WIDTH), 0.5),
        "w_a_up": nrm(ks[13], (DEPTH, AAA_RANK, B_WIDTH), AAA_RANK ** -0.5),
        "w_g_up": nrm(ks[14], (DEPTH, GATE_RANK, B_WIDTH), GATE_RANK ** -0.5),
        "k_k": 0.85 + nrm(ks[15], (DEPTH, B_WIDTH), 0.05),
        "k_a": 1.0 + nrm(ks[16], (DEPTH, B_WIDTH), 0.05),
        "r_k": nrm(ks[17], (DEPTH, B_HEADS, B_HEAD_DIM), 0.1),
        "lnx_g": 1.0 + nrm(ks[18], (DEPTH, B_WIDTH), 0.02),
        "lnx_b": nrm(ks[19], (DEPTH, B_WIDTH), 0.02),
        "w_proj_a": nrm(ks[20], (DEPTH, A_WIDTH, D_MODEL), A_WIDTH ** -0.5),
        "w_proj_b": nrm(ks[21], (DEPTH, B_WIDTH, D_MODEL), B_WIDTH ** -0.5),
        "w_out": nrm(ks[22], (DEPTH, D_MODEL, D_MODEL), D_MODEL ** -0.5),
        "norm2_g": 1.0 + nrm(ks[23], (DEPTH, D_MODEL), 0.02),
        "w_query": nrm(ks[24], (DEPTH, D_MODEL, P_HEADS * P_KEY_DIM), D_MODEL ** -0.5),
        "sub_keys": nrm(ks[25], (DEPTH, P_HEADS, 2, N_KEYS, P_HALF), P_HALF ** -0.5),
        "expert_u": nrm(ks[26], (DEPTH, N_EXPERTS, D_MODEL), D_MODEL ** -0.5),
        "expert_v": nrm(ks[27], (DEPTH, N_EXPERTS, D_MODEL), P_HEADS ** -0.5),
        "normf_g": 1.0 + nrm(ks[28], (D_MODEL,), 0.02),
    }


def reference(x_prompt, x_sample, cache_attn_k, cache_attn_v, state_wkv, state_shift, norm1_g, w_in, rel_bias,
              shift_mu, w_decay0, w_decay_up, a0, w_a_up, w_g_up, k_k, k_a, r_k, lnx_g, lnx_b, w_proj_a, w_proj_b,
              w_out, norm2_g, w_query, sub_keys, expert_u, expert_v, normf_g):
    hp, hs = x_prompt, x_sample
    kp_l, vp_l, wp_l, sp_l, ks_l, vs_l, ws_l, ss_l = [], [], [], [], [], [], [], []
    for l in range(DEPTH):
        p = dict(norm1_g=norm1_g[l], w_in=w_in[l], rel_bias=rel_bias[l], shift_mu=shift_mu[l],
                 w_decay0=w_decay0[l], w_decay_up=w_decay_up[l], a0=a0[l], w_a_up=w_a_up[l], w_g_up=w_g_up[l],
                 k_k=k_k[l], k_a=k_a[l], r_k=r_k[l], lnx_g=lnx_g[l], lnx_b=lnx_b[l], w_proj_a=w_proj_a[l],
                 w_proj_b=w_proj_b[l], w_out=w_out[l], norm2_g=norm2_g[l], w_query=w_query[l],
                 sub_keys=sub_keys[l], expert_u=expert_u[l], expert_v=expert_v[l])
        Bp = hp.shape[0]
        wkv_zero = jnp.zeros((Bp, B_HEADS, B_HEAD_DIM, B_HEAD_DIM), jnp.float32)
        shift_zero = jnp.zeros((Bp, 1, B_COLS), hp.dtype)
        hp, kp, vp, wp, sp = stream_layer(hp, None, None, wkv_zero, shift_zero, p)
        hs, ks_, vs_, ws_, ss_ = stream_layer(hs, cache_attn_k[l], cache_attn_v[l], state_wkv[l], state_shift[l], p)
        kp_l.append(kp); vp_l.append(vp); wp_l.append(wp); sp_l.append(sp)
        ks_l.append(ks_); vs_l.append(vs_); ws_l.append(ws_); ss_l.append(ss_)
    y_prompt = rmsnorm(hp, normf_g)
    y_sample = rmsnorm(hs, normf_g)
    return (y_prompt, y_sample, jnp.stack(kp_l), jnp.stack(vp_l), jnp.stack(wp_l), jnp.stack(sp_l),
            jnp.stack(ks_l), jnp.stack(vs_l), jnp.stack(ws_l), jnp.stack(ss_l))
```

```python
import functools

import jax
import jax.numpy as jnp
from jax import lax
from jax.experimental import pallas as pl
from jax.experimental.pallas import tpu as pltpu

F32 = jnp.float32
BF16 = jnp.bfloat16
HIGHEST = lax.Precision.HIGHEST

D_MODEL = 1024
CHUNK = 64
BAND_CHUNKS = 8
HEADS = 8
HEAD_DIM = 64
WIDTH = HEADS * HEAD_DIM
REL_CLIP = 128
DECAY_RANK = 64
AAA_RANK = 64
GATE_RANK = 128
LNX_EPS = 64e-5
NORM_EPS = 1e-6
P_HEADS = 8
N_KEYS = 128
P_HALF = 64
P_TOPK = 16
A_COLS = 3 * WIDTH
B_COLS = 3 * WIDTH + DECAY_RANK + AAA_RANK + GATE_RANK
G_COLS = 2 * D_MODEL
NEG = -1e30

VMEM_LIMIT = 56 * 1024 * 1024
ROW_TILE = 256
ATT_TILE = BAND_CHUNKS * CHUNK
SCAN_TILE = 128
TOK_TILE = 128
SHIFT_GROUP = 32
PAIRS = HEADS // 2
PAIR_W = 2 * HEAD_DIM


def _params(**kw):
    return pltpu.CompilerParams(vmem_limit_bytes=VMEM_LIMIT, **kw)


def _resident(shape):
    nd = len(shape)
    return pl.BlockSpec(shape, lambda *_: (0,) * nd, pipeline_mode=pl.Buffered(1))


def _sigmoid(x):
    return 1.0 / (1.0 + jnp.exp(-x))


def _dot(a, b, **kw):
    return jnp.dot(a, b, preferred_element_type=F32, **kw)


def _dot_nt(a, b, **kw):
    return lax.dot_general(a, b, (((1,), (1,)), ((), ())), preferred_element_type=F32, **kw)


def _inproj_kernel(x_ref, g_ref, w_ref, za_ref, zb_ref, zg_ref):
    x = x_ref[...]
    y = x * lax.rsqrt(jnp.mean(x * x, axis=-1, keepdims=True) + NORM_EPS) * g_ref[...]
    yb = y.astype(BF16)
    za_ref[...] = _dot(yb, w_ref[:, :A_COLS])
    zb_ref[...] = _dot(yb, w_ref[:, A_COLS:A_COLS + B_COLS])
    zg_ref[...] = _dot(yb, w_ref[:, A_COLS + B_COLS:])


def _in_proj(x, norm_g, w_in):
    m = x.shape[0]
    in_cols = w_in.shape[1]
    row = lambda i: (i, 0)
    return pl.pallas_call(
        _inproj_kernel,
        grid=(m // ROW_TILE,),
        in_specs=[pl.BlockSpec((ROW_TILE, D_MODEL), row), _resident((1, D_MODEL)), _resident((D_MODEL, in_cols))],
        out_specs=[pl.BlockSpec((ROW_TILE, A_COLS), row), pl.BlockSpec((ROW_TILE, B_COLS), row),
                   pl.BlockSpec((ROW_TILE, G_COLS), row)],
        out_shape=[jax.ShapeDtypeStruct((m, A_COLS), F32), jax.ShapeDtypeStruct((m, B_COLS), F32),
                   jax.ShapeDtypeStruct((m, G_COLS), F32)],
        compiler_params=_params(dimension_semantics=("parallel",)),
        name="in_proj",
    )(x, norm_g.reshape(1, D_MODEL), w_in.astype(BF16))


def _softmax_pv(scores, values):
    m = scores[0].max(axis=-1, keepdims=True)
    for s in scores[1:]:
        m = jnp.maximum(m, s.max(axis=-1, keepdims=True))
    acc, den = None, None
    for s, v in zip(scores, values):
        p = jnp.exp(s - m)
        d = p.sum(axis=-1, keepdims=True)
        o = _dot(p.astype(BF16), v)
        acc = o if acc is None else acc + o
        den = d if den is None else den + d
    return acc / den


def _attn_prompt_kernel(q_ref, kp_ref, kc_ref, vp_ref, vc_ref, bias_ref, o_ref):
    first = pl.program_id(1) == 0
    scale = HEAD_DIM ** -0.5
    for h in range(HEADS):
        sl = slice(h * HEAD_DIM, (h + 1) * HEAD_DIM)
        q = q_ref[:, sl].astype(BF16)
        s_prev = _dot_nt(q, kp_ref[:, sl].astype(BF16)) * scale + bias_ref[h, :, :ATT_TILE]
        s_cur = _dot_nt(q, kc_ref[:, sl].astype(BF16)) * scale + bias_ref[h, :, ATT_TILE:]
        s_prev = jnp.where(first, NEG, s_prev)
        o_ref[:, sl] = _softmax_pv([s_prev, s_cur], [vp_ref[:, sl].astype(BF16), vc_ref[:, sl].astype(BF16)])


def _prompt_bias_table(rel_bias):
    qi = jnp.arange(ATT_TILE)[:, None]
    kj = jnp.arange(2 * ATT_TILE)[None, :]
    dist = ATT_TILE + qi - kj
    qc = BAND_CHUNKS + qi // CHUNK
    kc = kj // CHUNK
    valid = (kc >= qc - BAND_CHUNKS) & (kc <= qc)
    b = rel_bias[:, jnp.clip(dist, -REL_CLIP, REL_CLIP) + REL_CLIP].astype(F32)
    return jnp.where(valid[None], b, NEG)


def _attn_prompt(za, rel_bias, batch, seq, m):
    nb = seq // ATT_TILE
    blk = (ATT_TILE, WIDTH)
    cur = lambda col: (lambda b, i: (b * nb + i, col))
    prev = lambda col: (lambda b, i: (b * nb + jnp.maximum(i - 1, 0), col))
    return pl.pallas_call(
        _attn_prompt_kernel,
        grid=(batch, nb),
        in_specs=[pl.BlockSpec(blk, cur(0)), pl.BlockSpec(blk, prev(1)), pl.BlockSpec(blk, cur(1)),
                  pl.BlockSpec(blk, prev(2)), pl.BlockSpec(blk, cur(2)),
                  _resident((HEADS, ATT_TILE, 2 * ATT_TILE))],
        out_specs=pl.BlockSpec(blk, cur(0)),
        out_shape=jax.ShapeDtypeStruct((m, WIDTH), F32),
        compiler_params=_params(dimension_semantics=("parallel", "arbitrary")),
        name="attn_prompt",
    )(za, za, za, za, za, _prompt_bias_table(rel_bias))


def _attn_sample_kernel(q_ref, kn_ref, vn_ref, kc_ref, vc_ref, bc_ref, bn_ref, oin_ref, o_ref):
    del oin_ref
    scale = HEAD_DIM ** -0.5
    for h in range(HEADS):
        sl = slice(h * HEAD_DIM, (h + 1) * HEAD_DIM)
        q = q_ref[:, sl].astype(BF16)
        s_cache = _dot_nt(q, kc_ref[0, h].astype(BF16)) * scale + bc_ref[h]
        s_new = _dot_nt(q, kn_ref[:, sl].astype(BF16)) * scale + bn_ref[h]
        o_ref[:, sl] = _softmax_pv([s_cache, s_new], [vc_ref[0, h].astype(BF16), vn_ref[:, sl].astype(BF16)])


def _attn_sample(za, o_a, k_cache, v_cache, rel_bias, row0):
    nb, _, past, _ = k_cache.shape
    s_len = (za.shape[0] - row0) // nb
    dist = jnp.arange(s_len)[:, None] + past - jnp.arange(past + s_len)[None, :]
    bias = rel_bias[:, jnp.clip(dist, -REL_CLIP, REL_CLIP) + REL_CLIP].astype(F32)
    base = row0 // s_len
    blk = (s_len, WIDTH)
    rows = lambda col: (lambda b: (base + b, col))
    cache = pl.BlockSpec((1, HEADS, past, HEAD_DIM), lambda b: (b, 0, 0, 0))
    return pl.pallas_call(
        _attn_sample_kernel,
        grid=(nb,),
        in_specs=[pl.BlockSpec(blk, rows(0)), pl.BlockSpec(blk, rows(1)), pl.BlockSpec(blk, rows(2)), cache, cache,
                  _resident((HEADS, s_len, past)), _resident((HEADS, s_len, s_len)),
                  pl.BlockSpec(memory_space=pl.ANY)],
        out_specs=pl.BlockSpec(blk, rows(0)),
        out_shape=jax.ShapeDtypeStruct(o_a.shape, F32),
        input_output_aliases={7: 0},
        compiler_params=_params(dimension_semantics=("parallel",)),
        name="attn_sample",
    )(za, za, za, k_cache, v_cache, bias[:, :, :past], bias[:, :, past:], o_a)


def _softplus(x):
    return jnp.maximum(x, 0.0) + jnp.log(1.0 + jnp.exp(-jnp.abs(x)))


def _rwkv_prep_kernel(zb_ref, prev_ref, shift_ref, mu_ref, wd0_ref, wdu_ref, a0_ref, wau_ref, wgu_ref, kk_ref, ka_ref,
                      rk_ref, gsum_ref, r_o, w_o, k_o, v_o, a_o, b_o, g_o, bonus_o, *, seq, prompt_rows):
    i = pl.program_id(0)
    zb = zb_ref[...]
    tm = zb.shape[0]
    row = lax.broadcasted_iota(jnp.int32, zb.shape, 0)
    prev = jnp.where(row == 0, prev_ref[7:8, :], pltpu.roll(zb, 1, 0))
    grow = i * tm + row
    in_prompt = grow < prompt_rows
    starts = (in_prompt & (lax.rem(grow, seq) == 0)) | (~in_prompt & (lax.rem(row, SHIFT_GROUP) == 0))
    ngrp = tm // SHIFT_GROUP
    shift = jnp.broadcast_to(shift_ref[...][:, None, :], (ngrp, SHIFT_GROUP, B_COLS)).reshape(tm, B_COLS)
    prev = jnp.where(starts, shift, prev)
    zm = zb + (prev - zb) * mu_ref[...]
    r = zm[:, 0:WIDTH]
    k = zm[:, WIDTH:2 * WIDTH]
    v = zm[:, 2 * WIDTH:3 * WIDTH]
    lora_in = zm[:, 3 * WIDTH:3 * WIDTH + DECAY_RANK + AAA_RANK]
    gate_in = zm[:, 3 * WIDTH + DECAY_RANK + AAA_RANK:]
    w_log = -_softplus(-(wd0_ref[...] + _dot(jnp.tanh(lora_in), wdu_ref[...], precision=HIGHEST))) - 0.5
    decay = jnp.exp(-jnp.exp(w_log))
    a = _sigmoid(a0_ref[...] + _dot(lora_in, wau_ref[...], precision=HIGHEST))
    g = _dot(_sigmoid(gate_in), wgu_ref[...], precision=HIGHEST)
    kk = k * kk_ref[...]
    kk = kk / jnp.maximum(jnp.sqrt(_dot(kk * kk, gsum_ref[...], precision=HIGHEST)), 1e-12)
    kmod = k * (1.0 + (a - 1.0) * ka_ref[...])
    for ref, val in ((r_o, r), (w_o, decay), (k_o, kmod), (v_o, v), (a_o, -kk), (b_o, kk * a)):
        for pair in range(PAIRS):
            ref[pair] = val[:, pair * PAIR_W:(pair + 1) * PAIR_W]
    g_o[...] = g
    bonus_o[...] = _dot(r * kmod * rk_ref[...], gsum_ref[...], precision=HIGHEST) * v


def _head_sum_matrix(scale):
    h = jnp.arange(WIDTH) // HEAD_DIM
    return jnp.where(h[:, None] == h[None, :], scale, 0.0).astype(F32)


def _rwkv_prep(zb, shift_rows, p, seq, prompt_rows):
    m = zb.shape[0]
    tm = ROW_TILE
    row = lambda i: (i, 0)
    vec = lambda a: a.reshape(1, -1).astype(F32)
    zeros = jnp.zeros((DECAY_RANK, WIDTH), F32)
    wdu = jnp.concatenate([p["w_decay_up"], zeros], axis=0)
    wau = jnp.concatenate([zeros, p["w_a_up"]], axis=0)
    out = jax.ShapeDtypeStruct((m, WIDTH), F32)
    return pl.pallas_call(
        functools.partial(_rwkv_prep_kernel, seq=seq, prompt_rows=prompt_rows),
        grid=(m // tm,),
        in_specs=[pl.BlockSpec((tm, B_COLS), row),
                  pl.BlockSpec((8, B_COLS), lambda i: (jnp.maximum(i * (tm // 8) - 1, 0), 0)),
                  pl.BlockSpec((tm // SHIFT_GROUP, B_COLS), row),
                  _resident((1, B_COLS)), _resident((1, WIDTH)), _resident((DECAY_RANK + AAA_RANK, WIDTH)),
                  _resident((1, WIDTH)), _resident((DECAY_RANK + AAA_RANK, WIDTH)), _resident((GATE_RANK, WIDTH)),
                  _resident((1, WIDTH)), _resident((1, WIDTH)), _resident((1, WIDTH)), _resident((WIDTH, WIDTH))],
        out_specs=[pl.BlockSpec((PAIRS, tm, PAIR_W), lambda i: (0, i, 0))] * 6 + [pl.BlockSpec((tm, WIDTH), row)] * 2,
        out_shape=[jax.ShapeDtypeStruct((PAIRS, m, PAIR_W), F32)] * 6 + [out] * 2,
        compiler_params=_params(dimension_semantics=("parallel",)),
        name="rwkv_prep",
    )(zb, zb, shift_rows, vec(p["shift_mu"]), vec(p["w_decay0"]), wdu, vec(p["a0"]), wau, p["w_g_up"],
      vec(p["k_k"]), vec(p["k_a"]), vec(p["r_k"]), _head_sum_matrix(1.0))


def _scan_kernel(r_ref, w_ref, k_ref, v_ref, a_ref, b_ref, s0_ref, yin_ref, y_ref, st_ref, s_scr, *, steps, unroll):
    del yin_ref
    tb = pl.program_id(1)

    @pl.when(tb == 0)
    def _():
        s_scr[...] = s0_ref[0]

    lane = lax.broadcasted_iota(jnp.int32, (HEAD_DIM, PAIR_W), 1)
    sub = lax.broadcasted_iota(jnp.int32, (HEAD_DIM, PAIR_W), 0)
    lo = lane < HEAD_DIM
    diag = (lane & (HEAD_DIM - 1)) == sub

    def head_sums(x):
        s_lo = jnp.sum(jnp.where(lo, x, 0.0), axis=1, keepdims=True)
        s_hi = jnp.sum(jnp.where(lo, 0.0, x), axis=1, keepdims=True)
        return jnp.where(lo, s_lo, s_hi)

    sub8 = lax.broadcasted_iota(jnp.int32, (8, PAIR_W), 0)

    def group(g, carry):
        base = pl.multiple_of(g * 8, 8)
        for p in range(PAIRS):
            r, w, k, v, a, b = (ref[p, pl.ds(base, 8), :] for ref in (r_ref, w_ref, k_ref, v_ref, a_ref, b_ref))
            s = s_scr[p]
            y = jnp.zeros((8, PAIR_W), F32)
            for j in range(8):
                row = lambda x: x[j:j + 1, :]
                sa = head_sums(s * row(a))
                v_col = head_sums(jnp.where(diag, row(v), 0.0))
                s = s * row(w) + sa * row(b) + v_col * row(k)
                y_col = head_sums(s * row(r))
                y = jnp.where(sub8 == j, jnp.sum(jnp.where(diag, y_col, 0.0), axis=0, keepdims=True), y)
            s_scr[p] = s
            y_ref[p, pl.ds(base, 8), :] = y
        return carry

    lax.fori_loop(0, steps // 8, group, 0, unroll=unroll)

    @pl.when(tb == pl.num_programs(1) - 1)
    def _():
        st_ref[0] = s_scr[...]


def _pair_state(s):
    b = s.shape[0]
    return s.reshape(b, PAIRS, 2, HEAD_DIM, HEAD_DIM).transpose(0, 1, 3, 2, 4).reshape(b, PAIRS, HEAD_DIM, PAIR_W)


def _unpair_state(s):
    b = s.shape[0]
    return s.reshape(b, PAIRS, HEAD_DIM, 2, HEAD_DIM).transpose(0, 1, 3, 2, 4).reshape(b, HEADS, HEAD_DIM, HEAD_DIM)


def _wkv_scan(r, w, k, v, a, b, s0, y_in, row0, batch, seq, steps):
    m = r.shape[1]
    nb = seq // steps
    base = row0 // steps
    rows = lambda bi, ti: (0, base + bi * nb + ti, 0)
    blk = pl.BlockSpec((PAIRS, steps, PAIR_W), rows)
    state = pl.BlockSpec((1, PAIRS, HEAD_DIM, PAIR_W), lambda bi, ti: (bi, 0, 0, 0))
    y_spec = [] if y_in is None else [pl.BlockSpec(memory_space=pl.ANY)]
    kern = _scan_kernel if y_in is not None else (lambda *refs, **kw: _scan_kernel(*refs[:7], None, *refs[7:], **kw))
    y, st = pl.pallas_call(
        functools.partial(kern, steps=steps, unroll=1),
        grid=(batch, nb),
        in_specs=[blk] * 6 + [state] + y_spec,
        out_specs=[blk, state],
        out_shape=[jax.ShapeDtypeStruct((PAIRS, m, PAIR_W), F32),
                   jax.ShapeDtypeStruct((batch, PAIRS, HEAD_DIM, PAIR_W), F32)],
        scratch_shapes=[pltpu.VMEM((PAIRS, HEAD_DIM, PAIR_W), F32)],
        input_output_aliases={} if y_in is None else {7: 0},
        compiler_params=_params(dimension_semantics=("parallel", "arbitrary")),
        name="wkv_scan",
    )(r, w, k, v, a, b, _pair_state(s0.astype(F32)), *([] if y_in is None else [y_in]))
    return y, _unpair_state(st)


def _postmix_kernel(x_ref, oa_ref, y_ref, bonus_ref, g_ref, zg_ref, lng_ref, lnb_ref, gmean_ref, wpa_ref, wpb_ref,
                    wout_ref, n2g_ref, wch_ref, wcl_ref, x1_ref, hn_ref, sc_ref):
    y = jnp.concatenate([y_ref[pair] for pair in range(PAIRS)], axis=1)
    mu = _dot(y, gmean_ref[...], precision=HIGHEST)
    d = y - mu
    var = _dot(d * d, gmean_ref[...], precision=HIGHEST)
    yn = d * lax.rsqrt(var + LNX_EPS) * lng_ref[...] + lnb_ref[...]
    ob = (yn + bonus_ref[...]) * g_ref[...]
    pa = _dot(oa_ref[...].astype(BF16), wpa_ref[...])
    pb = _dot(ob.astype(BF16), wpb_ref[...])
    merged = _sigmoid(zg_ref[:, :D_MODEL]) * pa + _sigmoid(zg_ref[:, D_MODEL:]) * pb
    x1 = x_ref[...] + _dot(merged.astype(BF16), wout_ref[...])
    x1_ref[...] = x1
    hn = x1 * lax.rsqrt(jnp.mean(x1 * x1, axis=-1, keepdims=True) + NORM_EPS) * n2g_ref[...]
    hn_ref[...] = hn
    hh = hn.astype(BF16)
    hl = (hn - hh.astype(F32)).astype(BF16)
    sc_ref[...] = _dot_nt(wch_ref[...], hh) + (_dot_nt(wcl_ref[...], hh) + _dot_nt(wch_ref[...], hl))


def _post_mix(x, o_a, y, bonus, g, zg, p, wc_t):
    m = x.shape[0]
    tm = ROW_TILE
    row = lambda i: (i, 0)
    nsc = wc_t.shape[0]
    vec = lambda a: a.reshape(1, -1).astype(F32)
    wch = wc_t.astype(BF16)
    wcl = (wc_t - wch.astype(F32)).astype(BF16)
    wide = pl.BlockSpec((tm, D_MODEL), row)
    half = pl.BlockSpec((tm, WIDTH), row)
    return pl.pallas_call(
        _postmix_kernel,
        grid=(m // tm,),
        in_specs=[wide, half, pl.BlockSpec((PAIRS, tm, PAIR_W), lambda i: (0, i, 0)), half, half,
                  pl.BlockSpec((tm, G_COLS), row),
                  _resident((1, WIDTH)), _resident((1, WIDTH)), _resident((WIDTH, WIDTH)),
                  _resident((WIDTH, D_MODEL)), _resident((WIDTH, D_MODEL)), _resident((D_MODEL, D_MODEL)),
                  _resident((1, D_MODEL)), _resident((nsc, D_MODEL)), _resident((nsc, D_MODEL))],
        out_specs=[wide, wide, pl.BlockSpec((nsc, tm), lambda i: (0, i))],
        out_shape=[jax.ShapeDtypeStruct((m, D_MODEL), F32), jax.ShapeDtypeStruct((m, D_MODEL), F32),
                   jax.ShapeDtypeStruct((nsc, m), F32)],
        compiler_params=_params(dimension_semantics=("parallel",)),
        name="post_mix",
    )(x, o_a, y, bonus, g, zg, vec(p["lnx_g"]), vec(p["lnx_b"]), _head_sum_matrix(1.0 / HEAD_DIM),
      p["w_proj_a"].astype(BF16), p["w_proj_b"].astype(BF16), p["w_out"].astype(BF16), vec(p["norm2_g"]), wch, wcl)


def _score_weight_kernel(wq_ref, sk_ref, o_ref):
    for c in range(2):
        wq = wq_ref[:, c * P_HALF:(c + 1) * P_HALF]
        o_ref[c * N_KEYS:(c + 1) * N_KEYS, :] = _dot_nt(sk_ref[0, c], wq, precision=HIGHEST)


def _score_weights(w_query, sub_keys):
    return pl.pallas_call(
        _score_weight_kernel,
        grid=(P_HEADS,),
        in_specs=[pl.BlockSpec((D_MODEL, 2 * P_HALF), lambda h: (0, h)),
                  pl.BlockSpec((1, 2, N_KEYS, P_HALF), lambda h: (h, 0, 0, 0))],
        out_specs=pl.BlockSpec((2 * N_KEYS, D_MODEL), lambda h: (h, 0)),
        out_shape=jax.ShapeDtypeStruct((P_HEADS * 2 * N_KEYS, D_MODEL), F32),
        compiler_params=_params(dimension_semantics=("parallel",)),
        name="score_weights",
    )(w_query, sub_keys)


def _extract_topk(vals, ids, n, payload=None):
    top_v, top_i = [], []
    big = jnp.int32(vals.shape[0])
    for _ in range(n):
        m = jnp.max(vals, axis=0, keepdims=True)
        pick = jnp.min(jnp.where(vals == m, ids, big), axis=0, keepdims=True)
        sel = ids == pick
        top_v.append(m)
        top_i.append(pick if payload is None else jnp.max(jnp.where(sel, payload, -1), axis=0, keepdims=True))
        vals = jnp.where(sel, -jnp.inf, vals)
    return top_v, top_i


def _topk_kernel(sc_ref, eidx_ref, gate_ref):
    lanes = sc_ref.shape[1]
    key_id = lax.broadcasted_iota(jnp.int32, (N_KEYS, lanes), 0)
    cand_id = lax.broadcasted_iota(jnp.int32, (P_TOPK * P_TOPK, lanes), 0)

    def head(h, carry):
        base = pl.multiple_of(h * 2 * N_KEYS, 2 * N_KEYS)
        v0, i0 = _extract_topk(sc_ref[pl.ds(base, N_KEYS), :], key_id, P_TOPK)
        v1, i1 = _extract_topk(sc_ref[pl.ds(base + N_KEYS, N_KEYS), :], key_id, P_TOPK)
        v1 = jnp.concatenate(v1, axis=0)
        i1 = jnp.concatenate(i1, axis=0)
        cand = jnp.concatenate([va + v1 for va in v0], axis=0)
        cidx = jnp.concatenate([ia * N_KEYS + i1 for ia in i0], axis=0)
        fv, fe = _extract_topk(cand, cand_id, P_TOPK, payload=cidx)
        fv = jnp.concatenate(fv, axis=0)
        e = jnp.exp(fv - fv[0:1])
        out = pl.multiple_of(h * P_TOPK, P_TOPK)
        gate_ref[pl.ds(out, P_TOPK), :] = e / jnp.sum(e, axis=0, keepdims=True)
        eidx_ref[pl.ds(out, P_TOPK), :] = jnp.concatenate(fe, axis=0)
        return carry

    lax.fori_loop(0, P_HEADS, head, 0)


def _topk(scores):
    nsc, m = scores.shape
    tl = 2 * TOK_TILE
    col = lambda i: (0, i)
    return pl.pallas_call(
        _topk_kernel,
        grid=(m // tl,),
        in_specs=[pl.BlockSpec((nsc, tl), col)],
        out_specs=[pl.BlockSpec((P_HEADS * P_TOPK, tl), col)] * 2,
        out_shape=[jax.ShapeDtypeStruct((P_HEADS * P_TOPK, m), jnp.int32),
                   jax.ShapeDtypeStruct((P_HEADS * P_TOPK, m), F32)],
        compiler_params=_params(dimension_semantics=("parallel",)),
        name="topk",
    )(scores)


N_SEL = P_HEADS * P_TOPK
ROW_SHAPE = (8, D_MODEL // 8)


def _pack_table(t):
    bits = lax.bitcast_convert_type(t.astype(BF16), jnp.uint16).astype(jnp.uint32)
    packed = bits[0::2] | (bits[1::2] << 16)
    return packed.reshape(t.shape[0] // 2, *ROW_SHAPE)


def _expert_row(tbl_ref, e):
    w = tbl_ref[e >> 1]
    shift = ((1 - (e & 1)) * 16).astype(jnp.uint32)
    return pltpu.bitcast((w << shift) & jnp.uint32(0xFFFF0000), F32)


def _sublane_fold(x, y, step, mask):
    return jnp.where(mask, x + pltpu.roll(x, 8 - step, 0), y + pltpu.roll(y, step, 0))


def _peer_u_kernel(idx_ref, hn_ref, gate_ref, tbl_ref, coef_ref, hid_scr):
    toks = hn_ref.shape[0]
    sub = lax.broadcasted_iota(jnp.int32, ROW_SHAPE, 0)
    m4, m2, m1 = sub < 4, (sub & 3) < 2, (sub & 1) == 0
    lane_tok = lax.broadcasted_iota(jnp.int32, (8, toks), 1)
    hid_scr[...] = jnp.zeros_like(hid_scr)

    def token(t, carry):
        x = hn_ref[t]
        for g in range(N_SEL // 8):
            prod = [_expert_row(tbl_ref, idx_ref[t, g * 8 + j]) * x for j in range(8)]
            z = [_sublane_fold(prod[a], prod[a + 4], 4, m4) for a in (0, 2, 1, 3)]
            w0 = _sublane_fold(z[0], z[1], 2, m2)
            w1 = _sublane_fold(z[2], z[3], 2, m2)
            folded = _sublane_fold(w0, w1, 1, m1)
            col = jnp.sum(folded, axis=1, keepdims=True)
            rows = slice(g * 8, (g + 1) * 8)
            hid_scr[rows, :] = jnp.where(lane_tok == t, col, hid_scr[rows, :])
        return carry

    lax.fori_loop(0, toks, token, 0)
    hid = hid_scr[...]
    coef_ref[...] = gate_ref[...] * (0.5 * hid * (1.0 + lax.erf(hid * (2.0 ** -0.5))))


def _peer_u(eidx_t, hn3, gate, table):
    m = hn3.shape[0]
    tt = TOK_TILE
    return pl.pallas_call(
        _peer_u_kernel,
        grid=(m // tt,),
        in_specs=[pl.BlockSpec((tt, N_SEL), lambda i: (i, 0), memory_space=pltpu.SMEM),
                  pl.BlockSpec((tt, *ROW_SHAPE), lambda i: (i, 0, 0)),
                  pl.BlockSpec((N_SEL, tt), lambda i: (0, i)),
                  _resident(table.shape)],
        out_specs=pl.BlockSpec((N_SEL, tt), lambda i: (0, i)),
        out_shape=jax.ShapeDtypeStruct((N_SEL, m), F32),
        scratch_shapes=[pltpu.VMEM((N_SEL, tt), F32)],
        compiler_params=_params(dimension_semantics=("parallel",)),
        name="peer_u",
    )(eidx_t, hn3, gate, table)


def _peer_v_kernel(idx_ref, coef_ref, x_ref, tbl_ref, g_ref, y_ref):
    toks = x_ref.shape[0]
    n_acc = 4

    def token(t, carry):
        acc = [jnp.zeros(ROW_SHAPE, F32) for _ in range(n_acc)]
        for j in range(N_SEL):
            acc[j % n_acc] = acc[j % n_acc] + coef_ref[t, j] * _expert_row(tbl_ref, idx_ref[t, j])
        x2 = x_ref[t] + ((acc[0] + acc[1]) + (acc[2] + acc[3]))
        ms = jnp.sum(jnp.sum(x2 * x2, axis=1, keepdims=True), axis=0, keepdims=True) * (1.0 / D_MODEL)
        y_ref[t] = x2 * lax.rsqrt(ms + NORM_EPS) * g_ref[...]
        return carry

    lax.fori_loop(0, toks, token, 0)


def _peer_v(eidx_t, coef_t, x3, table, normf_g):
    m = x3.shape[0]
    tt = TOK_TILE
    smem = pl.BlockSpec((tt, N_SEL), lambda i: (i, 0), memory_space=pltpu.SMEM)
    tok = pl.BlockSpec((tt, *ROW_SHAPE), lambda i: (i, 0, 0))
    return pl.pallas_call(
        _peer_v_kernel,
        grid=(m // tt,),
        in_specs=[smem, smem, tok, _resident(table.shape), _resident(ROW_SHAPE)],
        out_specs=tok,
        out_shape=jax.ShapeDtypeStruct(x3.shape, F32),
        compiler_params=_params(dimension_semantics=("parallel",)),
        name="peer_v",
    )(eidx_t, coef_t, x3, table, normf_g.reshape(ROW_SHAPE).astype(F32))


def kernel(x_prompt, x_sample, cache_attn_k, cache_attn_v, state_wkv, state_shift, norm1_g, w_in, rel_bias, shift_mu,
           w_decay0, w_decay_up, a0, w_a_up, w_g_up, k_k, k_a, r_k, lnx_g, lnx_b, w_proj_a, w_proj_b, w_out, norm2_g,
           w_query, sub_keys, expert_u, expert_v, normf_g):
    depth = norm1_g.shape[0]
    assert depth == 1, "single-layer step"
    batch, seq, _ = x_prompt.shape
    dbatch, dseq, _ = x_sample.shape
    assert dseq == SHIFT_GROUP and seq % ATT_TILE == 0
    mp, ms = batch * seq, dbatch * dseq
    m = mp + ms
    assert m % (2 * TOK_TILE) == 0 and mp % ROW_TILE == 0
    p = dict(shift_mu=shift_mu[0], w_decay0=w_decay0[0], w_decay_up=w_decay_up[0], a0=a0[0], w_a_up=w_a_up[0],
             w_g_up=w_g_up[0], k_k=k_k[0], k_a=k_a[0], r_k=r_k[0].reshape(-1), lnx_g=lnx_g[0], lnx_b=lnx_b[0],
             w_proj_a=w_proj_a[0], w_proj_b=w_proj_b[0], w_out=w_out[0], norm2_g=norm2_g[0])

    x = jnp.concatenate([x_prompt.reshape(mp, D_MODEL), x_sample.reshape(ms, D_MODEL)], axis=0)
    za, zb, zg = _in_proj(x, norm1_g[0], w_in[0])

    o_a = _attn_prompt(za, rel_bias[0], batch, seq, m)
    o_a = _attn_sample(za, o_a, cache_attn_k[0], cache_attn_v[0], rel_bias[0], mp)

    shift_rows = jnp.concatenate([jnp.zeros((mp // SHIFT_GROUP, B_COLS), F32), state_shift[0].reshape(dbatch, B_COLS)])
    r, w, k, v, a, b, g, bonus = _rwkv_prep(zb, shift_rows, p, seq, mp)
    y, wkv_p = _wkv_scan(r, w, k, v, a, b, jnp.zeros((batch, HEADS, HEAD_DIM, HEAD_DIM), F32), None, 0, batch, seq,
                         SCAN_TILE)
    y, wkv_s = _wkv_scan(r, w, k, v, a, b, state_wkv[0], y, mp, dbatch, dseq, dseq)

    wc_t = _score_weights(w_query[0], sub_keys[0])
    x1, hn, scores = _post_mix(x, o_a, y, bonus, g, zg, p, wc_t)
    eidx, gate = _topk(scores)
    eidx_t = eidx.T
    coef = _peer_u(eidx_t, hn.reshape(m, *ROW_SHAPE), gate, _pack_table(expert_u[0]))
    out = _peer_v(eidx_t, coef.T, x1.reshape(m, *ROW_SHAPE), _pack_table(expert_v[0]), normf_g).reshape(m, D_MODEL)

    heads = lambda t, n: t.reshape(n, -1, HEADS, HEAD_DIM).transpose(0, 2, 1, 3)[None]
    keep = min(BAND_CHUNKS * CHUNK, seq)
    zp = za[:mp].reshape(batch, seq, A_COLS)[:, seq - keep:]
    zs = za[mp:].reshape(dbatch, dseq, A_COLS)
    return (out[:mp].reshape(batch, seq, D_MODEL), out[mp:].reshape(dbatch, dseq, D_MODEL),
            heads(zp[..., WIDTH:2 * WIDTH], batch), heads(zp[..., 2 * WIDTH:], batch),
            wkv_p[None], zb[:mp].reshape(batch, seq, B_COLS)[:, -1:][None],
            heads(zs[..., WIDTH:2 * WIDTH], dbatch), heads(zs[..., 2 * WIDTH:], dbatch),
            wkv_s[None], zb[mp:].reshape(dbatch, dseq, B_COLS)[:, -1:][None])
```

```python
import functools

import jax
import jax.numpy as jnp
from jax import lax
from jax.experimental import pallas as pl
from jax.experimental.pallas import tpu as pltpu

F32 = jnp.float32
BF16 = jnp.bfloat16
HIGHEST = lax.Precision.HIGHEST

D_MODEL = 1024
CHUNK = 64
BAND_CHUNKS = 8
HEADS = 8
HEAD_DIM = 64
WIDTH = HEADS * HEAD_DIM
REL_CLIP = 128
DECAY_RANK = 64
AAA_RANK = 64
GATE_RANK = 128
LNX_EPS = 64e-5
NORM_EPS = 1e-6
P_HEADS = 8
N_KEYS = 128
P_HALF = 64
P_TOPK = 16
A_COLS = 3 * WIDTH
B_COLS = 3 * WIDTH + DECAY_RANK + AAA_RANK + GATE_RANK
G_COLS = 2 * D_MODEL
NEG = -1e30

VMEM_LIMIT = 56 * 1024 * 1024
ROW_TILE = 256
ATT_TILE = BAND_CHUNKS * CHUNK
SCAN_TILE = 128
TOK_TILE = 128
SHIFT_GROUP = 32
PAIRS = HEADS // 2
PAIR_W = 2 * HEAD_DIM


def _params(**kw):
    return pltpu.CompilerParams(vmem_limit_bytes=VMEM_LIMIT, **kw)


def _resident(shape):
    nd = len(shape)
    return pl.BlockSpec(shape, lambda *_: (0,) * nd, pipeline_mode=pl.Buffered(1))


def _sigmoid(x):
    return 1.0 / (1.0 + jnp.exp(-x))


def _dot(a, b, **kw):
    return jnp.dot(a, b, preferred_element_type=F32, **kw)


def _dot_nt(a, b, **kw):
    return lax.dot_general(a, b, (((1,), (1,)), ((), ())), preferred_element_type=F32, **kw)


def _inproj_kernel(x_ref, g_ref, w_ref, za_ref, zb_ref, zg_ref):
    x = x_ref[...]
    y = x * lax.rsqrt(jnp.mean(x * x, axis=-1, keepdims=True) + NORM_EPS) * g_ref[...]
    yb = y.astype(BF16)
    za_ref[...] = _dot(yb, w_ref[:, :A_COLS])
    zb_ref[...] = _dot(yb, w_ref[:, A_COLS:A_COLS + B_COLS])
    zg_ref[...] = _dot(yb, w_ref[:, A_COLS + B_COLS:])


def _in_proj(x, norm_g, w_in):
    m = x.shape[0]
    in_cols = w_in.shape[1]
    row = lambda i: (i, 0)
    return pl.pallas_call(
        _inproj_kernel,
        grid=(m // ROW_TILE,),
        in_specs=[pl.BlockSpec((ROW_TILE, D_MODEL), row), _resident((1, D_MODEL)), _resident((D_MODEL, in_cols))],
        out_specs=[pl.BlockSpec((ROW_TILE, A_COLS), row), pl.BlockSpec((ROW_TILE, B_COLS), row),
                   pl.BlockSpec((ROW_TILE, G_COLS), row)],
        out_shape=[jax.ShapeDtypeStruct((m, A_COLS), F32), jax.ShapeDtypeStruct((m, B_COLS), F32),
                   jax.ShapeDtypeStruct((m, G_COLS), F32)],
        compiler_params=_params(dimension_semantics=("parallel",)),
        name="in_proj",
    )(x, norm_g.reshape(1, D_MODEL), w_in.astype(BF16))


def _softmax_pv(scores, values):
    m = scores[0].max(axis=-1, keepdims=True)
    for s in scores[1:]:
        m = jnp.maximum(m, s.max(axis=-1, keepdims=True))
    acc, den = None, None
    for s, v in zip(scores, values):
        p = jnp.exp(s - m)
        d = p.sum(axis=-1, keepdims=True)
        o = _dot(p.astype(BF16), v)
        acc = o if acc is None else acc + o
        den = d if den is None else den + d
    return acc / den


def _attn_prompt_kernel(q_ref, kp_ref, kc_ref, vp_ref, vc_ref, bias_ref, o_ref):
    first = pl.program_id(1) == 0
    scale = HEAD_DIM ** -0.5
    for h in range(HEADS):
        sl = slice(h * HEAD_DIM, (h + 1) * HEAD_DIM)
        q = q_ref[:, sl].astype(BF16)
        s_prev = _dot_nt(q, kp_ref[:, sl].astype(BF16)) * scale + bias_ref[h, :, :ATT_TILE]
        s_cur = _dot_nt(q, kc_ref[:, sl].astype(BF16)) * scale + bias_ref[h, :, ATT_TILE:]
        s_prev = jnp.where(first, NEG, s_prev)
        o_ref[:, sl] = _softmax_pv([s_prev, s_cur], [vp_ref[:, sl].astype(BF16), vc_ref[:, sl].astype(BF16)])


def _prompt_bias_table(rel_bias):
    nq, nk = ATT_TILE, 2 * ATT_TILE
    qi = jnp.arange(nq)[:, None]
    kj = jnp.arange(nk)[None, :]
    valid = (kj // CHUNK >= qi // CHUNK) & (kj // CHUNK <= BAND_CHUNKS + qi // CHUNK)
    span = nq + nk - 1
    dist = nq + (nq - 1) - jnp.arange(span)
    diag_vec = rel_bias[:, jnp.clip(dist, -REL_CLIP, REL_CLIP) + REL_CLIP].astype(F32)
    diag_vec = jnp.pad(diag_vec, ((0, 0), (0, 1)))
    rows = jnp.broadcast_to(diag_vec[:, None, :], (HEADS, nq, span + 1)).reshape(HEADS, nq * (span + 1))
    skew = rows[:, :nq * span].reshape(HEADS, nq, span)
    return jnp.where(valid[None], skew[:, :, nq - 1:nq - 1 + nk], NEG)


def _attn_prompt(za, rel_bias, batch, seq, m):
    nb = seq // ATT_TILE
    blk = (ATT_TILE, WIDTH)
    cur = lambda col: (lambda b, i: (b * nb + i, col))
    prev = lambda col: (lambda b, i: (b * nb + jnp.maximum(i - 1, 0), col))
    return pl.pallas_call(
        _attn_prompt_kernel,
        grid=(batch, nb),
        in_specs=[pl.BlockSpec(blk, cur(0)), pl.BlockSpec(blk, prev(1)), pl.BlockSpec(blk, cur(1)),
                  pl.BlockSpec(blk, prev(2)), pl.BlockSpec(blk, cur(2)),
                  _resident((HEADS, ATT_TILE, 2 * ATT_TILE))],
        out_specs=pl.BlockSpec(blk, cur(0)),
        out_shape=jax.ShapeDtypeStruct((m, WIDTH), F32),
        compiler_params=_params(dimension_semantics=("parallel", "arbitrary")),
        name="attn_prompt",
    )(za, za, za, za, za, _prompt_bias_table(rel_bias))


def _attn_sample_kernel(q_ref, kn_ref, vn_ref, kc_ref, vc_ref, bc_ref, bn_ref, oin_ref, o_ref):
    del oin_ref
    scale = HEAD_DIM ** -0.5
    for h in range(HEADS):
        sl = slice(h * HEAD_DIM, (h + 1) * HEAD_DIM)
        q = q_ref[:, sl].astype(BF16)
        s_cache = _dot_nt(q, kc_ref[0, h].astype(BF16)) * scale + bc_ref[h]
        s_new = _dot_nt(q, kn_ref[:, sl].astype(BF16)) * scale + bn_ref[h]
        o_ref[:, sl] = _softmax_pv([s_cache, s_new], [vc_ref[0, h].astype(BF16), vn_ref[:, sl].astype(BF16)])


def _attn_sample(za, o_a, k_cache, v_cache, rel_bias, row0):
    nb, _, past, _ = k_cache.shape
    s_len = (za.shape[0] - row0) // nb
    dist = jnp.arange(s_len)[:, None] + past - jnp.arange(past + s_len)[None, :]
    bias = rel_bias[:, jnp.clip(dist, -REL_CLIP, REL_CLIP) + REL_CLIP].astype(F32)
    base = row0 // s_len
    blk = (s_len, WIDTH)
    rows = lambda col: (lambda b: (base + b, col))
    cache = pl.BlockSpec((1, HEADS, past, HEAD_DIM), lambda b: (b, 0, 0, 0))
    return pl.pallas_call(
        _attn_sample_kernel,
        grid=(nb,),
        in_specs=[pl.BlockSpec(blk, rows(0)), pl.BlockSpec(blk, rows(1)), pl.BlockSpec(blk, rows(2)), cache, cache,
                  _resident((HEADS, s_len, past)), _resident((HEADS, s_len, s_len)),
                  pl.BlockSpec(memory_space=pl.ANY)],
        out_specs=pl.BlockSpec(blk, rows(0)),
        out_shape=jax.ShapeDtypeStruct(o_a.shape, F32),
        input_output_aliases={7: 0},
        compiler_params=_params(dimension_semantics=("parallel",)),
        name="attn_sample",
    )(za, za, za, k_cache, v_cache, bias[:, :, :past], bias[:, :, past:], o_a)


def _softplus(x):
    return jnp.maximum(x, 0.0) + jnp.log(1.0 + jnp.exp(-jnp.abs(x)))


def _rwkv_prep_kernel(zb_ref, prev_ref, shift_ref, mu_ref, wd0_ref, wdu_ref, a0_ref, wau_ref, wgu_ref, kk_ref, ka_ref,
                      rk_ref, gsum_ref, r_o, w_o, k_o, v_o, a_o, b_o, g_o, bonus_o, *, seq, prompt_rows):
    i = pl.program_id(0)
    zb = zb_ref[...]
    tm = zb.shape[0]
    row = lax.broadcasted_iota(jnp.int32, zb.shape, 0)
    prev = jnp.where(row == 0, prev_ref[7:8, :], pltpu.roll(zb, 1, 0))
    grow = i * tm + row
    in_prompt = grow < prompt_rows
    starts = (in_prompt & (lax.rem(grow, seq) == 0)) | (~in_prompt & (lax.rem(row, SHIFT_GROUP) == 0))
    ngrp = tm // SHIFT_GROUP
    shift = jnp.broadcast_to(shift_ref[...][:, None, :], (ngrp, SHIFT_GROUP, B_COLS)).reshape(tm, B_COLS)
    prev = jnp.where(starts, shift, prev)
    zm = zb + (prev - zb) * mu_ref[...]
    r = zm[:, 0:WIDTH]
    k = zm[:, WIDTH:2 * WIDTH]
    v = zm[:, 2 * WIDTH:3 * WIDTH]
    lora_in = zm[:, 3 * WIDTH:3 * WIDTH + DECAY_RANK + AAA_RANK]
    gate_in = zm[:, 3 * WIDTH + DECAY_RANK + AAA_RANK:]
    w_log = -_softplus(-(wd0_ref[...] + _dot(jnp.tanh(lora_in), wdu_ref[...], precision=HIGHEST))) - 0.5
    decay = jnp.exp(-jnp.exp(w_log))
    a = _sigmoid(a0_ref[...] + _dot(lora_in, wau_ref[...], precision=HIGHEST))
    g = _dot(_sigmoid(gate_in), wgu_ref[...], precision=HIGHEST)
    kk = k * kk_ref[...]
    kk = kk / jnp.maximum(jnp.sqrt(_dot(kk * kk, gsum_ref[...], precision=HIGHEST)), 1e-12)
    kmod = k * (1.0 + (a - 1.0) * ka_ref[...])
    for ref, val in ((r_o, r), (w_o, decay), (k_o, kmod), (v_o, v), (a_o, -kk), (b_o, kk * a)):
        for pair in range(PAIRS):
            ref[pair] = val[:, pair * PAIR_W:(pair + 1) * PAIR_W]
    g_o[...] = g
    bonus_o[...] = _dot(r * kmod * rk_ref[...], gsum_ref[...], precision=HIGHEST) * v


def _head_sum_matrix(scale):
    h = jnp.arange(WIDTH) // HEAD_DIM
    return jnp.where(h[:, None] == h[None, :], scale, 0.0).astype(F32)


def _rwkv_prep(zb, shift_rows, p, seq, prompt_rows):
    m = zb.shape[0]
    tm = ROW_TILE
    row = lambda i: (i, 0)
    vec = lambda a: a.reshape(1, -1).astype(F32)
    zeros = jnp.zeros((DECAY_RANK, WIDTH), F32)
    wdu = jnp.concatenate([p["w_decay_up"], zeros], axis=0)
    wau = jnp.concatenate([zeros, p["w_a_up"]], axis=0)
    out = jax.ShapeDtypeStruct((m, WIDTH), F32)
    return pl.pallas_call(
        functools.partial(_rwkv_prep_kernel, seq=seq, prompt_rows=prompt_rows),
        grid=(m // tm,),
        in_specs=[pl.BlockSpec((tm, B_COLS), row),
                  pl.BlockSpec((8, B_COLS), lambda i: (jnp.maximum(i * (tm // 8) - 1, 0), 0)),
                  pl.BlockSpec((tm // SHIFT_GROUP, B_COLS), row),
                  _resident((1, B_COLS)), _resident((1, WIDTH)), _resident((DECAY_RANK + AAA_RANK, WIDTH)),
                  _resident((1, WIDTH)), _resident((DECAY_RANK + AAA_RANK, WIDTH)), _resident((GATE_RANK, WIDTH)),
                  _resident((1, WIDTH)), _resident((1, WIDTH)), _resident((1, WIDTH)), _resident((WIDTH, WIDTH))],
        out_specs=[pl.BlockSpec((PAIRS, tm, PAIR_W), lambda i: (0, i, 0))] * 6 + [pl.BlockSpec((tm, WIDTH), row)] * 2,
        out_shape=[jax.ShapeDtypeStruct((PAIRS, m, PAIR_W), F32)] * 6 + [out] * 2,
        compiler_params=_params(dimension_semantics=("parallel",)),
        name="rwkv_prep",
    )(zb, zb, shift_rows, vec(p["shift_mu"]), vec(p["w_decay0"]), wdu, vec(p["a0"]), wau, p["w_g_up"],
      vec(p["k_k"]), vec(p["k_a"]), vec(p["r_k"]), _head_sum_matrix(1.0))


def _scan_kernel(r_ref, w_ref, k_ref, v_ref, a_ref, b_ref, s0_ref, yin_ref, y_ref, st_ref, s_scr, *, steps, unroll):
    del yin_ref
    tb = pl.program_id(1)

    @pl.when(tb == 0)
    def _():
        s_scr[...] = s0_ref[0]

    lane = lax.broadcasted_iota(jnp.int32, (HEAD_DIM, PAIR_W), 1)
    sub = lax.broadcasted_iota(jnp.int32, (HEAD_DIM, PAIR_W), 0)
    lo = lane < HEAD_DIM
    diag = (lane & (HEAD_DIM - 1)) == sub

    def head_sums(x):
        s_lo = jnp.sum(jnp.where(lo, x, 0.0), axis=1, keepdims=True)
        s_hi = jnp.sum(jnp.where(lo, 0.0, x), axis=1, keepdims=True)
        return jnp.where(lo, s_lo, s_hi)

    sub8 = lax.broadcasted_iota(jnp.int32, (8, PAIR_W), 0)

    def row_of(ref, p, t):
        tile = ref[p, pl.ds(pl.multiple_of((t // 8) * 8, 8), 8), :]
        return jnp.sum(jnp.where(sub8 == (t % 8), tile, 0.0), axis=0, keepdims=True)

    def emit_y(p, t, s):
        y_col = head_sums(s * row_of(r_ref, p, t))
        y_ref[p, pl.ds(t, 1), :] = jnp.sum(jnp.where(diag, y_col, 0.0), axis=0, keepdims=True)

    def update(p, t, s):
        sa = head_sums(s * row_of(a_ref, p, t))
        v_col = head_sums(jnp.where(diag, row_of(v_ref, p, t), 0.0))
        return s * row_of(w_ref, p, t) + sa * row_of(b_ref, p, t) + v_col * row_of(k_ref, p, t)

    for p in range(PAIRS):
        s_scr[p] = update(p, 0, s_scr[p])

    def step(t, carry):
        for p in range(PAIRS):
            s = s_scr[p]
            emit_y(p, t - 1, s)
            s_scr[p] = update(p, t, s)
        return carry

    lax.fori_loop(1, steps, step, 0, unroll=unroll)
    for p in range(PAIRS):
        emit_y(p, steps - 1, s_scr[p])

    @pl.when(tb == pl.num_programs(1) - 1)
    def _():
        st_ref[0] = s_scr[...]


def _pair_state(s):
    b = s.shape[0]
    return s.reshape(b, PAIRS, 2, HEAD_DIM, HEAD_DIM).transpose(0, 1, 3, 2, 4).reshape(b, PAIRS, HEAD_DIM, PAIR_W)


def _unpair_state(s):
    b = s.shape[0]
    return s.reshape(b, PAIRS, HEAD_DIM, 2, HEAD_DIM).transpose(0, 1, 3, 2, 4).reshape(b, HEADS, HEAD_DIM, HEAD_DIM)


def _wkv_scan(r, w, k, v, a, b, s0, y_in, row0, batch, seq, steps):
    m = r.shape[1]
    nb = seq // steps
    base = row0 // steps
    rows = lambda bi, ti: (0, base + bi * nb + ti, 0)
    blk = pl.BlockSpec((PAIRS, steps, PAIR_W), rows)
    state = pl.BlockSpec((1, PAIRS, HEAD_DIM, PAIR_W), lambda bi, ti: (bi, 0, 0, 0))
    y_spec = [] if y_in is None else [pl.BlockSpec(memory_space=pl.ANY)]
    kern = _scan_kernel if y_in is not None else (lambda *refs, **kw: _scan_kernel(*refs[:7], None, *refs[7:], **kw))
    y, st = pl.pallas_call(
        functools.partial(kern, steps=steps, unroll=1),
        grid=(batch, nb),
        in_specs=[blk] * 6 + [state] + y_spec,
        out_specs=[blk, state],
        out_shape=[jax.ShapeDtypeStruct((PAIRS, m, PAIR_W), F32),
                   jax.ShapeDtypeStruct((batch, PAIRS, HEAD_DIM, PAIR_W), F32)],
        scratch_shapes=[pltpu.VMEM((PAIRS, HEAD_DIM, PAIR_W), F32)],
        input_output_aliases={} if y_in is None else {7: 0},
        compiler_params=_params(dimension_semantics=("parallel", "arbitrary")),
        name="wkv_scan",
    )(r, w, k, v, a, b, _pair_state(s0.astype(F32)), *([] if y_in is None else [y_in]))
    return y, _unpair_state(st)


def _postmix_kernel(x_ref, oa_ref, y_ref, bonus_ref, g_ref, zg_ref, lng_ref, lnb_ref, gmean_ref, wpa_ref, wpb_ref,
                    wout_ref, n2g_ref, wch_ref, wcl_ref, x1_ref, hn_ref, sc_ref):
    y = jnp.concatenate([y_ref[pair] for pair in range(PAIRS)], axis=1)
    mu = _dot(y, gmean_ref[...], precision=HIGHEST)
    d = y - mu
    var = _dot(d * d, gmean_ref[...], precision=HIGHEST)
    yn = d * lax.rsqrt(var + LNX_EPS) * lng_ref[...] + lnb_ref[...]
    ob = (yn + bonus_ref[...]) * g_ref[...]
    pa = _dot(oa_ref[...].astype(BF16), wpa_ref[...])
    pb = _dot(ob.astype(BF16), wpb_ref[...])
    merged = _sigmoid(zg_ref[:, :D_MODEL]) * pa + _sigmoid(zg_ref[:, D_MODEL:]) * pb
    x1 = x_ref[...] + _dot(merged.astype(BF16), wout_ref[...])
    x1_ref[...] = x1
    hn = x1 * lax.rsqrt(jnp.mean(x1 * x1, axis=-1, keepdims=True) + NORM_EPS) * n2g_ref[...]
    hn_ref[...] = hn
    hh = hn.astype(BF16)
    hl = (hn - hh.astype(F32)).astype(BF16)
    sc_ref[...] = _dot_nt(wch_ref[...], hh) + (_dot_nt(wcl_ref[...], hh) + _dot_nt(wch_ref[...], hl))


def _post_mix(x, o_a, y, bonus, g, zg, p, wc_t):
    m = x.shape[0]
    tm = ROW_TILE
    row = lambda i: (i, 0)
    nsc = wc_t.shape[0]
    vec = lambda a: a.reshape(1, -1).astype(F32)
    wch = wc_t.astype(BF16)
    wcl = (wc_t - wch.astype(F32)).astype(BF16)
    wide = pl.BlockSpec((tm, D_MODEL), row)
    half = pl.BlockSpec((tm, WIDTH), row)
    return pl.pallas_call(
        _postmix_kernel,
        grid=(m // tm,),
        in_specs=[wide, half, pl.BlockSpec((PAIRS, tm, PAIR_W), lambda i: (0, i, 0)), half, half,
                  pl.BlockSpec((tm, G_COLS), row),
                  _resident((1, WIDTH)), _resident((1, WIDTH)), _resident((WIDTH, WIDTH)),
                  _resident((WIDTH, D_MODEL)), _resident((WIDTH, D_MODEL)), _resident((D_MODEL, D_MODEL)),
                  _resident((1, D_MODEL)), _resident((nsc, D_MODEL)), _resident((nsc, D_MODEL))],
        out_specs=[wide, wide, pl.BlockSpec((nsc, tm), lambda i: (0, i))],
        out_shape=[jax.ShapeDtypeStruct((m, D_MODEL), F32), jax.ShapeDtypeStruct((m, D_MODEL), F32),
                   jax.ShapeDtypeStruct((nsc, m), F32)],
        compiler_params=_params(dimension_semantics=("parallel",)),
        name="post_mix",
    )(x, o_a, y, bonus, g, zg, vec(p["lnx_g"]), vec(p["lnx_b"]), _head_sum_matrix(1.0 / HEAD_DIM),
      p["w_proj_a"].astype(BF16), p["w_proj_b"].astype(BF16), p["w_out"].astype(BF16), vec(p["norm2_g"]), wch, wcl)


def _score_weight_kernel(wq_ref, sk_ref, o_ref):
    for c in range(2):
        wq = wq_ref[:, c * P_HALF:(c + 1) * P_HALF]
        o_ref[c * N_KEYS:(c + 1) * N_KEYS, :] = _dot_nt(sk_ref[0, c], wq, precision=HIGHEST)


def _score_weights(w_query, sub_keys):
    return pl.pallas_call(
        _score_weight_kernel,
        grid=(P_HEADS,),
        in_specs=[pl.BlockSpec((D_MODEL, 2 * P_HALF), lambda h: (0, h)),
                  pl.BlockSpec((1, 2, N_KEYS, P_HALF), lambda h: (h, 0, 0, 0))],
        out_specs=pl.BlockSpec((2 * N_KEYS, D_MODEL), lambda h: (h, 0)),
        out_shape=jax.ShapeDtypeStruct((P_HEADS * 2 * N_KEYS, D_MODEL), F32),
        compiler_params=_params(dimension_semantics=("parallel",)),
        name="score_weights",
    )(w_query, sub_keys)


def _extract_topk(vals, ids, n, payload=None):
    top_v, top_i = [], []
    big = jnp.int32(vals.shape[0])
    for _ in range(n):
        m = jnp.max(vals, axis=0, keepdims=True)
        pick = jnp.min(jnp.where(vals == m, ids, big), axis=0, keepdims=True)
        sel = ids == pick
        top_v.append(m)
        top_i.append(pick if payload is None else jnp.max(jnp.where(sel, payload, -1), axis=0, keepdims=True))
        vals = jnp.where(sel, -jnp.inf, vals)
    return top_v, top_i


def _topk_kernel(sc_ref, eidx_ref, gate_ref):
    lanes = sc_ref.shape[1]
    key_id = lax.broadcasted_iota(jnp.int32, (N_KEYS, lanes), 0)
    cand_id = lax.broadcasted_iota(jnp.int32, (P_TOPK * P_TOPK, lanes), 0)

    def head(h, carry):
        base = pl.multiple_of(h * 2 * N_KEYS, 2 * N_KEYS)
        v0, i0 = _extract_topk(sc_ref[pl.ds(base, N_KEYS), :], key_id, P_TOPK)
        v1, i1 = _extract_topk(sc_ref[pl.ds(base + N_KEYS, N_KEYS), :], key_id, P_TOPK)
        v1 = jnp.concatenate(v1, axis=0)
        i1 = jnp.concatenate(i1, axis=0)
        cand = jnp.concatenate([va + v1 for va in v0], axis=0)
        cidx = jnp.concatenate([ia * N_KEYS + i1 for ia in i0], axis=0)
        fv, fe = _extract_topk(cand, cand_id, P_TOPK, payload=cidx)
        fv = jnp.concatenate(fv, axis=0)
        e = jnp.exp(fv - fv[0:1])
        out = pl.multiple_of(h * P_TOPK, P_TOPK)
        gate_ref[pl.ds(out, P_TOPK), :] = e / jnp.sum(e, axis=0, keepdims=True)
        eidx_ref[pl.ds(out, P_TOPK), :] = jnp.concatenate(fe, axis=0)
        return carry

    lax.fori_loop(0, P_HEADS, head, 0)


def _topk(scores):
    nsc, m = scores.shape
    tl = 2 * TOK_TILE
    col = lambda i: (0, i)
    return pl.pallas_call(
        _topk_kernel,
        grid=(m // tl,),
        in_specs=[pl.BlockSpec((nsc, tl), col)],
        out_specs=[pl.BlockSpec((P_HEADS * P_TOPK, tl), col)] * 2,
        out_shape=[jax.ShapeDtypeStruct((P_HEADS * P_TOPK, m), jnp.int32),
                   jax.ShapeDtypeStruct((P_HEADS * P_TOPK, m), F32)],
        compiler_params=_params(dimension_semantics=("parallel",)),
        name="topk",
    )(scores)


N_SEL = P_HEADS * P_TOPK
ROW_SHAPE = (8, D_MODEL // 8)
U_ROW_GROUP = 64
V_ROW_GROUP = 16


def _tile_table(t):
    return t.astype(BF16).reshape(t.shape[0], *ROW_SHAPE)


def _sublane_fold(x, y, step, mask):
    return jnp.where(mask, x + pltpu.roll(x, 8 - step, 0), y + pltpu.roll(y, step, 0))


def _peer_u_kernel(idx_ref, hn_ref, gate_ref, tbl_ref, coef_ref, part_scr):
    toks = hn_ref.shape[0]
    sub = lax.broadcasted_iota(jnp.int32, ROW_SHAPE, 0)
    m4, m2, m1 = sub < 4, (sub & 3) < 2, (sub & 1) == 0

    def token(t, carry):
        x = hn_ref[t]

        def group(g, carry):
            first = t * N_SEL + g * U_ROW_GROUP
            for h in range(U_ROW_GROUP // 8):
                prod = [tbl_ref[idx_ref[first + h * 8 + j]].astype(F32) * x for j in range(8)]
                z = [_sublane_fold(prod[a], prod[a + 4], 4, m4) for a in (0, 2, 1, 3)]
                w0 = _sublane_fold(z[0], z[1], 2, m2)
                w1 = _sublane_fold(z[2], z[3], 2, m2)
                rows = pl.ds(pl.multiple_of(first + h * 8, 8), 8)
                part_scr[rows, :] = _sublane_fold(w0, w1, 1, m1)
            return carry

        return lax.fori_loop(0, N_SEL // U_ROW_GROUP, group, carry)

    lax.fori_loop(0, toks, token, 0)

    lane_tok = lax.broadcasted_iota(jnp.int32, (N_SEL, toks), 1)
    tok_unroll = 8

    def reduce_tokens(i, hid):
        for u in range(tok_unroll):
            t = i * tok_unroll + u
            rows = pl.ds(pl.multiple_of(t * N_SEL, N_SEL), N_SEL)
            hid = jnp.where(lane_tok == t, jnp.sum(part_scr[rows, :], axis=1, keepdims=True), hid)
        return hid

    hid = lax.fori_loop(0, toks // tok_unroll, reduce_tokens, jnp.zeros((N_SEL, toks), F32))
    coef_ref[...] = gate_ref[...] * (0.5 * hid * (1.0 + lax.erf(hid * (2.0 ** -0.5))))


def _peer_u(eidx_flat, hn3, gate, table):
    m = hn3.shape[0]
    tt = TOK_TILE
    return pl.pallas_call(
        _peer_u_kernel,
        grid=(m // tt,),
        in_specs=[pl.BlockSpec((tt * N_SEL,), lambda i: (i,), memory_space=pltpu.SMEM),
                  pl.BlockSpec((tt, *ROW_SHAPE), lambda i: (i, 0, 0)),
                  pl.BlockSpec((N_SEL, tt), lambda i: (0, i)),
                  _resident(table.shape)],
        out_specs=pl.BlockSpec((N_SEL, tt), lambda i: (0, i)),
        out_shape=jax.ShapeDtypeStruct((N_SEL, m), F32),
        scratch_shapes=[pltpu.VMEM((tt * N_SEL, ROW_SHAPE[1]), F32)],
        compiler_params=_params(dimension_semantics=("parallel",)),
        name="peer_u",
    )(eidx_flat, hn3, gate, table)


def _peer_v_kernel(idx_ref, coef_ref, x_ref, tbl_ref, g_ref, y_ref):
    toks = x_ref.shape[0]
    n_acc = 4

    def token(t, carry):
        def group(g, acc):
            acc = list(acc)
            first = t * N_SEL + g * V_ROW_GROUP
            for j in range(V_ROW_GROUP):
                acc[j % n_acc] = acc[j % n_acc] + coef_ref[first + j] * tbl_ref[idx_ref[first + j]].astype(F32)
            return tuple(acc)

        acc = lax.fori_loop(0, N_SEL // V_ROW_GROUP, group, tuple(jnp.zeros(ROW_SHAPE, F32) for _ in range(n_acc)))
        y_ref[t] = x_ref[t] + ((acc[0] + acc[1]) + (acc[2] + acc[3]))
        return carry

    lax.fori_loop(0, toks, token, 0)
    x2 = y_ref[...]
    ms = jnp.sum(jnp.sum(x2 * x2, axis=2, keepdims=True), axis=1, keepdims=True) * (1.0 / D_MODEL)
    y_ref[...] = x2 * lax.rsqrt(ms + NORM_EPS) * g_ref[...]


def _peer_v(eidx_flat, coef_flat, x3, table, normf_g):
    m = x3.shape[0]
    tt = TOK_TILE
    smem = pl.BlockSpec((tt * N_SEL,), lambda i: (i,), memory_space=pltpu.SMEM)
    tok = pl.BlockSpec((tt, *ROW_SHAPE), lambda i: (i, 0, 0))
    return pl.pallas_call(
        _peer_v_kernel,
        grid=(m // tt,),
        in_specs=[smem, smem, tok, _resident(table.shape), _resident(ROW_SHAPE)],
        out_specs=tok,
        out_shape=jax.ShapeDtypeStruct(x3.shape, F32),
        compiler_params=_params(dimension_semantics=("parallel",)),
        name="peer_v",
    )(eidx_flat, coef_flat, x3, table, normf_g.reshape(ROW_SHAPE).astype(F32))


def kernel(x_prompt, x_sample, cache_attn_k, cache_attn_v, state_wkv, state_shift, norm1_g, w_in, rel_bias, shift_mu,
           w_decay0, w_decay_up, a0, w_a_up, w_g_up, k_k, k_a, r_k, lnx_g, lnx_b, w_proj_a, w_proj_b, w_out, norm2_g,
           w_query, sub_keys, expert_u, expert_v, normf_g):
    depth = norm1_g.shape[0]
    assert depth == 1, "single-layer step"
    batch, seq, _ = x_prompt.shape
    dbatch, dseq, _ = x_sample.shape
    assert dseq == SHIFT_GROUP and seq % ATT_TILE == 0
    mp, ms = batch * seq, dbatch * dseq
    m = mp + ms
    assert m % (2 * TOK_TILE) == 0 and mp % ROW_TILE == 0
    p = dict(shift_mu=shift_mu[0], w_decay0=w_decay0[0], w_decay_up=w_decay_up[0], a0=a0[0], w_a_up=w_a_up[0],
             w_g_up=w_g_up[0], k_k=k_k[0], k_a=k_a[0], r_k=r_k[0].reshape(-1), lnx_g=lnx_g[0], lnx_b=lnx_b[0],
             w_proj_a=w_proj_a[0], w_proj_b=w_proj_b[0], w_out=w_out[0], norm2_g=norm2_g[0])

    x = jnp.concatenate([x_prompt.reshape(mp, D_MODEL), x_sample.reshape(ms, D_MODEL)], axis=0)
    za, zb, zg = _in_proj(x, norm1_g[0], w_in[0])

    o_a = _attn_prompt(za, rel_bias[0], batch, seq, m)
    o_a = _attn_sample(za, o_a, cache_attn_k[0], cache_attn_v[0], rel_bias[0], mp)

    shift_rows = jnp.concatenate([jnp.zeros((mp // SHIFT_GROUP, B_COLS), F32), state_shift[0].reshape(dbatch, B_COLS)])
    r, w, k, v, a, b, g, bonus = _rwkv_prep(zb, shift_rows, p, seq, mp)
    y, wkv_p = _wkv_scan(r, w, k, v, a, b, jnp.zeros((batch, HEADS, HEAD_DIM, HEAD_DIM), F32), None, 0, batch, seq,
                         SCAN_TILE)
    y, wkv_s = _wkv_scan(r, w, k, v, a, b, state_wkv[0], y, mp, dbatch, dseq, dseq)

    wc_t = _score_weights(w_query[0], sub_keys[0])
    x1, hn, scores = _post_mix(x, o_a, y, bonus, g, zg, p, wc_t)
    eidx, gate = _topk(scores)
    eidx_flat = eidx.T.reshape(-1)
    coef = _peer_u(eidx_flat, hn.reshape(m, *ROW_SHAPE), gate, _tile_table(expert_u[0]))
    out = _peer_v(eidx_flat, coef.T.reshape(-1), x1.reshape(m, *ROW_SHAPE), _tile_table(expert_v[0]),
                  normf_g).reshape(m, D_MODEL)

    heads = lambda t, n: t.reshape(n, -1, HEADS, HEAD_DIM).transpose(0, 2, 1, 3)[None]
    keep = min(BAND_CHUNKS * CHUNK, seq)
    zp = za[:mp].reshape(batch, seq, A_COLS)[:, seq - keep:]
    zs = za[mp:].reshape(dbatch, dseq, A_COLS)
    return (out[:mp].reshape(batch, seq, D_MODEL), out[mp:].reshape(dbatch, dseq, D_MODEL),
            heads(zp[..., WIDTH:2 * WIDTH], batch), heads(zp[..., 2 * WIDTH:], batch),
            wkv_p[None], zb[:mp].reshape(batch, seq, B_COLS)[:, -1:][None],
            heads(zs[..., WIDTH:2 * WIDTH], dbatch), heads(zs[..., 2 * WIDTH:], dbatch),
            wkv_s[None], zb[mp:].reshape(dbatch, dseq, B_COLS)[:, -1:][None])
```

```python
import functools

import jax
import jax.numpy as jnp
from jax import lax
from jax.experimental import pallas as pl
from jax.experimental.pallas import tpu as pltpu

F32 = jnp.float32
BF16 = jnp.bfloat16
HIGHEST = lax.Precision.HIGHEST

D_MODEL = 1024
CHUNK = 64
BAND_CHUNKS = 8
HEADS = 8
HEAD_DIM = 64
WIDTH = HEADS * HEAD_DIM
REL_CLIP = 128
DECAY_RANK = 64
AAA_RANK = 64
GATE_RANK = 128
LNX_EPS = 64e-5
NORM_EPS = 1e-6
P_HEADS = 8
N_KEYS = 128
P_HALF = 64
P_TOPK = 16
A_COLS = 3 * WIDTH
B_COLS = 3 * WIDTH + DECAY_RANK + AAA_RANK + GATE_RANK
G_COLS = 2 * D_MODEL
NEG = -1e30

VMEM_LIMIT = 56 * 1024 * 1024
ROW_TILE = 256
ATT_TILE = BAND_CHUNKS * CHUNK
BAND_KEYS = (BAND_CHUNKS + 1) * CHUNK
SCAN_TILE = 128
SCAN_BATCH = 2
TOK_TILE = 128
SHIFT_GROUP = 32
PAIRS = HEADS // 2
PAIR_W = 2 * HEAD_DIM


def _params(**kw):
    return pltpu.CompilerParams(vmem_limit_bytes=VMEM_LIMIT, **kw)


def _resident(shape):
    nd = len(shape)
    return pl.BlockSpec(shape, lambda *_: (0,) * nd, pipeline_mode=pl.Buffered(1))


def _sigmoid(x):
    return 1.0 / (1.0 + jnp.exp(-x))


def _dot(a, b, **kw):
    return jnp.dot(a, b, preferred_element_type=F32, **kw)


def _dot_nt(a, b, **kw):
    return lax.dot_general(a, b, (((1,), (1,)), ((), ())), preferred_element_type=F32, **kw)


def _row_vec(a):
    return a.reshape(1, -1).astype(F32)


def _inproj_kernel(x_ref, g_ref, w_ref, za_ref, zb_ref, zg_ref):
    x = x_ref[...]
    y = x * lax.rsqrt(jnp.mean(x * x, axis=-1, keepdims=True) + NORM_EPS) * g_ref[...]
    yb = y.astype(BF16)
    za_ref[...] = _dot(yb, w_ref[:, :A_COLS])
    zb_ref[...] = _dot(yb, w_ref[:, A_COLS:A_COLS + B_COLS])
    zg_ref[...] = _dot(yb, w_ref[:, A_COLS + B_COLS:])


def _in_proj(x, norm_g, w_in_bf16):
    m = x.shape[0]
    in_cols = w_in_bf16.shape[1]
    row = lambda i: (i, 0)
    return pl.pallas_call(
        _inproj_kernel,
        grid=(m // ROW_TILE,),
        in_specs=[pl.BlockSpec((ROW_TILE, D_MODEL), row), _resident((1, D_MODEL)), _resident((D_MODEL, in_cols))],
        out_specs=[pl.BlockSpec((ROW_TILE, A_COLS), row), pl.BlockSpec((ROW_TILE, B_COLS), row),
                   pl.BlockSpec((ROW_TILE, G_COLS), row)],
        out_shape=[jax.ShapeDtypeStruct((m, A_COLS), F32), jax.ShapeDtypeStruct((m, B_COLS), F32),
                   jax.ShapeDtypeStruct((m, G_COLS), F32)],
        compiler_params=_params(dimension_semantics=("parallel",)),
        name="in_proj",
    )(x, _row_vec(norm_g), w_in_bf16)


def _softmax_pv(scores, values):
    m = scores[0].max(axis=-1, keepdims=True)
    for s in scores[1:]:
        m = jnp.maximum(m, s.max(axis=-1, keepdims=True))
    acc, den = None, None
    for s, v in zip(scores, values):
        p = jnp.exp(s - m)
        d = p.sum(axis=-1, keepdims=True)
        o = _dot(p.astype(BF16), v)
        acc = o if acc is None else acc + o
        den = d if den is None else den + d
    return acc / den


def _attn_prompt_kernel(q_ref, kp_ref, kc_ref, vp_ref, vc_ref, bias_ref, o_ref):
    first = pl.program_id(1) == 0
    scale = HEAD_DIM ** -0.5
    for h in range(HEADS):
        sl = slice(h * HEAD_DIM, (h + 1) * HEAD_DIM)
        q = q_ref[:, sl].astype(BF16)
        s_prev = _dot_nt(q, kp_ref[:, sl].astype(BF16)) * scale + bias_ref[h, :, :ATT_TILE]
        s_cur = _dot_nt(q, kc_ref[:, sl].astype(BF16)) * scale + bias_ref[h, :, ATT_TILE:]
        s_prev = jnp.where(first, NEG, s_prev)
        o_ref[:, sl] = _softmax_pv([s_prev, s_cur], [vp_ref[:, sl].astype(BF16), vc_ref[:, sl].astype(BF16)])


def _prompt_bias_table(rel_bias):
    dist = jnp.arange(CHUNK)[:, None] + BAND_CHUNKS * CHUNK - jnp.arange(BAND_KEYS)[None, :]
    window = rel_bias[:, jnp.clip(dist, -REL_CLIP, REL_CLIP) + REL_CLIP].astype(F32)
    rows = [jnp.pad(window, ((0, 0), (0, 0), (c * CHUNK, 2 * ATT_TILE - BAND_KEYS - c * CHUNK)), constant_values=NEG)
            for c in range(BAND_CHUNKS)]
    return jnp.concatenate(rows, axis=1)


def _attn_prompt(za, rel_bias, batch, seq):
    nb = seq // ATT_TILE
    blk = (ATT_TILE, WIDTH)
    cur = lambda col: (lambda b, i: (b * nb + i, col))
    prev = lambda col: (lambda b, i: (b * nb + jnp.maximum(i - 1, 0), col))
    return pl.pallas_call(
        _attn_prompt_kernel,
        grid=(batch, nb),
        in_specs=[pl.BlockSpec(blk, cur(0)), pl.BlockSpec(blk, prev(1)), pl.BlockSpec(blk, cur(1)),
                  pl.BlockSpec(blk, prev(2)), pl.BlockSpec(blk, cur(2)),
                  _resident((HEADS, ATT_TILE, 2 * ATT_TILE))],
        out_specs=pl.BlockSpec(blk, cur(0)),
        out_shape=jax.ShapeDtypeStruct((batch * seq, WIDTH), F32),
        compiler_params=_params(dimension_semantics=("parallel", "arbitrary")),
        name="attn_prompt",
    )(za, za, za, za, za, _prompt_bias_table(rel_bias))


def _attn_sample_kernel(q_ref, kn_ref, vn_ref, kc_ref, vc_ref, bc_ref, bn_ref, o_ref):
    scale = HEAD_DIM ** -0.5
    for h in range(HEADS):
        sl = slice(h * HEAD_DIM, (h + 1) * HEAD_DIM)
        q = q_ref[:, sl].astype(BF16)
        s_cache = _dot_nt(q, kc_ref[0, h].astype(BF16)) * scale + bc_ref[h]
        s_new = _dot_nt(q, kn_ref[:, sl].astype(BF16)) * scale + bn_ref[h]
        o_ref[:, sl] = _softmax_pv([s_cache, s_new], [vc_ref[0, h].astype(BF16), vn_ref[:, sl].astype(BF16)])


def _attn_sample(za, k_cache, v_cache, rel_bias, seq):
    nb, _, past, _ = k_cache.shape
    dist = jnp.arange(seq)[:, None] + past - jnp.arange(past + seq)[None, :]
    bias = rel_bias[:, jnp.clip(dist, -REL_CLIP, REL_CLIP) + REL_CLIP].astype(F32)
    blk = (seq, WIDTH)
    rows = lambda col: (lambda b: (b, col))
    cache = pl.BlockSpec((1, HEADS, past, HEAD_DIM), lambda b: (b, 0, 0, 0))
    return pl.pallas_call(
        _attn_sample_kernel,
        grid=(nb,),
        in_specs=[pl.BlockSpec(blk, rows(0)), pl.BlockSpec(blk, rows(1)), pl.BlockSpec(blk, rows(2)), cache, cache,
                  _resident((HEADS, seq, past)), _resident((HEADS, seq, seq))],
        out_specs=pl.BlockSpec(blk, rows(0)),
        out_shape=jax.ShapeDtypeStruct((nb * seq, WIDTH), F32),
        compiler_params=_params(dimension_semantics=("parallel",)),
        name="attn_sample",
    )(za, za, za, k_cache, v_cache, bias[:, :, :past], bias[:, :, past:])


def _softplus(x):
    return jnp.maximum(x, 0.0) + jnp.log(1.0 + jnp.exp(-jnp.abs(x)))


def _rwkv_prep_kernel(zb_ref, prev_ref, shift_ref, mu_ref, wd0_ref, wdu_ref, a0_ref, wau_ref, wgu_ref, kk_ref, ka_ref,
                      rk_ref, gsum_ref, r_o, w_o, k_o, v_o, a_o, b_o, g_o, bonus_o, *, seq):
    i = pl.program_id(0)
    zb = zb_ref[...]
    tm = zb.shape[0]
    row = lax.broadcasted_iota(jnp.int32, zb.shape, 0)
    prev = jnp.where(row == 0, prev_ref[7:8, :], pltpu.roll(zb, 1, 0))
    ngrp = tm // SHIFT_GROUP
    shift = jnp.broadcast_to(shift_ref[...][:, None, :], (ngrp, SHIFT_GROUP, B_COLS)).reshape(tm, B_COLS)
    prev = jnp.where(lax.rem(i * tm + row, seq) == 0, shift, prev)
    zm = zb + (prev - zb) * mu_ref[...]
    r = zm[:, 0:WIDTH]
    k = zm[:, WIDTH:2 * WIDTH]
    v = zm[:, 2 * WIDTH:3 * WIDTH]
    lora_in = zm[:, 3 * WIDTH:3 * WIDTH + DECAY_RANK + AAA_RANK]
    gate_in = zm[:, 3 * WIDTH + DECAY_RANK + AAA_RANK:]
    w_log = -_softplus(-(wd0_ref[...] + _dot(jnp.tanh(lora_in), wdu_ref[...], precision=HIGHEST))) - 0.5
    decay = jnp.exp(-jnp.exp(w_log))
    a = _sigmoid(a0_ref[...] + _dot(lora_in, wau_ref[...], precision=HIGHEST))
    g = _dot(_sigmoid(gate_in), wgu_ref[...], precision=HIGHEST)
    kk = k * kk_ref[...]
    kk = kk / jnp.maximum(jnp.sqrt(_dot(kk * kk, gsum_ref[...], precision=HIGHEST)), 1e-12)
    kmod = k * (1.0 + (a - 1.0) * ka_ref[...])
    for ref, val in ((r_o, r), (w_o, decay), (k_o, kmod), (v_o, v), (a_o, -kk), (b_o, kk * a)):
        for pair in range(PAIRS):
            ref[pair] = val[:, pair * PAIR_W:(pair + 1) * PAIR_W].reshape(ref.shape[1:])
    g_o[...] = g
    bonus_o[...] = _dot(r * kmod * rk_ref[...], gsum_ref[...], precision=HIGHEST) * v


def _head_sum_matrix(scale):
    h = jnp.arange(WIDTH) // HEAD_DIM
    return jnp.where(h[:, None] == h[None, :], scale, 0.0).astype(F32)


def _scan_layout(batch, seq):
    steps = min(seq, SCAN_TILE)
    shape = (PAIRS, seq // steps, batch, steps, PAIR_W)
    if seq >= ROW_TILE:
        per_seq = seq // ROW_TILE
        block = (PAIRS, ROW_TILE // steps, 1, steps, PAIR_W)
        index = lambda i: (0, i % per_seq, i // per_seq, 0, 0)
    else:
        block = (PAIRS, 1, ROW_TILE // seq, steps, PAIR_W)
        index = lambda i: (0, 0, i, 0, 0)
    return steps, shape, pl.BlockSpec(block, index)


def _rwkv_prep(zb, shift0, w, batch, seq):
    m = zb.shape[0]
    tm = ROW_TILE
    row = lambda i: (i, 0)
    groups_per_seq = seq // SHIFT_GROUP
    shift_rows = jnp.zeros((batch, groups_per_seq, B_COLS), F32).at[:, 0].set(shift0.reshape(batch, B_COLS))
    _, scan_shape, scan_spec = _scan_layout(batch, seq)
    flat = jax.ShapeDtypeStruct((m, WIDTH), F32)
    return pl.pallas_call(
        functools.partial(_rwkv_prep_kernel, seq=seq),
        grid=(m // tm,),
        in_specs=[pl.BlockSpec((tm, B_COLS), row),
                  pl.BlockSpec((8, B_COLS), lambda i: (jnp.maximum(i * (tm // 8) - 1, 0), 0)),
                  pl.BlockSpec((tm // SHIFT_GROUP, B_COLS), row),
                  _resident((1, B_COLS)), _resident((1, WIDTH)), _resident((DECAY_RANK + AAA_RANK, WIDTH)),
                  _resident((1, WIDTH)), _resident((DECAY_RANK + AAA_RANK, WIDTH)), _resident((GATE_RANK, WIDTH)),
                  _resident((1, WIDTH)), _resident((1, WIDTH)), _resident((1, WIDTH)), _resident((WIDTH, WIDTH))],
        out_specs=[scan_spec] * 6 + [pl.BlockSpec((tm, WIDTH), row)] * 2,
        out_shape=[jax.ShapeDtypeStruct(scan_shape, F32)] * 6 + [flat] * 2,
        compiler_params=_params(dimension_semantics=("parallel",)),
        name="rwkv_prep",
    )(zb, zb, shift_rows.reshape(m // SHIFT_GROUP, B_COLS), w["shift_mu"], w["w_decay0"], w["w_decay_up"], w["a0"],
      w["w_a_up"], w["w_g_up"], w["k_k"], w["k_a"], w["r_k"], w["head_sum"])


def _scan_kernel(r_ref, w_ref, k_ref, v_ref, a_ref, b_ref, s0_ref, y_ref, st_ref, s_scr, *, steps):
    tb = pl.program_id(1)
    chains = [(b, p) for b in range(s_scr.shape[0]) for p in range(PAIRS)]

    @pl.when(tb == 0)
    def _():
        s_scr[...] = s0_ref[...]

    lane = lax.broadcasted_iota(jnp.int32, (HEAD_DIM, PAIR_W), 1)
    sub = lax.broadcasted_iota(jnp.int32, (HEAD_DIM, PAIR_W), 0)
    lo = lane < HEAD_DIM
    diag = (lane & (HEAD_DIM - 1)) == sub
    sub8 = lax.broadcasted_iota(jnp.int32, (8, PAIR_W), 0)

    def head_sums(x):
        s_lo = jnp.sum(jnp.where(lo, x, 0.0), axis=1, keepdims=True)
        s_hi = jnp.sum(jnp.where(lo, 0.0, x), axis=1, keepdims=True)
        return jnp.where(lo, s_lo, s_hi)

    def row_of(ref, chain, t):
        b, p = chain
        tile = ref[p, 0, b, pl.ds(pl.multiple_of((t // 8) * 8, 8), 8), :]
        return jnp.sum(jnp.where(sub8 == (t % 8), tile, 0.0), axis=0, keepdims=True)

    def emit_y(chain, t, s):
        b, p = chain
        y_col = head_sums(s * row_of(r_ref, chain, t))
        y_ref[p, 0, b, pl.ds(t, 1), :] = jnp.sum(jnp.where(diag, y_col, 0.0), axis=0, keepdims=True)

    def update(chain, t, s):
        sa = head_sums(s * row_of(a_ref, chain, t))
        v_col = head_sums(jnp.where(diag, row_of(v_ref, chain, t), 0.0))
        return s * row_of(w_ref, chain, t) + sa * row_of(b_ref, chain, t) + v_col * row_of(k_ref, chain, t)

    for b, p in chains:
        s_scr[b, p] = update((b, p), 0, s_scr[b, p])

    def step(t, carry):
        for b, p in chains:
            s = s_scr[b, p]
            emit_y((b, p), t - 1, s)
            s_scr[b, p] = update((b, p), t, s)
        return carry

    lax.fori_loop(1, steps, step, 0)
    for b, p in chains:
        emit_y((b, p), steps - 1, s_scr[b, p])

    @pl.when(tb == pl.num_programs(1) - 1)
    def _():
        st_ref[...] = s_scr[...]


def _pair_state(s):
    b = s.shape[0]
    return s.reshape(b, PAIRS, 2, HEAD_DIM, HEAD_DIM).transpose(0, 1, 3, 2, 4).reshape(b, PAIRS, HEAD_DIM, PAIR_W)


def _unpair_state(s):
    b = s.shape[0]
    return s.reshape(b, PAIRS, HEAD_DIM, 2, HEAD_DIM).transpose(0, 1, 3, 2, 4).reshape(b, HEADS, HEAD_DIM, HEAD_DIM)


def _wkv_scan(r, w, k, v, a, b, s0, batch, seq):
    steps, shape, _ = _scan_layout(batch, seq)
    bg = SCAN_BATCH
    blk = pl.BlockSpec((PAIRS, 1, bg, steps, PAIR_W), lambda bi, ti: (0, ti, bi, 0, 0))
    state = pl.BlockSpec((bg, PAIRS, HEAD_DIM, PAIR_W), lambda bi, ti: (bi, 0, 0, 0))
    y, st = pl.pallas_call(
        functools.partial(_scan_kernel, steps=steps),
        grid=(batch // bg, seq // steps),
        in_specs=[blk] * 6 + [state],
        out_specs=[blk, state],
        out_shape=[jax.ShapeDtypeStruct(shape, F32), jax.ShapeDtypeStruct((batch, PAIRS, HEAD_DIM, PAIR_W), F32)],
        scratch_shapes=[pltpu.VMEM((bg, PAIRS, HEAD_DIM, PAIR_W), F32)],
        compiler_params=_params(dimension_semantics=("parallel", "arbitrary")),
        name="wkv_scan",
    )(r, w, k, v, a, b, _pair_state(s0.astype(F32)))
    return y, _unpair_state(st)


def _postmix_kernel(x_ref, oa_ref, y_ref, bonus_ref, g_ref, zg_ref, lng_ref, lnb_ref, gmean_ref, wpa_ref, wpb_ref,
                    wout_ref, n2g_ref, wch_ref, wcl_ref, x1_ref, hn_ref, sc_ref):
    tm = x_ref.shape[0]
    y = jnp.concatenate([y_ref[pair].reshape(tm, PAIR_W) for pair in range(PAIRS)], axis=1)
    mu = _dot(y, gmean_ref[...], precision=HIGHEST)
    d = y - mu
    var = _dot(d * d, gmean_ref[...], precision=HIGHEST)
    yn = d * lax.rsqrt(var + LNX_EPS) * lng_ref[...] + lnb_ref[...]
    ob = (yn + bonus_ref[...]) * g_ref[...]
    pa = _dot(oa_ref[...].astype(BF16), wpa_ref[...])
    pb = _dot(ob.astype(BF16), wpb_ref[...])
    merged = _sigmoid(zg_ref[:, :D_MODEL]) * pa + _sigmoid(zg_ref[:, D_MODEL:]) * pb
    x1 = x_ref[...] + _dot(merged.astype(BF16), wout_ref[...])
    x1_ref[...] = x1
    hn = x1 * lax.rsqrt(jnp.mean(x1 * x1, axis=-1, keepdims=True) + NORM_EPS) * n2g_ref[...]
    hn_ref[...] = hn
    hh = hn.astype(BF16)
    hl = (hn - hh.astype(F32)).astype(BF16)
    sc_ref[...] = _dot_nt(wch_ref[...], hh) + (_dot_nt(wcl_ref[...], hh) + _dot_nt(wch_ref[...], hl))


def _post_mix(x, o_a, y, bonus, g, zg, w, batch, seq):
    m = x.shape[0]
    tm = ROW_TILE
    row = lambda i: (i, 0)
    nsc = w["score_hi"].shape[0]
    wide = pl.BlockSpec((tm, D_MODEL), row)
    half = pl.BlockSpec((tm, WIDTH), row)
    _, _, scan_spec = _scan_layout(batch, seq)
    return pl.pallas_call(
        _postmix_kernel,
        grid=(m // tm,),
        in_specs=[wide, half, scan_spec, half, half, pl.BlockSpec((tm, G_COLS), row),
                  _resident((1, WIDTH)), _resident((1, WIDTH)), _resident((WIDTH, WIDTH)),
                  _resident((WIDTH, D_MODEL)), _resident((WIDTH, D_MODEL)), _resident((D_MODEL, D_MODEL)),
                  _resident((1, D_MODEL)), _resident((nsc, D_MODEL)), _resident((nsc, D_MODEL))],
        out_specs=[wide, wide, pl.BlockSpec((nsc, tm), lambda i: (0, i))],
        out_shape=[jax.ShapeDtypeStruct((m, D_MODEL), F32), jax.ShapeDtypeStruct((m, D_MODEL), F32),
                   jax.ShapeDtypeStruct((nsc, m), F32)],
        compiler_params=_params(dimension_semantics=("parallel",)),
        name="post_mix",
    )(x, o_a, y, bonus, g, zg, w["lnx_g"], w["lnx_b"], w["head_mean"], w["w_proj_a"], w["w_proj_b"], w["w_out"],
      w["norm2_g"], w["score_hi"], w["score_lo"])


def _score_weight_kernel(wq_ref, sk_ref, o_ref):
    for c in range(2):
        wq = wq_ref[:, c * P_HALF:(c + 1) * P_HALF]
        o_ref[c * N_KEYS:(c + 1) * N_KEYS, :] = _dot_nt(sk_ref[0, c], wq, precision=HIGHEST)


def _score_weights(w_query, sub_keys):
    return pl.pallas_call(
        _score_weight_kernel,
        grid=(P_HEADS,),
        in_specs=[pl.BlockSpec((D_MODEL, 2 * P_HALF), lambda h: (0, h)),
                  pl.BlockSpec((1, 2, N_KEYS, P_HALF), lambda h: (h, 0, 0, 0))],
        out_specs=pl.BlockSpec((2 * N_KEYS, D_MODEL), lambda h: (h, 0)),
        out_shape=jax.ShapeDtypeStruct((P_HEADS * 2 * N_KEYS, D_MODEL), F32),
        compiler_params=_params(dimension_semantics=("parallel",)),
        name="score_weights",
    )(w_query, sub_keys)


N_CAND = P_TOPK + 7 * 8 + 8


def _extract_topk(vals, ids, n, payload=None):
    top_v, top_i = [], []
    big = jnp.int32(2 ** 30)
    for _ in range(n):
        m = jnp.max(vals, axis=0, keepdims=True)
        pick = jnp.min(jnp.where(vals == m, ids, big), axis=0, keepdims=True)
        sel = ids == pick
        top_v.append(m)
        top_i.append(pick if payload is None else jnp.max(jnp.where(sel, payload, -1), axis=0, keepdims=True))
        vals = jnp.where(sel, -jnp.inf, vals)
    return top_v, top_i


def _topk_kernel(sc_ref, eidx_ref, gate_ref):
    lanes = sc_ref.shape[1]
    key_id = lax.broadcasted_iota(jnp.int32, (N_KEYS, lanes), 0)
    sub16 = lax.broadcasted_iota(jnp.int32, (P_TOPK, lanes), 0)
    sub8 = lax.broadcasted_iota(jnp.int32, (8, lanes), 0)
    cand_id = jnp.concatenate([sub16] + [a * P_TOPK + sub8 for a in range(1, 8)] + [(8 + sub8) * P_TOPK], axis=0)

    def head(h, carry):
        base = pl.multiple_of(h * 2 * N_KEYS, 2 * N_KEYS)
        v0, i0 = _extract_topk(sc_ref[pl.ds(base, N_KEYS), :], key_id, P_TOPK)
        v1, i1 = _extract_topk(sc_ref[pl.ds(base + N_KEYS, N_KEYS), :], key_id, P_TOPK)
        v1_16, i1_16 = jnp.concatenate(v1, axis=0), jnp.concatenate(i1, axis=0)
        v1_8, i1_8 = jnp.concatenate(v1[:8], axis=0), jnp.concatenate(i1[:8], axis=0)
        cand = jnp.concatenate([v0[0] + v1_16] + [v0[a] + v1_8 for a in range(1, 8)]
                               + [jnp.concatenate(v0[8:], axis=0) + v1[0]], axis=0)
        cidx = jnp.concatenate([i0[0] * N_KEYS + i1_16] + [i0[a] * N_KEYS + i1_8 for a in range(1, 8)]
                               + [jnp.concatenate(i0[8:], axis=0) * N_KEYS + i1[0]], axis=0)
        fv, fe = _extract_topk(cand, cand_id, P_TOPK, payload=cidx)
        fv = jnp.concatenate(fv, axis=0)
        e = jnp.exp(fv - fv[0:1])
        out = pl.multiple_of(h * P_TOPK, P_TOPK)
        gate_ref[pl.ds(out, P_TOPK), :] = e / jnp.sum(e, axis=0, keepdims=True)
        eidx_ref[pl.ds(out, P_TOPK), :] = jnp.concatenate(fe, axis=0)
        return carry

    lax.fori_loop(0, P_HEADS, head, 0)


def _topk(scores):
    nsc, m = scores.shape
    tl = 2 * TOK_TILE
    col = lambda i: (0, i)
    return pl.pallas_call(
        _topk_kernel,
        grid=(m // tl,),
        in_specs=[pl.BlockSpec((nsc, tl), col)],
        out_specs=[pl.BlockSpec((P_HEADS * P_TOPK, tl), col)] * 2,
        out_shape=[jax.ShapeDtypeStruct((P_HEADS * P_TOPK, m), jnp.int32),
                   jax.ShapeDtypeStruct((P_HEADS * P_TOPK, m), F32)],
        compiler_params=_params(dimension_semantics=("parallel",)),
        name="topk",
    )(scores)


N_SEL = P_HEADS * P_TOPK
ROW_SHAPE = (8, D_MODEL // 8)
U_ROW_GROUP = 64
V_ROW_GROUP = 16


def _tile_table(t):
    return t.astype(BF16).reshape(t.shape[0], *ROW_SHAPE)


def _sublane_fold(x, y, step, mask):
    return jnp.where(mask, x + pltpu.roll(x, 8 - step, 0), y + pltpu.roll(y, step, 0))


def _peer_u_kernel(idx_ref, hn_ref, gate_ref, tbl_ref, coef_ref, part_scr):
    toks = hn_ref.shape[0]
    sub = lax.broadcasted_iota(jnp.int32, ROW_SHAPE, 0)
    m4, m2, m1 = sub < 4, (sub & 3) < 2, (sub & 1) == 0

    def token(t, carry):
        x = hn_ref[t]

        def group(g, carry):
            first = t * N_SEL + g * U_ROW_GROUP
            for h in range(U_ROW_GROUP // 8):
                prod = [tbl_ref[idx_ref[first + h * 8 + j]].astype(F32) * x for j in range(8)]
                z = [_sublane_fold(prod[a], prod[a + 4], 4, m4) for a in (0, 2, 1, 3)]
                w0 = _sublane_fold(z[0], z[1], 2, m2)
                w1 = _sublane_fold(z[2], z[3], 2, m2)
                rows = pl.ds(pl.multiple_of(first + h * 8, 8), 8)
                part_scr[rows, :] = _sublane_fold(w0, w1, 1, m1)
            return carry

        return lax.fori_loop(0, N_SEL // U_ROW_GROUP, group, carry)

    lax.fori_loop(0, toks, token, 0)

    lane_tok = lax.broadcasted_iota(jnp.int32, (N_SEL, toks), 1)
    tok_unroll = 8

    def reduce_tokens(i, hid):
        for u in range(tok_unroll):
            t = i * tok_unroll + u
            rows = pl.ds(pl.multiple_of(t * N_SEL, N_SEL), N_SEL)
            hid = jnp.where(lane_tok == t, jnp.sum(part_scr[rows, :], axis=1, keepdims=True), hid)
        return hid

    hid = lax.fori_loop(0, toks // tok_unroll, reduce_tokens, jnp.zeros((N_SEL, toks), F32))
    coef_ref[...] = gate_ref[...] * (0.5 * hid * (1.0 + lax.erf(hid * (2.0 ** -0.5))))


def _peer_u(eidx_flat, hn3, gate, table):
    m = hn3.shape[0]
    tt = TOK_TILE
    return pl.pallas_call(
        _peer_u_kernel,
        grid=(m // tt,),
        in_specs=[pl.BlockSpec((tt * N_SEL,), lambda i: (i,), memory_space=pltpu.SMEM),
                  pl.BlockSpec((tt, *ROW_SHAPE), lambda i: (i, 0, 0)),
                  pl.BlockSpec((N_SEL, tt), lambda i: (0, i)),
                  _resident(table.shape)],
        out_specs=pl.BlockSpec((N_SEL, tt), lambda i: (0, i)),
        out_shape=jax.ShapeDtypeStruct((N_SEL, m), F32),
        scratch_shapes=[pltpu.VMEM((tt * N_SEL, ROW_SHAPE[1]), F32)],
        compiler_params=_params(dimension_semantics=("parallel",)),
        name="peer_u",
    )(eidx_flat, hn3, gate, table)


def _peer_v_kernel(idx_ref, coef_ref, x_ref, tbl_ref, g_ref, y_ref):
    toks = x_ref.shape[0]
    n_acc = 4

    def token(t, carry):
        def group(g, acc):
            acc = list(acc)
            first = t * N_SEL + g * V_ROW_GROUP
            for j in range(V_ROW_GROUP):
                acc[j % n_acc] = acc[j % n_acc] + coef_ref[first + j] * tbl_ref[idx_ref[first + j]].astype(F32)
            return tuple(acc)

        acc = lax.fori_loop(0, N_SEL // V_ROW_GROUP, group, tuple(jnp.zeros(ROW_SHAPE, F32) for _ in range(n_acc)))
        y_ref[t] = x_ref[t] + ((acc[0] + acc[1]) + (acc[2] + acc[3]))
        return carry

    lax.fori_loop(0, toks, token, 0)
    x2 = y_ref[...]
    ms = jnp.sum(jnp.sum(x2 * x2, axis=2, keepdims=True), axis=1, keepdims=True) * (1.0 / D_MODEL)
    y_ref[...] = x2 * lax.rsqrt(ms + NORM_EPS) * g_ref[...]


def _peer_v(eidx_flat, coef_flat, x3, table, normf_g):
    m = x3.shape[0]
    tt = TOK_TILE
    smem = pl.BlockSpec((tt * N_SEL,), lambda i: (i,), memory_space=pltpu.SMEM)
    tok = pl.BlockSpec((tt, *ROW_SHAPE), lambda i: (i, 0, 0))
    return pl.pallas_call(
        _peer_v_kernel,
        grid=(m // tt,),
        in_specs=[smem, smem, tok, _resident(table.shape), _resident(ROW_SHAPE)],
        out_specs=tok,
        out_shape=jax.ShapeDtypeStruct(x3.shape, F32),
        compiler_params=_params(dimension_semantics=("parallel",)),
        name="peer_v",
    )(eidx_flat, coef_flat, x3, table, normf_g)


def _stream_step(x, k_cache, v_cache, wkv0, shift0, w):
    batch, seq, _ = x.shape
    m = batch * seq
    assert m % (2 * TOK_TILE) == 0 and batch % SCAN_BATCH == 0 and seq % SHIFT_GROUP == 0
    assert (seq % ROW_TILE == 0 and seq % SCAN_TILE == 0) or ROW_TILE % seq == 0
    za, zb, zg = _in_proj(x.reshape(m, D_MODEL), w["norm1_g"], w["w_in"])
    if k_cache is None:
        assert seq % ATT_TILE == 0
        o_a = _attn_prompt(za, w["rel_bias"], batch, seq)
    else:
        o_a = _attn_sample(za, k_cache, v_cache, w["rel_bias"], seq)
    r, dec, k, v, a, b, g, bonus = _rwkv_prep(zb, shift0, w, batch, seq)
    y, wkv = _wkv_scan(r, dec, k, v, a, b, wkv0, batch, seq)
    x1, hn, scores = _post_mix(x.reshape(m, D_MODEL), o_a, y, bonus, g, zg, w, batch, seq)
    eidx, gate = _topk(scores)
    eidx_flat = eidx.T.reshape(-1)
    coef = _peer_u(eidx_flat, hn.reshape(m, *ROW_SHAPE), gate, w["expert_u"])
    out = _peer_v(eidx_flat, coef.T.reshape(-1), x1.reshape(m, *ROW_SHAPE), w["expert_v"], w["normf_g"])

    keep = min(BAND_CHUNKS * CHUNK, seq) if k_cache is None else seq
    zk = za.reshape(batch, seq, A_COLS)[:, seq - keep:]
    heads = lambda t: t.reshape(batch, keep, HEADS, HEAD_DIM).transpose(0, 2, 1, 3)
    return (out.reshape(batch, seq, D_MODEL), heads(zk[..., WIDTH:2 * WIDTH]), heads(zk[..., 2 * WIDTH:]), wkv,
            zb.reshape(batch, seq, B_COLS)[:, -1:])


def kernel(x_prompt, x_sample, cache_attn_k, cache_attn_v, state_wkv, state_shift, norm1_g, w_in, rel_bias, shift_mu,
           w_decay0, w_decay_up, a0, w_a_up, w_g_up, k_k, k_a, r_k, lnx_g, lnx_b, w_proj_a, w_proj_b, w_out, norm2_g,
           w_query, sub_keys, expert_u, expert_v, normf_g):
    assert norm1_g.shape[0] == 1, "single-layer step"
    zeros = jnp.zeros((DECAY_RANK, WIDTH), F32)
    score_w = _score_weights(w_query[0], sub_keys[0])
    score_hi = score_w.astype(BF16)
    w = dict(
        norm1_g=norm1_g[0], w_in=w_in[0].astype(BF16), rel_bias=rel_bias[0],
        shift_mu=_row_vec(shift_mu[0]), w_decay0=_row_vec(w_decay0[0]), a0=_row_vec(a0[0]),
        w_decay_up=jnp.concatenate([w_decay_up[0], zeros], axis=0), w_a_up=jnp.concatenate([zeros, w_a_up[0]], axis=0),
        w_g_up=w_g_up[0], k_k=_row_vec(k_k[0]), k_a=_row_vec(k_a[0]), r_k=_row_vec(r_k[0]),
        head_sum=_head_sum_matrix(1.0), head_mean=_head_sum_matrix(1.0 / HEAD_DIM),
        lnx_g=_row_vec(lnx_g[0]), lnx_b=_row_vec(lnx_b[0]),
        w_proj_a=w_proj_a[0].astype(BF16), w_proj_b=w_proj_b[0].astype(BF16), w_out=w_out[0].astype(BF16),
        norm2_g=_row_vec(norm2_g[0]), score_hi=score_hi, score_lo=(score_w - score_hi.astype(F32)).astype(BF16),
        expert_u=_tile_table(expert_u[0]), expert_v=_tile_table(expert_v[0]),
        normf_g=normf_g.reshape(ROW_SHAPE).astype(F32))

    batch = x_prompt.shape[0]
    yp, kp, vp, wp, sp = _stream_step(x_prompt, None, None, jnp.zeros((batch, HEADS, HEAD_DIM, HEAD_DIM), F32),
                                      jnp.zeros((batch, 1, B_COLS), F32), w)
    ys, ks, vs, ws, ss = _stream_step(x_sample, cache_attn_k[0], cache_attn_v[0], state_wkv[0], state_shift[0], w)
    return (yp, ys, kp[None], vp[None], wp[None], sp[None], ks[None], vs[None], ws[None], ss[None])
```

```python
import functools

import jax
import jax.numpy as jnp
from jax import lax
from jax.experimental import pallas as pl
from jax.experimental.pallas import tpu as pltpu

F32 = jnp.float32
BF16 = jnp.bfloat16
HIGHEST = lax.Precision.HIGHEST

D_MODEL = 1024
CHUNK = 64
BAND_CHUNKS = 8
HEADS = 8
HEAD_DIM = 64
WIDTH = HEADS * HEAD_DIM
REL_CLIP = 128
DECAY_RANK = 64
AAA_RANK = 64
GATE_RANK = 128
LNX_EPS = 64e-5
NORM_EPS = 1e-6
P_HEADS = 8
N_KEYS = 128
P_HALF = 64
P_TOPK = 16
A_COLS = 3 * WIDTH
B_COLS = 3 * WIDTH + DECAY_RANK + AAA_RANK + GATE_RANK
G_COLS = 2 * D_MODEL
NEG = -1e30

VMEM_LIMIT = 56 * 1024 * 1024
ROW_TILE = 256
ATT_TILE = BAND_CHUNKS * CHUNK
BAND_KEYS = (BAND_CHUNKS + 1) * CHUNK
SCAN_TILE = 128
SCAN_BATCH = 2
TOK_TILE = 128
SHIFT_GROUP = 32
PAIRS = HEADS // 2
PAIR_W = 2 * HEAD_DIM
N_SEL = P_HEADS * P_TOPK
ROW_SHAPE = (8, D_MODEL // 8)
TILE_WORDS = ROW_SHAPE[0] // 2


def _params(**kw):
    return pltpu.CompilerParams(vmem_limit_bytes=VMEM_LIMIT, **kw)


def _resident(shape):
    nd = len(shape)
    return pl.BlockSpec(shape, lambda *_: (0,) * nd, pipeline_mode=pl.Buffered(1))


def _sigmoid(x):
    return 1.0 / (1.0 + jnp.exp(-x))


def _dot(a, b, **kw):
    return jnp.dot(a, b, preferred_element_type=F32, **kw)


def _dot_nt(a, b, **kw):
    return lax.dot_general(a, b, (((1,), (1,)), ((), ())), preferred_element_type=F32, **kw)


def _row_vec(a):
    return a.reshape(1, -1).astype(F32)


def _inproj_kernel(x_ref, g_ref, w_ref, za_ref, zb_ref, zg_ref):
    x = x_ref[...]
    y = x * lax.rsqrt(jnp.mean(x * x, axis=-1, keepdims=True) + NORM_EPS) * g_ref[...]
    yb = y.astype(BF16)
    za_ref[...] = _dot(yb, w_ref[:, :A_COLS])
    zb_ref[...] = _dot(yb, w_ref[:, A_COLS:A_COLS + B_COLS])
    zg_ref[...] = _dot(yb, w_ref[:, A_COLS + B_COLS:])


def _in_proj(x, norm_g, w_in_bf16):
    m = x.shape[0]
    in_cols = w_in_bf16.shape[1]
    row = lambda i: (i, 0)
    return pl.pallas_call(
        _inproj_kernel,
        grid=(m // ROW_TILE,),
        in_specs=[pl.BlockSpec((ROW_TILE, D_MODEL), row), _resident((1, D_MODEL)), _resident((D_MODEL, in_cols))],
        out_specs=[pl.BlockSpec((ROW_TILE, A_COLS), row), pl.BlockSpec((ROW_TILE, B_COLS), row),
                   pl.BlockSpec((ROW_TILE, G_COLS), row)],
        out_shape=[jax.ShapeDtypeStruct((m, A_COLS), F32), jax.ShapeDtypeStruct((m, B_COLS), F32),
                   jax.ShapeDtypeStruct((m, G_COLS), F32)],
        compiler_params=_params(dimension_semantics=("parallel",)),
        name="in_proj",
    )(x, _row_vec(norm_g), w_in_bf16)


def _softmax_pv(scores, values):
    m = scores[0].max(axis=-1, keepdims=True)
    for s in scores[1:]:
        m = jnp.maximum(m, s.max(axis=-1, keepdims=True))
    acc, den = None, None
    for s, v in zip(scores, values):
        p = jnp.exp(s - m)
        d = p.sum(axis=-1, keepdims=True)
        o = _dot(p.astype(BF16), v)
        acc = o if acc is None else acc + o
        den = d if den is None else den + d
    return acc / den


def _attn_prompt_kernel(q_ref, kp_ref, kc_ref, vp_ref, vc_ref, bias_ref, o_ref):
    first = pl.program_id(1) == 0
    scale = HEAD_DIM ** -0.5
    for h in range(HEADS):
        sl = slice(h * HEAD_DIM, (h + 1) * HEAD_DIM)
        q = q_ref[:, sl].astype(BF16)
        s_prev = _dot_nt(q, kp_ref[:, sl].astype(BF16)) * scale + bias_ref[h, :, :ATT_TILE]
        s_cur = _dot_nt(q, kc_ref[:, sl].astype(BF16)) * scale + bias_ref[h, :, ATT_TILE:]
        s_prev = jnp.where(first, NEG, s_prev)
        o_ref[:, sl] = _softmax_pv([s_prev, s_cur], [vp_ref[:, sl].astype(BF16), vc_ref[:, sl].astype(BF16)])


def _prompt_bias_table(rel_bias):
    dist = jnp.arange(CHUNK)[:, None] + BAND_CHUNKS * CHUNK - jnp.arange(BAND_KEYS)[None, :]
    window = rel_bias[:, jnp.clip(dist, -REL_CLIP, REL_CLIP) + REL_CLIP].astype(F32)
    rows = [jnp.pad(window, ((0, 0), (0, 0), (c * CHUNK, 2 * ATT_TILE - BAND_KEYS - c * CHUNK)), constant_values=NEG)
            for c in range(BAND_CHUNKS)]
    return jnp.concatenate(rows, axis=1)


def _attn_prompt(za, rel_bias, batch, seq):
    nb = seq // ATT_TILE
    blk = (ATT_TILE, WIDTH)
    cur = lambda col: (lambda b, i: (b * nb + i, col))
    prev = lambda col: (lambda b, i: (b * nb + jnp.maximum(i - 1, 0), col))
    return pl.pallas_call(
        _attn_prompt_kernel,
        grid=(batch, nb),
        in_specs=[pl.BlockSpec(blk, cur(0)), pl.BlockSpec(blk, prev(1)), pl.BlockSpec(blk, cur(1)),
                  pl.BlockSpec(blk, prev(2)), pl.BlockSpec(blk, cur(2)),
                  _resident((HEADS, ATT_TILE, 2 * ATT_TILE))],
        out_specs=pl.BlockSpec(blk, cur(0)),
        out_shape=jax.ShapeDtypeStruct((batch * seq, WIDTH), F32),
        compiler_params=_params(dimension_semantics=("parallel", "arbitrary")),
        name="attn_prompt",
    )(za, za, za, za, za, _prompt_bias_table(rel_bias))


def _attn_sample_kernel(q_ref, kn_ref, vn_ref, kc_ref, vc_ref, bc_ref, bn_ref, o_ref):
    scale = HEAD_DIM ** -0.5
    for h in range(HEADS):
        sl = slice(h * HEAD_DIM, (h + 1) * HEAD_DIM)
        q = q_ref[:, sl].astype(BF16)
        s_cache = _dot_nt(q, kc_ref[0, h].astype(BF16)) * scale + bc_ref[h]
        s_new = _dot_nt(q, kn_ref[:, sl].astype(BF16)) * scale + bn_ref[h]
        o_ref[:, sl] = _softmax_pv([s_cache, s_new], [vc_ref[0, h].astype(BF16), vn_ref[:, sl].astype(BF16)])


def _attn_sample(za, k_cache, v_cache, rel_bias, seq):
    nb, _, past, _ = k_cache.shape
    dist = jnp.arange(seq)[:, None] + past - jnp.arange(past + seq)[None, :]
    bias = rel_bias[:, jnp.clip(dist, -REL_CLIP, REL_CLIP) + REL_CLIP].astype(F32)
    blk = (seq, WIDTH)
    rows = lambda col: (lambda b: (b, col))
    cache = pl.BlockSpec((1, HEADS, past, HEAD_DIM), lambda b: (b, 0, 0, 0))
    return pl.pallas_call(
        _attn_sample_kernel,
        grid=(nb,),
        in_specs=[pl.BlockSpec(blk, rows(0)), pl.BlockSpec(blk, rows(1)), pl.BlockSpec(blk, rows(2)), cache, cache,
                  _resident((HEADS, seq, past)), _resident((HEADS, seq, seq))],
        out_specs=pl.BlockSpec(blk, rows(0)),
        out_shape=jax.ShapeDtypeStruct((nb * seq, WIDTH), F32),
        compiler_params=_params(dimension_semantics=("parallel",)),
        name="attn_sample",
    )(za, za, za, k_cache, v_cache, bias[:, :, :past], bias[:, :, past:])


def _softplus(x):
    return jnp.maximum(x, 0.0) + jnp.log(1.0 + jnp.exp(-jnp.abs(x)))


def _rwkv_prep_kernel(zb_ref, prev_ref, shift_ref, mu_ref, wd0_ref, wdu_ref, a0_ref, wau_ref, wgu_ref, kk_ref, ka_ref,
                      rk_ref, gsum_ref, r_o, w_o, k_o, v_o, a_o, b_o, g_o, bonus_o, *, seq):
    i = pl.program_id(0)
    zb = zb_ref[...]
    tm = zb.shape[0]
    row = lax.broadcasted_iota(jnp.int32, zb.shape, 0)
    prev = jnp.where(row == 0, prev_ref[7:8, :], pltpu.roll(zb, 1, 0))
    ngrp = tm // SHIFT_GROUP
    shift = jnp.broadcast_to(shift_ref[...][:, None, :], (ngrp, SHIFT_GROUP, B_COLS)).reshape(tm, B_COLS)
    prev = jnp.where(lax.rem(i * tm + row, seq) == 0, shift, prev)
    zm = zb + (prev - zb) * mu_ref[...]
    r = zm[:, 0:WIDTH]
    k = zm[:, WIDTH:2 * WIDTH]
    v = zm[:, 2 * WIDTH:3 * WIDTH]
    lora_in = zm[:, 3 * WIDTH:3 * WIDTH + DECAY_RANK + AAA_RANK]
    gate_in = zm[:, 3 * WIDTH + DECAY_RANK + AAA_RANK:]
    w_log = -_softplus(-(wd0_ref[...] + _dot(jnp.tanh(lora_in), wdu_ref[...], precision=HIGHEST))) - 0.5
    decay = jnp.exp(-jnp.exp(w_log))
    a = _sigmoid(a0_ref[...] + _dot(lora_in, wau_ref[...], precision=HIGHEST))
    g = _dot(_sigmoid(gate_in), wgu_ref[...], precision=HIGHEST)
    kk = k * kk_ref[...]
    kk = kk / jnp.maximum(jnp.sqrt(_dot(kk * kk, gsum_ref[...], precision=HIGHEST)), 1e-12)
    kmod = k * (1.0 + (a - 1.0) * ka_ref[...])
    for ref, val in ((r_o, r), (w_o, decay), (k_o, kmod), (v_o, v), (a_o, -kk), (b_o, kk * a)):
        for pair in range(PAIRS):
            ref[pair] = val[:, pair * PAIR_W:(pair + 1) * PAIR_W].reshape(ref.shape[1:])
    g_o[...] = g
    bonus_o[...] = _dot(r * kmod * rk_ref[...], gsum_ref[...], precision=HIGHEST) * v


def _head_sum_matrix(scale):
    h = jnp.arange(WIDTH) // HEAD_DIM
    return jnp.where(h[:, None] == h[None, :], scale, 0.0).astype(F32)


def _scan_layout(batch, seq):
    steps = min(seq, SCAN_TILE)
    shape = (PAIRS, seq // steps, batch, steps, PAIR_W)
    if seq >= ROW_TILE:
        per_seq = seq // ROW_TILE
        block = (PAIRS, ROW_TILE // steps, 1, steps, PAIR_W)
        index = lambda i: (0, i % per_seq, i // per_seq, 0, 0)
    else:
        block = (PAIRS, 1, ROW_TILE // seq, steps, PAIR_W)
        index = lambda i: (0, 0, i, 0, 0)
    return steps, shape, pl.BlockSpec(block, index)


def _rwkv_prep(zb, shift0, w, batch, seq):
    m = zb.shape[0]
    tm = ROW_TILE
    row = lambda i: (i, 0)
    groups_per_seq = seq // SHIFT_GROUP
    shift_rows = jnp.zeros((batch, groups_per_seq, B_COLS), F32).at[:, 0].set(shift0.reshape(batch, B_COLS))
    _, scan_shape, scan_spec = _scan_layout(batch, seq)
    flat = jax.ShapeDtypeStruct((m, WIDTH), F32)
    return pl.pallas_call(
        functools.partial(_rwkv_prep_kernel, seq=seq),
        grid=(m // tm,),
        in_specs=[pl.BlockSpec((tm, B_COLS), row),
                  pl.BlockSpec((8, B_COLS), lambda i: (jnp.maximum(i * (tm // 8) - 1, 0), 0)),
                  pl.BlockSpec((tm // SHIFT_GROUP, B_COLS), row),
                  _resident((1, B_COLS)), _resident((1, WIDTH)), _resident((DECAY_RANK + AAA_RANK, WIDTH)),
                  _resident((1, WIDTH)), _resident((DECAY_RANK + AAA_RANK, WIDTH)), _resident((GATE_RANK, WIDTH)),
                  _resident((1, WIDTH)), _resident((1, WIDTH)), _resident((1, WIDTH)), _resident((WIDTH, WIDTH))],
        out_specs=[scan_spec] * 6 + [pl.BlockSpec((tm, WIDTH), row)] * 2,
        out_shape=[jax.ShapeDtypeStruct(scan_shape, F32)] * 6 + [flat] * 2,
        compiler_params=_params(dimension_semantics=("parallel",)),
        name="rwkv_prep",
    )(zb, zb, shift_rows.reshape(m // SHIFT_GROUP, B_COLS), w["shift_mu"], w["w_decay0"], w["w_decay_up"], w["a0"],
      w["w_a_up"], w["w_g_up"], w["k_k"], w["k_a"], w["r_k"], w["head_sum"])


def _scan_kernel(r_ref, w_ref, k_ref, v_ref, a_ref, b_ref, s0_ref, y_ref, st_ref, s_scr, *, steps):
    tb = pl.program_id(1)
    chains = [(b, p) for b in range(s_scr.shape[0]) for p in range(PAIRS)]

    @pl.when(tb == 0)
    def _():
        s_scr[...] = s0_ref[...]

    lane = lax.broadcasted_iota(jnp.int32, (HEAD_DIM, PAIR_W), 1)
    sub = lax.broadcasted_iota(jnp.int32, (HEAD_DIM, PAIR_W), 0)
    lo = lane < HEAD_DIM
    diag = (lane & (HEAD_DIM - 1)) == sub
    sub8 = lax.broadcasted_iota(jnp.int32, (8, PAIR_W), 0)

    def head_sums(x):
        s_lo = jnp.sum(jnp.where(lo, x, 0.0), axis=1, keepdims=True)
        s_hi = jnp.sum(jnp.where(lo, 0.0, x), axis=1, keepdims=True)
        return jnp.where(lo, s_lo, s_hi)

    def row_of(ref, chain, t):
        b, p = chain
        tile = ref[p, 0, b, pl.ds(pl.multiple_of((t // 8) * 8, 8), 8), :]
        return jnp.sum(jnp.where(sub8 == (t % 8), tile, 0.0), axis=0, keepdims=True)

    def emit_y(chain, t, s):
        b, p = chain
        y_col = head_sums(s * row_of(r_ref, chain, t))
        y_ref[p, 0, b, pl.ds(t, 1), :] = jnp.sum(jnp.where(diag, y_col, 0.0), axis=0, keepdims=True)

    def update(chain, t, s):
        sa = head_sums(s * row_of(a_ref, chain, t))
        v_col = head_sums(jnp.where(diag, row_of(v_ref, chain, t), 0.0))
        return s * row_of(w_ref, chain, t) + sa * row_of(b_ref, chain, t) + v_col * row_of(k_ref, chain, t)

    for b, p in chains:
        s_scr[b, p] = update((b, p), 0, s_scr[b, p])

    def step(t, carry):
        for b, p in chains:
            s = s_scr[b, p]
            emit_y((b, p), t - 1, s)
            s_scr[b, p] = update((b, p), t, s)
        return carry

    lax.fori_loop(1, steps, step, 0)
    for b, p in chains:
        emit_y((b, p), steps - 1, s_scr[b, p])

    @pl.when(tb == pl.num_programs(1) - 1)
    def _():
        st_ref[...] = s_scr[...]


def _pair_state(s):
    b = s.shape[0]
    return s.reshape(b, PAIRS, 2, HEAD_DIM, HEAD_DIM).transpose(0, 1, 3, 2, 4).reshape(b, PAIRS, HEAD_DIM, PAIR_W)


def _unpair_state(s):
    b = s.shape[0]
    return s.reshape(b, PAIRS, HEAD_DIM, 2, HEAD_DIM).transpose(0, 1, 3, 2, 4).reshape(b, HEADS, HEAD_DIM, HEAD_DIM)


def _wkv_scan(r, w, k, v, a, b, s0, batch, seq):
    steps, shape, _ = _scan_layout(batch, seq)
    bg = SCAN_BATCH
    blk = pl.BlockSpec((PAIRS, 1, bg, steps, PAIR_W), lambda bi, ti: (0, ti, bi, 0, 0))
    state = pl.BlockSpec((bg, PAIRS, HEAD_DIM, PAIR_W), lambda bi, ti: (bi, 0, 0, 0))
    y, st = pl.pallas_call(
        functools.partial(_scan_kernel, steps=steps),
        grid=(batch // bg, seq // steps),
        in_specs=[blk] * 6 + [state],
        out_specs=[blk, state],
        out_shape=[jax.ShapeDtypeStruct(shape, F32), jax.ShapeDtypeStruct((batch, PAIRS, HEAD_DIM, PAIR_W), F32)],
        scratch_shapes=[pltpu.VMEM((bg, PAIRS, HEAD_DIM, PAIR_W), F32)],
        compiler_params=_params(dimension_semantics=("parallel", "arbitrary")),
        name="wkv_scan",
    )(r, w, k, v, a, b, _pair_state(s0.astype(F32)))
    return y, _unpair_state(st)


def _postmix_kernel(x_ref, oa_ref, y_ref, bonus_ref, g_ref, zg_ref, lng_ref, lnb_ref, gmean_ref, wpa_ref, wpb_ref,
                    wout_ref, n2g_ref, wch_ref, wcl_ref, x1_ref, hn_ref, sc_ref):
    tm = x_ref.shape[0]
    y = jnp.concatenate([y_ref[pair].reshape(tm, PAIR_W) for pair in range(PAIRS)], axis=1)
    mu = _dot(y, gmean_ref[...], precision=HIGHEST)
    d = y - mu
    var = _dot(d * d, gmean_ref[...], precision=HIGHEST)
    yn = d * lax.rsqrt(var + LNX_EPS) * lng_ref[...] + lnb_ref[...]
    ob = (yn + bonus_ref[...]) * g_ref[...]
    pa = _dot(oa_ref[...].astype(BF16), wpa_ref[...])
    pb = _dot(ob.astype(BF16), wpb_ref[...])
    merged = _sigmoid(zg_ref[:, :D_MODEL]) * pa + _sigmoid(zg_ref[:, D_MODEL:]) * pb
    x1 = x_ref[...] + _dot(merged.astype(BF16), wout_ref[...])
    x1_ref[...] = x1
    hn = x1 * lax.rsqrt(jnp.mean(x1 * x1, axis=-1, keepdims=True) + NORM_EPS) * n2g_ref[...]
    hn_ref[...] = hn
    hh = hn.astype(BF16)
    hl = (hn - hh.astype(F32)).astype(BF16)
    sc_ref[...] = _dot_nt(wch_ref[...], hh) + (_dot_nt(wcl_ref[...], hh) + _dot_nt(wch_ref[...], hl))


def _post_mix(x, o_a, y, bonus, g, zg, w, batch, seq):
    m = x.shape[0]
    tm = ROW_TILE
    row = lambda i: (i, 0)
    nsc = w["score_hi"].shape[0]
    wide = pl.BlockSpec((tm, D_MODEL), row)
    half = pl.BlockSpec((tm, WIDTH), row)
    _, _, scan_spec = _scan_layout(batch, seq)
    return pl.pallas_call(
        _postmix_kernel,
        grid=(m // tm,),
        in_specs=[wide, half, scan_spec, half, half, pl.BlockSpec((tm, G_COLS), row),
                  _resident((1, WIDTH)), _resident((1, WIDTH)), _resident((WIDTH, WIDTH)),
                  _resident((WIDTH, D_MODEL)), _resident((WIDTH, D_MODEL)), _resident((D_MODEL, D_MODEL)),
                  _resident((1, D_MODEL)), _resident((nsc, D_MODEL)), _resident((nsc, D_MODEL))],
        out_specs=[wide, wide, pl.BlockSpec((nsc, tm), lambda i: (0, i))],
        out_shape=[jax.ShapeDtypeStruct((m, D_MODEL), F32), jax.ShapeDtypeStruct((m, D_MODEL), F32),
                   jax.ShapeDtypeStruct((nsc, m), F32)],
        compiler_params=_params(dimension_semantics=("parallel",)),
        name="post_mix",
    )(x, o_a, y, bonus, g, zg, w["lnx_g"], w["lnx_b"], w["head_mean"], w["w_proj_a"], w["w_proj_b"], w["w_out"],
      w["norm2_g"], w["score_hi"], w["score_lo"])


def _score_weight_kernel(wq_ref, sk_ref, o_ref):
    for c in range(2):
        wq = wq_ref[:, c * P_HALF:(c + 1) * P_HALF]
        o_ref[c * N_KEYS:(c + 1) * N_KEYS, :] = _dot_nt(sk_ref[0, c], wq, precision=HIGHEST)


def _score_weights(w_query, sub_keys):
    return pl.pallas_call(
        _score_weight_kernel,
        grid=(P_HEADS,),
        in_specs=[pl.BlockSpec((D_MODEL, 2 * P_HALF), lambda h: (0, h)),
                  pl.BlockSpec((1, 2, N_KEYS, P_HALF), lambda h: (h, 0, 0, 0))],
        out_specs=pl.BlockSpec((2 * N_KEYS, D_MODEL), lambda h: (h, 0)),
        out_shape=jax.ShapeDtypeStruct((P_HEADS * 2 * N_KEYS, D_MODEL), F32),
        compiler_params=_params(dimension_semantics=("parallel",)),
        name="score_weights",
    )(w_query, sub_keys)


N_CAND = P_TOPK + 7 * 8 + 8


def _extract_topk(vals, ids, n, payload=None):
    top_v, top_i = [], []
    big = jnp.int32(2 ** 30)
    for _ in range(n):
        m = jnp.max(vals, axis=0, keepdims=True)
        pick = jnp.min(jnp.where(vals == m, ids, big), axis=0, keepdims=True)
        sel = ids == pick
        top_v.append(m)
        top_i.append(pick if payload is None else jnp.max(jnp.where(sel, payload, -1), axis=0, keepdims=True))
        vals = jnp.where(sel, -jnp.inf, vals)
    return top_v, top_i


def _topk_kernel(sc_ref, eidx_ref, gate_ref):
    lanes = sc_ref.shape[1]
    key_id = lax.broadcasted_iota(jnp.int32, (N_KEYS, lanes), 0)
    sub16 = lax.broadcasted_iota(jnp.int32, (P_TOPK, lanes), 0)
    sub8 = lax.broadcasted_iota(jnp.int32, (8, lanes), 0)
    cand_id = jnp.concatenate([sub16] + [a * P_TOPK + sub8 for a in range(1, 8)] + [(8 + sub8) * P_TOPK], axis=0)

    def head(h, carry):
        base = pl.multiple_of(h * 2 * N_KEYS, 2 * N_KEYS)
        v0, i0 = _extract_topk(sc_ref[pl.ds(base, N_KEYS), :], key_id, P_TOPK)
        v1, i1 = _extract_topk(sc_ref[pl.ds(base + N_KEYS, N_KEYS), :], key_id, P_TOPK)
        v1_16, i1_16 = jnp.concatenate(v1, axis=0), jnp.concatenate(i1, axis=0)
        v1_8, i1_8 = jnp.concatenate(v1[:8], axis=0), jnp.concatenate(i1[:8], axis=0)
        cand = jnp.concatenate([v0[0] + v1_16] + [v0[a] + v1_8 for a in range(1, 8)]
                               + [jnp.concatenate(v0[8:], axis=0) + v1[0]], axis=0)
        cidx = jnp.concatenate([i0[0] * N_KEYS + i1_16] + [i0[a] * N_KEYS + i1_8 for a in range(1, 8)]
                               + [jnp.concatenate(i0[8:], axis=0) * N_KEYS + i1[0]], axis=0) * TILE_WORDS
        fv, fe = _extract_topk(cand, cand_id, P_TOPK, payload=cidx)
        fv = jnp.concatenate(fv, axis=0)
        e = jnp.exp(fv - fv[0:1])
        out = pl.multiple_of(h * P_TOPK, P_TOPK)
        gate_ref[pl.ds(out, P_TOPK), :] = e / jnp.sum(e, axis=0, keepdims=True)
        eidx_ref[pl.ds(out, P_TOPK), :] = jnp.concatenate(fe, axis=0)
        return carry

    lax.fori_loop(0, P_HEADS, head, 0)


def _topk(scores):
    nsc, m = scores.shape
    tl = 2 * TOK_TILE
    col = lambda i: (0, i)
    return pl.pallas_call(
        _topk_kernel,
        grid=(m // tl,),
        in_specs=[pl.BlockSpec((nsc, tl), col)],
        out_specs=[pl.BlockSpec((P_HEADS * P_TOPK, tl), col)] * 2,
        out_shape=[jax.ShapeDtypeStruct((P_HEADS * P_TOPK, m), jnp.int32),
                   jax.ShapeDtypeStruct((P_HEADS * P_TOPK, m), F32)],
        compiler_params=_params(dimension_semantics=("parallel",)),
        name="topk",
    )(scores)


U_ROW_GROUP = 64


def _tile_table(t):
    pairs = t.astype(BF16).reshape(t.shape[0], TILE_WORDS, 2, ROW_SHAPE[1]).transpose(0, 1, 3, 2)
    return lax.bitcast_convert_type(pairs, jnp.uint32).reshape(t.shape[0] * TILE_WORDS, ROW_SHAPE[1])


def _expert_row(tbl_ref, word_row):
    words = tbl_ref[pl.ds(pl.multiple_of(word_row, TILE_WORDS), TILE_WORDS), :]
    return pltpu.bitcast(words, BF16).astype(F32)


def _sublane_fold(x, y, step, mask):
    return jnp.where(mask, x + pltpu.roll(x, 8 - step, 0), y + pltpu.roll(y, step, 0))


def _peer_u_kernel(idx_ref, hn_ref, gate_ref, tbl_ref, coef_ref, part_scr):
    toks = hn_ref.shape[0]
    sub = lax.broadcasted_iota(jnp.int32, ROW_SHAPE, 0)
    m4, m2, m1 = sub < 4, (sub & 3) < 2, (sub & 1) == 0

    def token(t, carry):
        x = hn_ref[t]

        def group(g, carry):
            first = t * N_SEL + g * U_ROW_GROUP
            for h in range(U_ROW_GROUP // 8):
                prod = [_expert_row(tbl_ref, idx_ref[first + h * 8 + j]) * x for j in range(8)]
                z = [_sublane_fold(prod[a], prod[a + 4], 4, m4) for a in (0, 2, 1, 3)]
                w0 = _sublane_fold(z[0], z[1], 2, m2)
                w1 = _sublane_fold(z[2], z[3], 2, m2)
                rows = pl.ds(pl.multiple_of(first + h * 8, 8), 8)
                part_scr[rows, :] = _sublane_fold(w0, w1, 1, m1)
            return carry

        return lax.fori_loop(0, N_SEL // U_ROW_GROUP, group, carry)

    lax.fori_loop(0, toks, token, 0)

    lane_tok = lax.broadcasted_iota(jnp.int32, (N_SEL, toks), 1)
    tok_unroll = 8

    def reduce_tokens(i, hid):
        for u in range(tok_unroll):
            t = i * tok_unroll + u
            rows = pl.ds(pl.multiple_of(t * N_SEL, N_SEL), N_SEL)
            hid = jnp.where(lane_tok == t, jnp.sum(part_scr[rows, :], axis=1, keepdims=True), hid)
        return hid

    hid = lax.fori_loop(0, toks // tok_unroll, reduce_tokens, jnp.zeros((N_SEL, toks), F32))
    coef_ref[...] = gate_ref[...] * (0.5 * hid * (1.0 + lax.erf(hid * (2.0 ** -0.5))))


def _peer_u(eidx_flat, hn3, gate, table):
    m = hn3.shape[0]
    tt = TOK_TILE
    return pl.pallas_call(
        _peer_u_kernel,
        grid=(m // tt,),
        in_specs=[pl.BlockSpec((tt * N_SEL,), lambda i: (i,), memory_space=pltpu.SMEM),
                  pl.BlockSpec((tt, *ROW_SHAPE), lambda i: (i, 0, 0)),
                  pl.BlockSpec((N_SEL, tt), lambda i: (0, i)),
                  _resident(table.shape)],
        out_specs=pl.BlockSpec((N_SEL, tt), lambda i: (0, i)),
        out_shape=jax.ShapeDtypeStruct((N_SEL, m), F32),
        scratch_shapes=[pltpu.VMEM((tt * N_SEL, ROW_SHAPE[1]), F32)],
        compiler_params=_params(dimension_semantics=("parallel",)),
        name="peer_u",
    )(eidx_flat, hn3, gate, table)


def _peer_v_kernel(idx_ref, coef_ref, x_ref, tbl_ref, g_ref, y_ref, splat_scr):
    toks = x_ref.shape[0]
    n_acc = 4
    lane_tok = lax.broadcasted_iota(jnp.int32, (N_SEL, toks), 1)

    def splat(t):
        col = jnp.sum(jnp.where(lane_tok == t, coef_ref[...], 0.0), axis=1, keepdims=True)
        return jnp.broadcast_to(col, (N_SEL, ROW_SHAPE[1]))

    splat_scr[0] = splat(0)

    def token(t, carry):
        slot = t % 2
        nxt = splat(jnp.minimum(t + 1, toks - 1))
        acc = [jnp.zeros(ROW_SHAPE, F32) for _ in range(n_acc)]
        for h in range(N_SEL // 8):
            coef = splat_scr[slot, h * 8:(h + 1) * 8, :]
            for j in range(8):
                row = _expert_row(tbl_ref, idx_ref[t * N_SEL + h * 8 + j])
                acc[j % n_acc] = acc[j % n_acc] + coef[j:j + 1, :] * row
        y_ref[t] = x_ref[t] + ((acc[0] + acc[1]) + (acc[2] + acc[3]))
        splat_scr[1 - slot] = nxt
        return carry

    lax.fori_loop(0, toks, token, 0)
    x2 = y_ref[...]
    ms = jnp.sum(jnp.sum(x2 * x2, axis=2, keepdims=True), axis=1, keepdims=True) * (1.0 / D_MODEL)
    y_ref[...] = x2 * lax.rsqrt(ms + NORM_EPS) * g_ref[...]


def _peer_v(eidx_flat, coef, x3, table, normf_g):
    m = x3.shape[0]
    tt = TOK_TILE
    tok = pl.BlockSpec((tt, *ROW_SHAPE), lambda i: (i, 0, 0))
    return pl.pallas_call(
        _peer_v_kernel,
        grid=(m // tt,),
        in_specs=[pl.BlockSpec((tt * N_SEL,), lambda i: (i,), memory_space=pltpu.SMEM),
                  pl.BlockSpec((N_SEL, tt), lambda i: (0, i)), tok, _resident(table.shape), _resident(ROW_SHAPE)],
        out_specs=tok,
        out_shape=jax.ShapeDtypeStruct(x3.shape, F32),
        scratch_shapes=[pltpu.VMEM((2, N_SEL, ROW_SHAPE[1]), F32)],
        compiler_params=_params(dimension_semantics=("parallel",)),
        name="peer_v",
    )(eidx_flat, coef, x3, table, normf_g)


def _stream_step(x, k_cache, v_cache, wkv0, shift0, w):
    batch, seq, _ = x.shape
    m = batch * seq
    assert m % (2 * TOK_TILE) == 0 and batch % SCAN_BATCH == 0 and seq % SHIFT_GROUP == 0
    assert (seq % ROW_TILE == 0 and seq % SCAN_TILE == 0) or ROW_TILE % seq == 0
    za, zb, zg = _in_proj(x.reshape(m, D_MODEL), w["norm1_g"], w["w_in"])
    if k_cache is None:
        assert seq % ATT_TILE == 0
        o_a = _attn_prompt(za, w["rel_bias"], batch, seq)
    else:
        o_a = _attn_sample(za, k_cache, v_cache, w["rel_bias"], seq)
    r, dec, k, v, a, b, g, bonus = _rwkv_prep(zb, shift0, w, batch, seq)
    y, wkv = _wkv_scan(r, dec, k, v, a, b, wkv0, batch, seq)
    x1, hn, scores = _post_mix(x.reshape(m, D_MODEL), o_a, y, bonus, g, zg, w, batch, seq)
    eidx, gate = _topk(scores)
    eidx_flat = eidx.T.reshape(-1)
    coef = _peer_u(eidx_flat, hn.reshape(m, *ROW_SHAPE), gate, w["expert_u"])
    out = _peer_v(eidx_flat, coef, x1.reshape(m, *ROW_SHAPE), w["expert_v"], w["normf_g"])

    keep = min(BAND_CHUNKS * CHUNK, seq) if k_cache is None else seq
    zk = za.reshape(batch, seq, A_COLS)[:, seq - keep:]
    heads = lambda t: t.reshape(batch, keep, HEADS, HEAD_DIM).transpose(0, 2, 1, 3)
    return (out.reshape(batch, seq, D_MODEL), heads(zk[..., WIDTH:2 * WIDTH]), heads(zk[..., 2 * WIDTH:]), wkv,
            zb.reshape(batch, seq, B_COLS)[:, -1:])


def kernel(x_prompt, x_sample, cache_attn_k, cache_attn_v, state_wkv, state_shift, norm1_g, w_in, rel_bias, shift_mu,
           w_decay0, w_decay_up, a0, w_a_up, w_g_up, k_k, k_a, r_k, lnx_g, lnx_b, w_proj_a, w_proj_b, w_out, norm2_g,
           w_query, sub_keys, expert_u, expert_v, normf_g):
    assert norm1_g.shape[0] == 1, "single-layer step"
    zeros = jnp.zeros((DECAY_RANK, WIDTH), F32)
    score_w = _score_weights(w_query[0], sub_keys[0])
    score_hi = score_w.astype(BF16)
    w = dict(
        norm1_g=norm1_g[0], w_in=w_in[0].astype(BF16), rel_bias=rel_bias[0],
        shift_mu=_row_vec(shift_mu[0]), w_decay0=_row_vec(w_decay0[0]), a0=_row_vec(a0[0]),
        w_decay_up=jnp.concatenate([w_decay_up[0], zeros], axis=0), w_a_up=jnp.concatenate([zeros, w_a_up[0]], axis=0),
        w_g_up=w_g_up[0], k_k=_row_vec(k_k[0]), k_a=_row_vec(k_a[0]), r_k=_row_vec(r_k[0]),
        head_sum=_head_sum_matrix(1.0), head_mean=_head_sum_matrix(1.0 / HEAD_DIM),
        lnx_g=_row_vec(lnx_g[0]), lnx_b=_row_vec(lnx_b[0]),
        w_proj_a=w_proj_a[0].astype(BF16), w_proj_b=w_proj_b[0].astype(BF16), w_out=w_out[0].astype(BF16),
        norm2_g=_row_vec(norm2_g[0]), score_hi=score_hi, score_lo=(score_w - score_hi.astype(F32)).astype(BF16),
        expert_u=_tile_table(expert_u[0]), expert_v=_tile_table(expert_v[0]),
        normf_g=normf_g.reshape(ROW_SHAPE).astype(F32))

    batch = x_prompt.shape[0]
    yp, kp, vp, wp, sp = _stream_step(x_prompt, None, None, jnp.zeros((batch, HEADS, HEAD_DIM, HEAD_DIM), F32),
                                      jnp.zeros((batch, 1, B_COLS), F32), w)
    ys, ks, vs, ws, ss = _stream_step(x_sample, cache_attn_k[0], cache_attn_v[0], state_wkv[0], state_shift[0], w)
    return (yp, ys, kp[None], vp[None], wp[None], sp[None], ks[None], vs[None], ws[None], ss[None])
```

```python
import functools

import jax
import jax.numpy as jnp
from jax import lax
from jax.experimental import pallas as pl
from jax.experimental.pallas import tpu as pltpu

F32 = jnp.float32
BF16 = jnp.bfloat16
HIGHEST = lax.Precision.HIGHEST

D_MODEL = 1024
CHUNK = 64
BAND_CHUNKS = 8
HEADS = 8
HEAD_DIM = 64
WIDTH = HEADS * HEAD_DIM
REL_CLIP = 128
DECAY_RANK = 64
AAA_RANK = 64
GATE_RANK = 128
LNX_EPS = 64e-5
NORM_EPS = 1e-6
P_HEADS = 8
N_KEYS = 128
P_HALF = 64
P_TOPK = 16
A_COLS = 3 * WIDTH
B_COLS = 3 * WIDTH + DECAY_RANK + AAA_RANK + GATE_RANK
G_COLS = 2 * D_MODEL
NEG = -1e30

VMEM_LIMIT = 56 * 1024 * 1024
ROW_TILE = 256
ATT_TILE = BAND_CHUNKS * CHUNK
BAND_KEYS = (BAND_CHUNKS + 1) * CHUNK
SCAN_TILE = 128
SCAN_BATCH = 2
SCAN_UNROLL = 4
TOK_TILE = 128
SHIFT_GROUP = 32
PAIRS = HEADS // 2
PAIR_W = 2 * HEAD_DIM
N_SEL = P_HEADS * P_TOPK
ROW_SHAPE = (8, D_MODEL // 8)
TILE_WORDS = ROW_SHAPE[0] // 2


def _params(**kw):
    return pltpu.CompilerParams(vmem_limit_bytes=VMEM_LIMIT, **kw)


def _resident(shape):
    nd = len(shape)
    return pl.BlockSpec(shape, lambda *_: (0,) * nd, pipeline_mode=pl.Buffered(1))


def _sigmoid(x):
    return 1.0 / (1.0 + jnp.exp(-x))


def _dot(a, b, **kw):
    return jnp.dot(a, b, preferred_element_type=F32, **kw)


def _dot_nt(a, b, **kw):
    return lax.dot_general(a, b, (((1,), (1,)), ((), ())), preferred_element_type=F32, **kw)


def _row_vec(a):
    return a.reshape(1, -1).astype(F32)


def _inproj_kernel(x_ref, g_ref, w_ref, za_ref, zb_ref, zg_ref):
    x = x_ref[...]
    y = x * lax.rsqrt(jnp.mean(x * x, axis=-1, keepdims=True) + NORM_EPS) * g_ref[...]
    yb = y.astype(BF16)
    za_ref[...] = _dot(yb, w_ref[:, :A_COLS])
    zb_ref[...] = _dot(yb, w_ref[:, A_COLS:A_COLS + B_COLS])
    zg_ref[...] = _dot(yb, w_ref[:, A_COLS + B_COLS:])


def _in_proj(x, norm_g, w_in_bf16):
    m = x.shape[0]
    in_cols = w_in_bf16.shape[1]
    row = lambda i: (i, 0)
    return pl.pallas_call(
        _inproj_kernel,
        grid=(m // ROW_TILE,),
        in_specs=[pl.BlockSpec((ROW_TILE, D_MODEL), row), _resident((1, D_MODEL)), _resident((D_MODEL, in_cols))],
        out_specs=[pl.BlockSpec((ROW_TILE, A_COLS), row), pl.BlockSpec((ROW_TILE, B_COLS), row),
                   pl.BlockSpec((ROW_TILE, G_COLS), row)],
        out_shape=[jax.ShapeDtypeStruct((m, A_COLS), F32), jax.ShapeDtypeStruct((m, B_COLS), F32),
                   jax.ShapeDtypeStruct((m, G_COLS), F32)],
        compiler_params=_params(dimension_semantics=("parallel",)),
        name="in_proj",
    )(x, _row_vec(norm_g), w_in_bf16)


def _softmax_pv(scores, values):
    m = scores[0].max(axis=-1, keepdims=True)
    for s in scores[1:]:
        m = jnp.maximum(m, s.max(axis=-1, keepdims=True))
    acc, den = None, None
    for s, v in zip(scores, values):
        p = jnp.exp(s - m)
        d = p.sum(axis=-1, keepdims=True)
        o = _dot(p.astype(BF16), v)
        acc = o if acc is None else acc + o
        den = d if den is None else den + d
    return acc / den


def _attn_prompt_kernel(q_ref, kp_ref, kc_ref, vp_ref, vc_ref, bias_ref, o_ref):
    first = pl.program_id(1) == 0
    scale = HEAD_DIM ** -0.5
    for h in range(HEADS):
        sl = slice(h * HEAD_DIM, (h + 1) * HEAD_DIM)
        q = q_ref[:, sl].astype(BF16)
        s_prev = _dot_nt(q, kp_ref[:, sl].astype(BF16)) * scale + bias_ref[h, :, :ATT_TILE]
        s_cur = _dot_nt(q, kc_ref[:, sl].astype(BF16)) * scale + bias_ref[h, :, ATT_TILE:]
        s_prev = jnp.where(first, NEG, s_prev)
        o_ref[:, sl] = _softmax_pv([s_prev, s_cur], [vp_ref[:, sl].astype(BF16), vc_ref[:, sl].astype(BF16)])


def _prompt_bias_table(rel_bias):
    dist = jnp.arange(CHUNK)[:, None] + BAND_CHUNKS * CHUNK - jnp.arange(BAND_KEYS)[None, :]
    window = rel_bias[:, jnp.clip(dist, -REL_CLIP, REL_CLIP) + REL_CLIP].astype(F32)
    rows = [jnp.pad(window, ((0, 0), (0, 0), (c * CHUNK, 2 * ATT_TILE - BAND_KEYS - c * CHUNK)), constant_values=NEG)
            for c in range(BAND_CHUNKS)]
    return jnp.concatenate(rows, axis=1)


def _attn_prompt(za, rel_bias, batch, seq):
    nb = seq // ATT_TILE
    blk = (ATT_TILE, WIDTH)
    cur = lambda col: (lambda b, i: (b * nb + i, col))
    prev = lambda col: (lambda b, i: (b * nb + jnp.maximum(i - 1, 0), col))
    return pl.pallas_call(
        _attn_prompt_kernel,
        grid=(batch, nb),
        in_specs=[pl.BlockSpec(blk, cur(0)), pl.BlockSpec(blk, prev(1)), pl.BlockSpec(blk, cur(1)),
                  pl.BlockSpec(blk, prev(2)), pl.BlockSpec(blk, cur(2)),
                  _resident((HEADS, ATT_TILE, 2 * ATT_TILE))],
        out_specs=pl.BlockSpec(blk, cur(0)),
        out_shape=jax.ShapeDtypeStruct((batch * seq, WIDTH), F32),
        compiler_params=_params(dimension_semantics=("parallel", "arbitrary")),
        name="attn_prompt",
    )(za, za, za, za, za, _prompt_bias_table(rel_bias))


def _attn_sample_kernel(q_ref, kn_ref, vn_ref, kc_ref, vc_ref, bc_ref, bn_ref, o_ref):
    scale = HEAD_DIM ** -0.5
    for h in range(HEADS):
        sl = slice(h * HEAD_DIM, (h + 1) * HEAD_DIM)
        q = q_ref[:, sl].astype(BF16)
        s_cache = _dot_nt(q, kc_ref[0, h].astype(BF16)) * scale + bc_ref[h]
        s_new = _dot_nt(q, kn_ref[:, sl].astype(BF16)) * scale + bn_ref[h]
        o_ref[:, sl] = _softmax_pv([s_cache, s_new], [vc_ref[0, h].astype(BF16), vn_ref[:, sl].astype(BF16)])


def _attn_sample(za, k_cache, v_cache, rel_bias, seq):
    nb, _, past, _ = k_cache.shape
    dist = jnp.arange(seq)[:, None] + past - jnp.arange(past + seq)[None, :]
    bias = rel_bias[:, jnp.clip(dist, -REL_CLIP, REL_CLIP) + REL_CLIP].astype(F32)
    blk = (seq, WIDTH)
    rows = lambda col: (lambda b: (b, col))
    cache = pl.BlockSpec((1, HEADS, past, HEAD_DIM), lambda b: (b, 0, 0, 0))
    return pl.pallas_call(
        _attn_sample_kernel,
        grid=(nb,),
        in_specs=[pl.BlockSpec(blk, rows(0)), pl.BlockSpec(blk, rows(1)), pl.BlockSpec(blk, rows(2)), cache, cache,
                  _resident((HEADS, seq, past)), _resident((HEADS, seq, seq))],
        out_specs=pl.BlockSpec(blk, rows(0)),
        out_shape=jax.ShapeDtypeStruct((nb * seq, WIDTH), F32),
        compiler_params=_params(dimension_semantics=("parallel",)),
        name="attn_sample",
    )(za, za, za, k_cache, v_cache, bias[:, :, :past], bias[:, :, past:])


def _softplus(x):
    return jnp.maximum(x, 0.0) + jnp.log(1.0 + jnp.exp(-jnp.abs(x)))


def _rwkv_prep_kernel(zb_ref, prev_ref, shift_ref, mu_ref, wd0_ref, wdu_ref, a0_ref, wau_ref, wgu_ref, kk_ref, ka_ref,
                      rk_ref, gsum_ref, r_o, w_o, k_o, v_o, a_o, b_o, g_o, bonus_o, *, seq):
    i = pl.program_id(0)
    zb = zb_ref[...]
    tm = zb.shape[0]
    row = lax.broadcasted_iota(jnp.int32, zb.shape, 0)
    prev = jnp.where(row == 0, prev_ref[7:8, :], pltpu.roll(zb, 1, 0))
    ngrp = tm // SHIFT_GROUP
    shift = jnp.broadcast_to(shift_ref[...][:, None, :], (ngrp, SHIFT_GROUP, B_COLS)).reshape(tm, B_COLS)
    prev = jnp.where(lax.rem(i * tm + row, seq) == 0, shift, prev)
    zm = zb + (prev - zb) * mu_ref[...]
    r = zm[:, 0:WIDTH]
    k = zm[:, WIDTH:2 * WIDTH]
    v = zm[:, 2 * WIDTH:3 * WIDTH]
    lora_in = zm[:, 3 * WIDTH:3 * WIDTH + DECAY_RANK + AAA_RANK]
    gate_in = zm[:, 3 * WIDTH + DECAY_RANK + AAA_RANK:]
    w_log = -_softplus(-(wd0_ref[...] + _dot(jnp.tanh(lora_in), wdu_ref[...], precision=HIGHEST))) - 0.5
    decay = jnp.exp(-jnp.exp(w_log))
    a = _sigmoid(a0_ref[...] + _dot(lora_in, wau_ref[...], precision=HIGHEST))
    g = _dot(_sigmoid(gate_in), wgu_ref[...], precision=HIGHEST)
    kk = k * kk_ref[...]
    kk = kk / jnp.maximum(jnp.sqrt(_dot(kk * kk, gsum_ref[...], precision=HIGHEST)), 1e-12)
    kmod = k * (1.0 + (a - 1.0) * ka_ref[...])
    for ref, val in ((r_o, r), (w_o, decay), (k_o, kmod), (v_o, v), (a_o, -kk), (b_o, kk * a)):
        for pair in range(PAIRS):
            ref[pair] = val[:, pair * PAIR_W:(pair + 1) * PAIR_W].reshape(ref.shape[1:])
    g_o[...] = g
    bonus_o[...] = _dot(r * kmod * rk_ref[...], gsum_ref[...], precision=HIGHEST) * v


def _head_sum_matrix(scale):
    h = jnp.arange(WIDTH) // HEAD_DIM
    return jnp.where(h[:, None] == h[None, :], scale, 0.0).astype(F32)


def _scan_layout(batch, seq):
    steps = min(seq, SCAN_TILE)
    shape = (PAIRS, seq // steps, batch, steps, PAIR_W)
    if seq >= ROW_TILE:
        per_seq = seq // ROW_TILE
        block = (PAIRS, ROW_TILE // steps, 1, steps, PAIR_W)
        index = lambda i: (0, i % per_seq, i // per_seq, 0, 0)
    else:
        block = (PAIRS, 1, ROW_TILE // seq, steps, PAIR_W)
        index = lambda i: (0, 0, i, 0, 0)
    return steps, shape, pl.BlockSpec(block, index)


def _rwkv_prep(zb, shift0, w, batch, seq):
    m = zb.shape[0]
    tm = ROW_TILE
    row = lambda i: (i, 0)
    groups_per_seq = seq // SHIFT_GROUP
    shift_rows = jnp.zeros((batch, groups_per_seq, B_COLS), F32).at[:, 0].set(shift0.reshape(batch, B_COLS))
    _, scan_shape, scan_spec = _scan_layout(batch, seq)
    flat = jax.ShapeDtypeStruct((m, WIDTH), F32)
    return pl.pallas_call(
        functools.partial(_rwkv_prep_kernel, seq=seq),
        grid=(m // tm,),
        in_specs=[pl.BlockSpec((tm, B_COLS), row),
                  pl.BlockSpec((8, B_COLS), lambda i: (jnp.maximum(i * (tm // 8) - 1, 0), 0)),
                  pl.BlockSpec((tm // SHIFT_GROUP, B_COLS), row),
                  _resident((1, B_COLS)), _resident((1, WIDTH)), _resident((DECAY_RANK + AAA_RANK, WIDTH)),
                  _resident((1, WIDTH)), _resident((DECAY_RANK + AAA_RANK, WIDTH)), _resident((GATE_RANK, WIDTH)),
                  _resident((1, WIDTH)), _resident((1, WIDTH)), _resident((1, WIDTH)), _resident((WIDTH, WIDTH))],
        out_specs=[scan_spec] * 6 + [pl.BlockSpec((tm, WIDTH), row)] * 2,
        out_shape=[jax.ShapeDtypeStruct(scan_shape, F32)] * 6 + [flat] * 2,
        compiler_params=_params(dimension_semantics=("parallel",)),
        name="rwkv_prep",
    )(zb, zb, shift_rows.reshape(m // SHIFT_GROUP, B_COLS), w["shift_mu"], w["w_decay0"], w["w_decay_up"], w["a0"],
      w["w_a_up"], w["w_g_up"], w["k_k"], w["k_a"], w["r_k"], w["head_sum"])


def _scan_kernel(r_ref, w_ref, k_ref, v_ref, a_ref, b_ref, s0_ref, y_ref, st_ref, s_scr, *, steps):
    tb = pl.program_id(1)
    chains = [(b, p) for b in range(s_scr.shape[0]) for p in range(PAIRS)]

    @pl.when(tb == 0)
    def _():
        s_scr[...] = s0_ref[...]

    lane = lax.broadcasted_iota(jnp.int32, (HEAD_DIM, PAIR_W), 1)
    sub = lax.broadcasted_iota(jnp.int32, (HEAD_DIM, PAIR_W), 0)
    lo = lane < HEAD_DIM
    diag = (lane & (HEAD_DIM - 1)) == sub
    sub8 = lax.broadcasted_iota(jnp.int32, (8, PAIR_W), 0)

    def head_sums(x):
        s_lo = jnp.sum(jnp.where(lo, x, 0.0), axis=1, keepdims=True)
        s_hi = jnp.sum(jnp.where(lo, 0.0, x), axis=1, keepdims=True)
        return jnp.where(lo, s_lo, s_hi)

    def row_of(ref, chain, t):
        b, p = chain
        tile = ref[p, 0, b, pl.ds(pl.multiple_of((t // 8) * 8, 8), 8), :]
        return jnp.sum(jnp.where(sub8 == (t % 8), tile, 0.0), axis=0, keepdims=True)

    same_head = ((lax.broadcasted_iota(jnp.int32, (PAIR_W, PAIR_W), 0) < HEAD_DIM)
                 == (lax.broadcasted_iota(jnp.int32, (PAIR_W, PAIR_W), 1) < HEAD_DIM))
    head_ones = jnp.where(same_head, 1.0, 0.0).astype(BF16)

    def emit_y(chain, t, s):
        b, p = chain
        prod = s * row_of(r_ref, chain, t)
        hi = prod.astype(BF16)
        lo = (prod - hi.astype(F32)).astype(BF16)
        sums = _dot(jnp.concatenate([hi, lo], axis=0), head_ones)
        y_col = sums[:HEAD_DIM] + sums[HEAD_DIM:]
        y_ref[p, 0, b, pl.ds(t, 1), :] = jnp.sum(jnp.where(diag, y_col, 0.0), axis=0, keepdims=True)

    def update(chain, t, s):
        sa = head_sums(s * row_of(a_ref, chain, t))
        v_col = head_sums(jnp.where(diag, row_of(v_ref, chain, t), 0.0))
        return s * row_of(w_ref, chain, t) + sa * row_of(b_ref, chain, t) + v_col * row_of(k_ref, chain, t)

    for b, p in chains:
        s_scr[b, p] = update((b, p), 0, s_scr[b, p])

    def step(t):
        for b, p in chains:
            s = s_scr[b, p]
            emit_y((b, p), t - 1, s)
            s_scr[b, p] = update((b, p), t, s)

    def trip(i, carry):
        for u in range(SCAN_UNROLL):
            step(1 + SCAN_UNROLL * i + u)
        return carry

    trips = (steps - 1) // SCAN_UNROLL
    lax.fori_loop(0, trips, trip, 0)
    for t in range(1 + trips * SCAN_UNROLL, steps):
        step(t)
    for b, p in chains:
        emit_y((b, p), steps - 1, s_scr[b, p])

    @pl.when(tb == pl.num_programs(1) - 1)
    def _():
        st_ref[...] = s_scr[...]


def _pair_state(s):
    b = s.shape[0]
    return s.reshape(b, PAIRS, 2, HEAD_DIM, HEAD_DIM).transpose(0, 1, 3, 2, 4).reshape(b, PAIRS, HEAD_DIM, PAIR_W)


def _unpair_state(s):
    b = s.shape[0]
    return s.reshape(b, PAIRS, HEAD_DIM, 2, HEAD_DIM).transpose(0, 1, 3, 2, 4).reshape(b, HEADS, HEAD_DIM, HEAD_DIM)


def _wkv_scan(r, w, k, v, a, b, s0, batch, seq):
    steps, shape, _ = _scan_layout(batch, seq)
    bg = SCAN_BATCH
    blk = pl.BlockSpec((PAIRS, 1, bg, steps, PAIR_W), lambda bi, ti: (0, ti, bi, 0, 0))
    state = pl.BlockSpec((bg, PAIRS, HEAD_DIM, PAIR_W), lambda bi, ti: (bi, 0, 0, 0))
    y, st = pl.pallas_call(
        functools.partial(_scan_kernel, steps=steps),
        grid=(batch // bg, seq // steps),
        in_specs=[blk] * 6 + [state],
        out_specs=[blk, state],
        out_shape=[jax.ShapeDtypeStruct(shape, F32), jax.ShapeDtypeStruct((batch, PAIRS, HEAD_DIM, PAIR_W), F32)],
        scratch_shapes=[pltpu.VMEM((bg, PAIRS, HEAD_DIM, PAIR_W), F32)],
        compiler_params=_params(dimension_semantics=("parallel", "arbitrary")),
        name="wkv_scan",
    )(r, w, k, v, a, b, _pair_state(s0.astype(F32)))
    return y, _unpair_state(st)


def _postmix_kernel(x_ref, oa_ref, y_ref, bonus_ref, g_ref, zg_ref, lng_ref, lnb_ref, gmean_ref, wpa_ref, wpb_ref,
                    wout_ref, n2g_ref, wch_ref, wcl_ref, x1_ref, hn_ref, sc_ref):
    tm = x_ref.shape[0]
    y = jnp.concatenate([y_ref[pair].reshape(tm, PAIR_W) for pair in range(PAIRS)], axis=1)
    mu = _dot(y, gmean_ref[...], precision=HIGHEST)
    d = y - mu
    var = _dot(d * d, gmean_ref[...], precision=HIGHEST)
    yn = d * lax.rsqrt(var + LNX_EPS) * lng_ref[...] + lnb_ref[...]
    ob = (yn + bonus_ref[...]) * g_ref[...]
    pa = _dot(oa_ref[...].astype(BF16), wpa_ref[...])
    pb = _dot(ob.astype(BF16), wpb_ref[...])
    merged = _sigmoid(zg_ref[:, :D_MODEL]) * pa + _sigmoid(zg_ref[:, D_MODEL:]) * pb
    x1 = x_ref[...] + _dot(merged.astype(BF16), wout_ref[...])
    x1_ref[...] = x1
    hn = x1 * lax.rsqrt(jnp.mean(x1 * x1, axis=-1, keepdims=True) + NORM_EPS) * n2g_ref[...]
    hn_ref[...] = hn
    hh = hn.astype(BF16)
    hl = (hn - hh.astype(F32)).astype(BF16)
    sc_ref[...] = _dot_nt(wch_ref[...], hh) + (_dot_nt(wcl_ref[...], hh) + _dot_nt(wch_ref[...], hl))


def _post_mix(x, o_a, y, bonus, g, zg, w, batch, seq):
    m = x.shape[0]
    tm = ROW_TILE
    row = lambda i: (i, 0)
    nsc = w["score_hi"].shape[0]
    wide = pl.BlockSpec((tm, D_MODEL), row)
    half = pl.BlockSpec((tm, WIDTH), row)
    _, _, scan_spec = _scan_layout(batch, seq)
    return pl.pallas_call(
        _postmix_kernel,
        grid=(m // tm,),
        in_specs=[wide, half, scan_spec, half, half, pl.BlockSpec((tm, G_COLS), row),
                  _resident((1, WIDTH)), _resident((1, WIDTH)), _resident((WIDTH, WIDTH)),
                  _resident((WIDTH, D_MODEL)), _resident((WIDTH, D_MODEL)), _resident((D_MODEL, D_MODEL)),
                  _resident((1, D_MODEL)), _resident((nsc, D_MODEL)), _resident((nsc, D_MODEL))],
        out_specs=[wide, wide, pl.BlockSpec((nsc, tm), lambda i: (0, i))],
        out_shape=[jax.ShapeDtypeStruct((m, D_MODEL), F32), jax.ShapeDtypeStruct((m, D_MODEL), F32),
                   jax.ShapeDtypeStruct((nsc, m), F32)],
        compiler_params=_params(dimension_semantics=("parallel",)),
        name="post_mix",
    )(x, o_a, y, bonus, g, zg, w["lnx_g"], w["lnx_b"], w["head_mean"], w["w_proj_a"], w["w_proj_b"], w["w_out"],
      w["norm2_g"], w["score_hi"], w["score_lo"])


def _score_weight_kernel(wq_ref, sk_ref, o_ref):
    for c in range(2):
        wq = wq_ref[:, c * P_HALF:(c + 1) * P_HALF]
        o_ref[c * N_KEYS:(c + 1) * N_KEYS, :] = _dot_nt(sk_ref[0, c], wq, precision=HIGHEST)


def _score_weights(w_query, sub_keys):
    return pl.pallas_call(
        _score_weight_kernel,
        grid=(P_HEADS,),
        in_specs=[pl.BlockSpec((D_MODEL, 2 * P_HALF), lambda h: (0, h)),
                  pl.BlockSpec((1, 2, N_KEYS, P_HALF), lambda h: (h, 0, 0, 0))],
        out_specs=pl.BlockSpec((2 * N_KEYS, D_MODEL), lambda h: (h, 0)),
        out_shape=jax.ShapeDtypeStruct((P_HEADS * 2 * N_KEYS, D_MODEL), F32),
        compiler_params=_params(dimension_semantics=("parallel",)),
        name="score_weights",
    )(w_query, sub_keys)


N_CAND = P_TOPK + 7 * 8 + 8


def _extract_topk(vals, ids, n, payload=None):
    top_v, top_i = [], []
    big = jnp.int32(2 ** 30)
    for _ in range(n):
        m = jnp.max(vals, axis=0, keepdims=True)
        pick = jnp.min(jnp.where(vals == m, ids, big), axis=0, keepdims=True)
        sel = ids == pick
        top_v.append(m)
        top_i.append(pick if payload is None else jnp.max(jnp.where(sel, payload, -1), axis=0, keepdims=True))
        vals = jnp.where(sel, -jnp.inf, vals)
    return top_v, top_i


def _topk_kernel(sc_ref, eidx_ref, gate_ref):
    lanes = sc_ref.shape[1]
    key_id = lax.broadcasted_iota(jnp.int32, (N_KEYS, lanes), 0)
    sub16 = lax.broadcasted_iota(jnp.int32, (P_TOPK, lanes), 0)
    sub8 = lax.broadcasted_iota(jnp.int32, (8, lanes), 0)
    cand_id = jnp.concatenate([sub16] + [a * P_TOPK + sub8 for a in range(1, 8)] + [(8 + sub8) * P_TOPK], axis=0)

    def head(h, carry):
        base = pl.multiple_of(h * 2 * N_KEYS, 2 * N_KEYS)
        v0, i0 = _extract_topk(sc_ref[pl.ds(base, N_KEYS), :], key_id, P_TOPK)
        v1, i1 = _extract_topk(sc_ref[pl.ds(base + N_KEYS, N_KEYS), :], key_id, P_TOPK)
        v1_16, i1_16 = jnp.concatenate(v1, axis=0), jnp.concatenate(i1, axis=0)
        v1_8, i1_8 = jnp.concatenate(v1[:8], axis=0), jnp.concatenate(i1[:8], axis=0)
        cand = jnp.concatenate([v0[0] + v1_16] + [v0[a] + v1_8 for a in range(1, 8)]
                               + [jnp.concatenate(v0[8:], axis=0) + v1[0]], axis=0)
        cidx = jnp.concatenate([i0[0] * N_KEYS + i1_16] + [i0[a] * N_KEYS + i1_8 for a in range(1, 8)]
                               + [jnp.concatenate(i0[8:], axis=0) * N_KEYS + i1[0]], axis=0) * TILE_WORDS
        fv, fe = _extract_topk(cand, cand_id, P_TOPK, payload=cidx)
        fv = jnp.concatenate(fv, axis=0)
        e = jnp.exp(fv - fv[0:1])
        out = pl.multiple_of(h * P_TOPK, P_TOPK)
        gate_ref[pl.ds(out, P_TOPK), :] = e / jnp.sum(e, axis=0, keepdims=True)
        eidx_ref[pl.ds(out, P_TOPK), :] = jnp.concatenate(fe, axis=0)
        return carry

    lax.fori_loop(0, P_HEADS, head, 0)


def _topk(scores):
    nsc, m = scores.shape
    tl = 2 * TOK_TILE
    col = lambda i: (0, i)
    return pl.pallas_call(
        _topk_kernel,
        grid=(m // tl,),
        in_specs=[pl.BlockSpec((nsc, tl), col)],
        out_specs=[pl.BlockSpec((P_HEADS * P_TOPK, tl), col)] * 2,
        out_shape=[jax.ShapeDtypeStruct((P_HEADS * P_TOPK, m), jnp.int32),
                   jax.ShapeDtypeStruct((P_HEADS * P_TOPK, m), F32)],
        compiler_params=_params(dimension_semantics=("parallel",)),
        name="topk",
    )(scores)


U_ROW_GROUP = 64


def _tile_table(t):
    pairs = t.astype(BF16).reshape(t.shape[0], TILE_WORDS, 2, ROW_SHAPE[1]).transpose(0, 1, 3, 2)
    return lax.bitcast_convert_type(pairs, jnp.uint32).reshape(t.shape[0] * TILE_WORDS, ROW_SHAPE[1])


def _expert_row(tbl_ref, word_row):
    words = tbl_ref[pl.ds(pl.multiple_of(word_row, TILE_WORDS), TILE_WORDS), :]
    return pltpu.bitcast(words, BF16).astype(F32)


def _sublane_fold(x, y, step, mask):
    return jnp.where(mask, x + pltpu.roll(x, 8 - step, 0), y + pltpu.roll(y, step, 0))


def _peer_u_kernel(idx_ref, hn_ref, gate_ref, tbl_ref, coef_ref, part_scr):
    toks = hn_ref.shape[0]
    sub = lax.broadcasted_iota(jnp.int32, ROW_SHAPE, 0)
    m4, m2, m1 = sub < 4, (sub & 3) < 2, (sub & 1) == 0

    def token(t, carry):
        x = hn_ref[t]

        def group(g, carry):
            first = t * N_SEL + g * U_ROW_GROUP
            for h in range(U_ROW_GROUP // 8):
                prod = [_expert_row(tbl_ref, idx_ref[first + h * 8 + j]) * x for j in range(8)]
                z = [_sublane_fold(prod[a], prod[a + 4], 4, m4) for a in (0, 2, 1, 3)]
                w0 = _sublane_fold(z[0], z[1], 2, m2)
                w1 = _sublane_fold(z[2], z[3], 2, m2)
                rows = pl.ds(pl.multiple_of(first + h * 8, 8), 8)
                part_scr[rows, :] = _sublane_fold(w0, w1, 1, m1)
            return carry

        return lax.fori_loop(0, N_SEL // U_ROW_GROUP, group, carry)

    lax.fori_loop(0, toks, token, 0)

    lane_tok = lax.broadcasted_iota(jnp.int32, (N_SEL, toks), 1)
    tok_unroll = 8

    def reduce_tokens(i, hid):
        for u in range(tok_unroll):
            t = i * tok_unroll + u
            rows = pl.ds(pl.multiple_of(t * N_SEL, N_SEL), N_SEL)
            hid = jnp.where(lane_tok == t, jnp.sum(part_scr[rows, :], axis=1, keepdims=True), hid)
        return hid

    hid = lax.fori_loop(0, toks // tok_unroll, reduce_tokens, jnp.zeros((N_SEL, toks), F32))
    coef_ref[...] = gate_ref[...] * (0.5 * hid * (1.0 + lax.erf(hid * (2.0 ** -0.5))))


def _peer_u(eidx_flat, hn3, gate, table):
    m = hn3.shape[0]
    tt = TOK_TILE
    return pl.pallas_call(
        _peer_u_kernel,
        grid=(m // tt,),
        in_specs=[pl.BlockSpec((tt * N_SEL,), lambda i: (i,), memory_space=pltpu.SMEM),
                  pl.BlockSpec((tt, *ROW_SHAPE), lambda i: (i, 0, 0)),
                  pl.BlockSpec((N_SEL, tt), lambda i: (0, i)),
                  _resident(table.shape)],
        out_specs=pl.BlockSpec((N_SEL, tt), lambda i: (0, i)),
        out_shape=jax.ShapeDtypeStruct((N_SEL, m), F32),
        scratch_shapes=[pltpu.VMEM((tt * N_SEL, ROW_SHAPE[1]), F32)],
        compiler_params=_params(dimension_semantics=("parallel",)),
        name="peer_u",
    )(eidx_flat, hn3, gate, table)


def _peer_v_kernel(idx_ref, coef_ref, x_ref, tbl_ref, g_ref, y_ref, splat_scr):
    toks = x_ref.shape[0]
    n_acc = 4
    lane_tok = lax.broadcasted_iota(jnp.int32, (N_SEL, toks), 1)

    def splat(t):
        col = jnp.sum(jnp.where(lane_tok == t, coef_ref[...], 0.0), axis=1, keepdims=True)
        return jnp.broadcast_to(col, (N_SEL, ROW_SHAPE[1]))

    splat_scr[0] = splat(0)

    def token(t, carry):
        slot = t % 2
        nxt = splat(jnp.minimum(t + 1, toks - 1))
        acc = [jnp.zeros(ROW_SHAPE, F32) for _ in range(n_acc)]
        for h in range(N_SEL // 8):
            coef = splat_scr[slot, h * 8:(h + 1) * 8, :]
            for j in range(8):
                row = _expert_row(tbl_ref, idx_ref[t * N_SEL + h * 8 + j])
                acc[j % n_acc] = acc[j % n_acc] + coef[j:j + 1, :] * row
        y_ref[t] = x_ref[t] + ((acc[0] + acc[1]) + (acc[2] + acc[3]))
        splat_scr[1 - slot] = nxt
        return carry

    lax.fori_loop(0, toks, token, 0)
    x2 = y_ref[...]
    ms = jnp.sum(jnp.sum(x2 * x2, axis=2, keepdims=True), axis=1, keepdims=True) * (1.0 / D_MODEL)
    y_ref[...] = x2 * lax.rsqrt(ms + NORM_EPS) * g_ref[...]


def _peer_v(eidx_flat, coef, x3, table, normf_g):
    m = x3.shape[0]
    tt = TOK_TILE
    tok = pl.BlockSpec((tt, *ROW_SHAPE), lambda i: (i, 0, 0))
    return pl.pallas_call(
        _peer_v_kernel,
        grid=(m // tt,),
        in_specs=[pl.BlockSpec((tt * N_SEL,), lambda i: (i,), memory_space=pltpu.SMEM),
                  pl.BlockSpec((N_SEL, tt), lambda i: (0, i)), tok, _resident(table.shape), _resident(ROW_SHAPE)],
        out_specs=tok,
        out_shape=jax.ShapeDtypeStruct(x3.shape, F32),
        scratch_shapes=[pltpu.VMEM((2, N_SEL, ROW_SHAPE[1]), F32)],
        compiler_params=_params(dimension_semantics=("parallel",)),
        name="peer_v",
    )(eidx_flat, coef, x3, table, normf_g)


def _stream_step(x, k_cache, v_cache, wkv0, shift0, w):
    batch, seq, _ = x.shape
    m = batch * seq
    assert m % (2 * TOK_TILE) == 0 and batch % SCAN_BATCH == 0 and seq % SHIFT_GROUP == 0
    assert (seq % ROW_TILE == 0 and seq % SCAN_TILE == 0) or ROW_TILE % seq == 0
    za, zb, zg = _in_proj(x.reshape(m, D_MODEL), w["norm1_g"], w["w_in"])
    if k_cache is None:
        assert seq % ATT_TILE == 0
        o_a = _attn_prompt(za, w["rel_bias"], batch, seq)
    else:
        o_a = _attn_sample(za, k_cache, v_cache, w["rel_bias"], seq)
    r, dec, k, v, a, b, g, bonus = _rwkv_prep(zb, shift0, w, batch, seq)
    y, wkv = _wkv_scan(r, dec, k, v, a, b, wkv0, batch, seq)
    x1, hn, scores = _post_mix(x.reshape(m, D_MODEL), o_a, y, bonus, g, zg, w, batch, seq)
    eidx, gate = _topk(scores)
    eidx_flat = eidx.T.reshape(-1)
    coef = _peer_u(eidx_flat, hn.reshape(m, *ROW_SHAPE), gate, w["expert_u"])
    out = _peer_v(eidx_flat, coef, x1.reshape(m, *ROW_SHAPE), w["expert_v"], w["normf_g"])

    keep = min(BAND_CHUNKS * CHUNK, seq) if k_cache is None else seq
    zk = za.reshape(batch, seq, A_COLS)[:, seq - keep:]
    heads = lambda t: t.reshape(batch, keep, HEADS, HEAD_DIM).transpose(0, 2, 1, 3)
    return (out.reshape(batch, seq, D_MODEL), heads(zk[..., WIDTH:2 * WIDTH]), heads(zk[..., 2 * WIDTH:]), wkv,
            zb.reshape(batch, seq, B_COLS)[:, -1:])


def kernel(x_prompt, x_sample, cache_attn_k, cache_attn_v, state_wkv, state_shift, norm1_g, w_in, rel_bias, shift_mu,
           w_decay0, w_decay_up, a0, w_a_up, w_g_up, k_k, k_a, r_k, lnx_g, lnx_b, w_proj_a, w_proj_b, w_out, norm2_g,
           w_query, sub_keys, expert_u, expert_v, normf_g):
    assert norm1_g.shape[0] == 1, "single-layer step"
    zeros = jnp.zeros((DECAY_RANK, WIDTH), F32)
    score_w = _score_weights(w_query[0], sub_keys[0])
    score_hi = score_w.astype(BF16)
    w = dict(
        norm1_g=norm1_g[0], w_in=w_in[0].astype(BF16), rel_bias=rel_bias[0],
        shift_mu=_row_vec(shift_mu[0]), w_decay0=_row_vec(w_decay0[0]), a0=_row_vec(a0[0]),
        w_decay_up=jnp.concatenate([w_decay_up[0], zeros], axis=0), w_a_up=jnp.concatenate([zeros, w_a_up[0]], axis=0),
        w_g_up=w_g_up[0], k_k=_row_vec(k_k[0]), k_a=_row_vec(k_a[0]), r_k=_row_vec(r_k[0]),
        head_sum=_head_sum_matrix(1.0), head_mean=_head_sum_matrix(1.0 / HEAD_DIM),
        lnx_g=_row_vec(lnx_g[0]), lnx_b=_row_vec(lnx_b[0]),
        w_proj_a=w_proj_a[0].astype(BF16), w_proj_b=w_proj_b[0].astype(BF16), w_out=w_out[0].astype(BF16),
        norm2_g=_row_vec(norm2_g[0]), score_hi=score_hi, score_lo=(score_w - score_hi.astype(F32)).astype(BF16),
        expert_u=_tile_table(expert_u[0]), expert_v=_tile_table(expert_v[0]),
        normf_g=normf_g.reshape(ROW_SHAPE).astype(F32))

    batch = x_prompt.shape[0]
    yp, kp, vp, wp, sp = _stream_step(x_prompt, None, None, jnp.zeros((batch, HEADS, HEAD_DIM, HEAD_DIM), F32),
                                      jnp.zeros((batch, 1, B_COLS), F32), w)
    ys, ks, vs, ws, ss = _stream_step(x_sample, cache_attn_k[0], cache_attn_v[0], state_wkv[0], state_shift[0], w)
    return (yp, ys, kp[None], vp[None], wp[None], sp[None], ks[None], vs[None], ws[None], ss[None])
```

```python
import functools

import jax
import jax.numpy as jnp
from jax import lax
from jax.experimental import pallas as pl
from jax.experimental.pallas import tpu as pltpu

F32 = jnp.float32
BF16 = jnp.bfloat16
HIGHEST = lax.Precision.HIGHEST

D_MODEL = 1024
CHUNK = 64
BAND_CHUNKS = 8
HEADS = 8
HEAD_DIM = 64
WIDTH = HEADS * HEAD_DIM
REL_CLIP = 128
DECAY_RANK = 64
AAA_RANK = 64
GATE_RANK = 128
LNX_EPS = 64e-5
NORM_EPS = 1e-6
P_HEADS = 8
N_KEYS = 128
P_HALF = 64
P_TOPK = 16
A_COLS = 3 * WIDTH
B_COLS = 3 * WIDTH + DECAY_RANK + AAA_RANK + GATE_RANK
G_COLS = 2 * D_MODEL
NEG = -1e30

VMEM_LIMIT = 56 * 1024 * 1024
ROW_TILE = 256
ATT_TILE = BAND_CHUNKS * CHUNK
BAND_KEYS = (BAND_CHUNKS + 1) * CHUNK
SCAN_TILE = 128
SCAN_BATCH = 2
SCAN_UNROLL = 4
TOK_TILE = 128
SHIFT_GROUP = 32
PAIRS = HEADS // 2
PAIR_W = 2 * HEAD_DIM
N_SEL = P_HEADS * P_TOPK
ROW_SHAPE = (8, D_MODEL // 8)
TILE_WORDS = ROW_SHAPE[0] // 2


def _params(**kw):
    return pltpu.CompilerParams(vmem_limit_bytes=VMEM_LIMIT, **kw)


def _resident(shape):
    nd = len(shape)
    return pl.BlockSpec(shape, lambda *_: (0,) * nd, pipeline_mode=pl.Buffered(1))


def _sigmoid(x):
    return 1.0 / (1.0 + jnp.exp(-x))


def _dot(a, b, **kw):
    return jnp.dot(a, b, preferred_element_type=F32, **kw)


def _dot_nt(a, b, **kw):
    return lax.dot_general(a, b, (((1,), (1,)), ((), ())), preferred_element_type=F32, **kw)


def _row_vec(a):
    return a.reshape(1, -1).astype(F32)


def _inproj_kernel(x_ref, g_ref, w_ref, za_ref, zb_ref, zg_ref):
    x = x_ref[...]
    y = x * lax.rsqrt(jnp.mean(x * x, axis=-1, keepdims=True) + NORM_EPS) * g_ref[...]
    yb = y.astype(BF16)
    za_ref[...] = _dot(yb, w_ref[:, :A_COLS])
    zb_ref[...] = _dot(yb, w_ref[:, A_COLS:A_COLS + B_COLS])
    zg_ref[...] = _dot(yb, w_ref[:, A_COLS + B_COLS:])


def _in_proj(x, norm_g, w_in_bf16):
    m = x.shape[0]
    in_cols = w_in_bf16.shape[1]
    row = lambda i: (i, 0)
    return pl.pallas_call(
        _inproj_kernel,
        grid=(m // ROW_TILE,),
        in_specs=[pl.BlockSpec((ROW_TILE, D_MODEL), row), _resident((1, D_MODEL)), _resident((D_MODEL, in_cols))],
        out_specs=[pl.BlockSpec((ROW_TILE, A_COLS), row), pl.BlockSpec((ROW_TILE, B_COLS), row),
                   pl.BlockSpec((ROW_TILE, G_COLS), row)],
        out_shape=[jax.ShapeDtypeStruct((m, A_COLS), F32), jax.ShapeDtypeStruct((m, B_COLS), F32),
                   jax.ShapeDtypeStruct((m, G_COLS), F32)],
        compiler_params=_params(dimension_semantics=("parallel",)),
        name="in_proj",
    )(x, _row_vec(norm_g), w_in_bf16)


def _softmax_pv(scores, values):
    m = scores[0].max(axis=-1, keepdims=True)
    for s in scores[1:]:
        m = jnp.maximum(m, s.max(axis=-1, keepdims=True))
    acc, den = None, None
    for s, v in zip(scores, values):
        p = jnp.exp(s - m)
        d = p.sum(axis=-1, keepdims=True)
        o = _dot(p.astype(BF16), v)
        acc = o if acc is None else acc + o
        den = d if den is None else den + d
    return acc / den


def _attn_prompt_kernel(q_ref, kp_ref, kc_ref, vp_ref, vc_ref, bias_ref, o_ref):
    first = pl.program_id(1) == 0
    scale = HEAD_DIM ** -0.5
    for h in range(HEADS):
        sl = slice(h * HEAD_DIM, (h + 1) * HEAD_DIM)
        q = q_ref[:, sl].astype(BF16)
        s_prev = _dot_nt(q, kp_ref[:, sl].astype(BF16)) * scale + bias_ref[h, :, :ATT_TILE]
        s_cur = _dot_nt(q, kc_ref[:, sl].astype(BF16)) * scale + bias_ref[h, :, ATT_TILE:]
        s_prev = jnp.where(first, NEG, s_prev)
        o_ref[:, sl] = _softmax_pv([s_prev, s_cur], [vp_ref[:, sl].astype(BF16), vc_ref[:, sl].astype(BF16)])


def _prompt_bias_table(rel_bias):
    dist = jnp.arange(CHUNK)[:, None] + BAND_CHUNKS * CHUNK - jnp.arange(BAND_KEYS)[None, :]
    window = rel_bias[:, jnp.clip(dist, -REL_CLIP, REL_CLIP) + REL_CLIP].astype(F32)
    rows = [jnp.pad(window, ((0, 0), (0, 0), (c * CHUNK, 2 * ATT_TILE - BAND_KEYS - c * CHUNK)), constant_values=NEG)
            for c in range(BAND_CHUNKS)]
    return jnp.concatenate(rows, axis=1)


def _attn_prompt(za, rel_bias, batch, seq):
    nb = seq // ATT_TILE
    blk = (ATT_TILE, WIDTH)
    cur = lambda col: (lambda b, i: (b * nb + i, col))
    prev = lambda col: (lambda b, i: (b * nb + jnp.maximum(i - 1, 0), col))
    return pl.pallas_call(
        _attn_prompt_kernel,
        grid=(batch, nb),
        in_specs=[pl.BlockSpec(blk, cur(0)), pl.BlockSpec(blk, prev(1)), pl.BlockSpec(blk, cur(1)),
                  pl.BlockSpec(blk, prev(2)), pl.BlockSpec(blk, cur(2)),
                  _resident((HEADS, ATT_TILE, 2 * ATT_TILE))],
        out_specs=pl.BlockSpec(blk, cur(0)),
        out_shape=jax.ShapeDtypeStruct((batch * seq, WIDTH), F32),
        compiler_params=_params(dimension_semantics=("parallel", "arbitrary")),
        name="attn_prompt",
    )(za, za, za, za, za, _prompt_bias_table(rel_bias))


def _attn_sample_kernel(q_ref, kn_ref, vn_ref, kc_ref, vc_ref, bc_ref, bn_ref, o_ref):
    scale = HEAD_DIM ** -0.5
    for h in range(HEADS):
        sl = slice(h * HEAD_DIM, (h + 1) * HEAD_DIM)
        q = q_ref[:, sl].astype(BF16)
        s_cache = _dot_nt(q, kc_ref[0, h].astype(BF16)) * scale + bc_ref[h]
        s_new = _dot_nt(q, kn_ref[:, sl].astype(BF16)) * scale + bn_ref[h]
        o_ref[:, sl] = _softmax_pv([s_cache, s_new], [vc_ref[0, h].astype(BF16), vn_ref[:, sl].astype(BF16)])


def _attn_sample(za, k_cache, v_cache, rel_bias, seq):
    nb, _, past, _ = k_cache.shape
    dist = jnp.arange(seq)[:, None] + past - jnp.arange(past + seq)[None, :]
    bias = rel_bias[:, jnp.clip(dist, -REL_CLIP, REL_CLIP) + REL_CLIP].astype(F32)
    blk = (seq, WIDTH)
    rows = lambda col: (lambda b: (b, col))
    cache = pl.BlockSpec((1, HEADS, past, HEAD_DIM), lambda b: (b, 0, 0, 0))
    return pl.pallas_call(
        _attn_sample_kernel,
        grid=(nb,),
        in_specs=[pl.BlockSpec(blk, rows(0)), pl.BlockSpec(blk, rows(1)), pl.BlockSpec(blk, rows(2)), cache, cache,
                  _resident((HEADS, seq, past)), _resident((HEADS, seq, seq))],
        out_specs=pl.BlockSpec(blk, rows(0)),
        out_shape=jax.ShapeDtypeStruct((nb * seq, WIDTH), F32),
        compiler_params=_params(dimension_semantics=("parallel",)),
        name="attn_sample",
    )(za, za, za, k_cache, v_cache, bias[:, :, :past], bias[:, :, past:])


def _softplus(x):
    return jnp.maximum(x, 0.0) + jnp.log(1.0 + jnp.exp(-jnp.abs(x)))


def _rwkv_prep_kernel(zb_ref, prev_ref, shift_ref, mu_ref, wd0_ref, wdu_ref, a0_ref, wau_ref, wgu_ref, kk_ref, ka_ref,
                      rk_ref, gsum_ref, r_o, w_o, k_o, v_o, a_o, b_o, g_o, bonus_o, *, seq):
    i = pl.program_id(0)
    zb = zb_ref[...]
    tm = zb.shape[0]
    row = lax.broadcasted_iota(jnp.int32, zb.shape, 0)
    prev = jnp.where(row == 0, prev_ref[7:8, :], pltpu.roll(zb, 1, 0))
    ngrp = tm // SHIFT_GROUP
    shift = jnp.broadcast_to(shift_ref[...][:, None, :], (ngrp, SHIFT_GROUP, B_COLS)).reshape(tm, B_COLS)
    prev = jnp.where(lax.rem(i * tm + row, seq) == 0, shift, prev)
    zm = zb + (prev - zb) * mu_ref[...]
    r = zm[:, 0:WIDTH]
    k = zm[:, WIDTH:2 * WIDTH]
    v = zm[:, 2 * WIDTH:3 * WIDTH]
    lora_in = zm[:, 3 * WIDTH:3 * WIDTH + DECAY_RANK + AAA_RANK]
    gate_in = zm[:, 3 * WIDTH + DECAY_RANK + AAA_RANK:]
    w_log = -_softplus(-(wd0_ref[...] + _dot(jnp.tanh(lora_in), wdu_ref[...], precision=HIGHEST))) - 0.5
    decay = jnp.exp(-jnp.exp(w_log))
    a = _sigmoid(a0_ref[...] + _dot(lora_in, wau_ref[...], precision=HIGHEST))
    g = _dot(_sigmoid(gate_in), wgu_ref[...], precision=HIGHEST)
    kk = k * kk_ref[...]
    kk = kk / jnp.maximum(jnp.sqrt(_dot(kk * kk, gsum_ref[...], precision=HIGHEST)), 1e-12)
    kmod = k * (1.0 + (a - 1.0) * ka_ref[...])
    for ref, val in ((r_o, r), (w_o, decay), (k_o, kmod), (v_o, v), (a_o, -kk), (b_o, kk * a)):
        for pair in range(PAIRS):
            ref[pair] = val[:, pair * PAIR_W:(pair + 1) * PAIR_W].reshape(ref.shape[1:])
    g_o[...] = g
    bonus_o[...] = _dot(r * kmod * rk_ref[...], gsum_ref[...], precision=HIGHEST) * v


def _head_sum_matrix(scale):
    h = jnp.arange(WIDTH) // HEAD_DIM
    return jnp.where(h[:, None] == h[None, :], scale, 0.0).astype(F32)


def _scan_layout(batch, seq):
    steps = min(seq, SCAN_TILE)
    shape = (PAIRS, seq // steps, batch, steps, PAIR_W)
    if seq >= ROW_TILE:
        per_seq = seq // ROW_TILE
        block = (PAIRS, ROW_TILE // steps, 1, steps, PAIR_W)
        index = lambda i: (0, i % per_seq, i // per_seq, 0, 0)
    else:
        block = (PAIRS, 1, ROW_TILE // seq, steps, PAIR_W)
        index = lambda i: (0, 0, i, 0, 0)
    return steps, shape, pl.BlockSpec(block, index)


def _rwkv_prep(zb, shift0, w, batch, seq):
    m = zb.shape[0]
    tm = ROW_TILE
    row = lambda i: (i, 0)
    groups_per_seq = seq // SHIFT_GROUP
    shift_rows = jnp.zeros((batch, groups_per_seq, B_COLS), F32).at[:, 0].set(shift0.reshape(batch, B_COLS))
    _, scan_shape, scan_spec = _scan_layout(batch, seq)
    flat = jax.ShapeDtypeStruct((m, WIDTH), F32)
    return pl.pallas_call(
        functools.partial(_rwkv_prep_kernel, seq=seq),
        grid=(m // tm,),
        in_specs=[pl.BlockSpec((tm, B_COLS), row),
                  pl.BlockSpec((8, B_COLS), lambda i: (jnp.maximum(i * (tm // 8) - 1, 0), 0)),
                  pl.BlockSpec((tm // SHIFT_GROUP, B_COLS), row),
                  _resident((1, B_COLS)), _resident((1, WIDTH)), _resident((DECAY_RANK + AAA_RANK, WIDTH)),
                  _resident((1, WIDTH)), _resident((DECAY_RANK + AAA_RANK, WIDTH)), _resident((GATE_RANK, WIDTH)),
                  _resident((1, WIDTH)), _resident((1, WIDTH)), _resident((1, WIDTH)), _resident((WIDTH, WIDTH))],
        out_specs=[scan_spec] * 6 + [pl.BlockSpec((tm, WIDTH), row)] * 2,
        out_shape=[jax.ShapeDtypeStruct(scan_shape, F32)] * 6 + [flat] * 2,
        compiler_params=_params(dimension_semantics=("parallel",)),
        name="rwkv_prep",
    )(zb, zb, shift_rows.reshape(m // SHIFT_GROUP, B_COLS), w["shift_mu"], w["w_decay0"], w["w_decay_up"], w["a0"],
      w["w_a_up"], w["w_g_up"], w["k_k"], w["k_a"], w["r_k"], w["head_sum"])


def _scan_kernel(r_ref, w_ref, k_ref, v_ref, a_ref, b_ref, s0_ref, y_ref, st_ref, s_scr, *, steps):
    tb = pl.program_id(1)
    chains = [(b, p) for b in range(s_scr.shape[0]) for p in range(PAIRS)]

    @pl.when(tb == 0)
    def _():
        s_scr[...] = s0_ref[...]

    lane = lax.broadcasted_iota(jnp.int32, (HEAD_DIM, PAIR_W), 1)
    sub = lax.broadcasted_iota(jnp.int32, (HEAD_DIM, PAIR_W), 0)
    lo = lane < HEAD_DIM
    diag = (lane & (HEAD_DIM - 1)) == sub
    sub8 = lax.broadcasted_iota(jnp.int32, (8, PAIR_W), 0)

    def head_sums(x):
        s_lo = jnp.sum(jnp.where(lo, x, 0.0), axis=1, keepdims=True)
        s_hi = jnp.sum(jnp.where(lo, 0.0, x), axis=1, keepdims=True)
        return jnp.where(lo, s_lo, s_hi)

    def row_of(ref, chain, t):
        b, p = chain
        tile8, j = t
        return ref[p, 0, b, pl.ds(pl.multiple_of(tile8 * 8, 8), 8), :][j:j + 1, :]

    same_head = ((lax.broadcasted_iota(jnp.int32, (PAIR_W, PAIR_W), 0) < HEAD_DIM)
                 == (lax.broadcasted_iota(jnp.int32, (PAIR_W, PAIR_W), 1) < HEAD_DIM))
    head_ones = jnp.where(same_head, 1.0, 0.0).astype(BF16)

    def head_sums_mxu(x):
        hi = x.astype(BF16)
        lo = (x - hi.astype(F32)).astype(BF16)
        sums = _dot(jnp.concatenate([hi, lo], axis=0), head_ones)
        return sums[:HEAD_DIM] + sums[HEAD_DIM:]

    def emit_y(chain, t, s):
        b, p = chain
        y_col = head_sums_mxu(s * row_of(r_ref, chain, t))
        y_ref[p, 0, b, pl.ds(t[0] * 8 + t[1], 1), :] = jnp.sum(jnp.where(diag, y_col, 0.0), axis=0, keepdims=True)

    def update(chain, t, s):
        sums = head_sums_mxu if chain[1] else head_sums
        sa = sums(s * row_of(a_ref, chain, t))
        v_col = head_sums(jnp.where(diag, row_of(v_ref, chain, t), 0.0))
        return s * row_of(w_ref, chain, t) + sa * row_of(b_ref, chain, t) + v_col * row_of(k_ref, chain, t)

    def step(tile8, j):
        prev = (tile8, j - 1) if j else (tile8 - 1, 7)
        for b, p in chains:
            s = s_scr[b, p]
            if not (isinstance(tile8, int) and tile8 == 0 and j == 0):
                emit_y((b, p), prev, s)
            s_scr[b, p] = update((b, p), (tile8, j), s)

    def trip(tile8, carry):
        for j in range(8):
            step(tile8, j)
        return carry

    trip(0, 0)
    lax.fori_loop(1, steps // 8, trip, 0)
    for b, p in chains:
        emit_y((b, p), (steps // 8 - 1, 7), s_scr[b, p])

    @pl.when(tb == pl.num_programs(1) - 1)
    def _():
        st_ref[...] = s_scr[...]


def _pair_state(s):
    b = s.shape[0]
    return s.reshape(b, PAIRS, 2, HEAD_DIM, HEAD_DIM).transpose(0, 1, 3, 2, 4).reshape(b, PAIRS, HEAD_DIM, PAIR_W)


def _unpair_state(s):
    b = s.shape[0]
    return s.reshape(b, PAIRS, HEAD_DIM, 2, HEAD_DIM).transpose(0, 1, 3, 2, 4).reshape(b, HEADS, HEAD_DIM, HEAD_DIM)


def _wkv_scan(r, w, k, v, a, b, s0, batch, seq):
    steps, shape, _ = _scan_layout(batch, seq)
    bg = SCAN_BATCH
    blk = pl.BlockSpec((PAIRS, 1, bg, steps, PAIR_W), lambda bi, ti: (0, ti, bi, 0, 0))
    state = pl.BlockSpec((bg, PAIRS, HEAD_DIM, PAIR_W), lambda bi, ti: (bi, 0, 0, 0))
    y, st = pl.pallas_call(
        functools.partial(_scan_kernel, steps=steps),
        grid=(batch // bg, seq // steps),
        in_specs=[blk] * 6 + [state],
        out_specs=[blk, state],
        out_shape=[jax.ShapeDtypeStruct(shape, F32), jax.ShapeDtypeStruct((batch, PAIRS, HEAD_DIM, PAIR_W), F32)],
        scratch_shapes=[pltpu.VMEM((bg, PAIRS, HEAD_DIM, PAIR_W), F32)],
        compiler_params=_params(dimension_semantics=("parallel", "arbitrary")),
        name="wkv_scan",
    )(r, w, k, v, a, b, _pair_state(s0.astype(F32)))
    return y, _unpair_state(st)


def _postmix_kernel(x_ref, oa_ref, y_ref, bonus_ref, g_ref, zg_ref, lng_ref, lnb_ref, gmean_ref, wpa_ref, wpb_ref,
                    wout_ref, n2g_ref, wch_ref, wcl_ref, x1_ref, hn_ref, sc_ref):
    tm = x_ref.shape[0]
    y = jnp.concatenate([y_ref[pair].reshape(tm, PAIR_W) for pair in range(PAIRS)], axis=1)
    mu = _dot(y, gmean_ref[...], precision=HIGHEST)
    d = y - mu
    var = _dot(d * d, gmean_ref[...], precision=HIGHEST)
    yn = d * lax.rsqrt(var + LNX_EPS) * lng_ref[...] + lnb_ref[...]
    ob = (yn + bonus_ref[...]) * g_ref[...]
    pa = _dot(oa_ref[...].astype(BF16), wpa_ref[...])
    pb = _dot(ob.astype(BF16), wpb_ref[...])
    merged = _sigmoid(zg_ref[:, :D_MODEL]) * pa + _sigmoid(zg_ref[:, D_MODEL:]) * pb
    x1 = x_ref[...] + _dot(merged.astype(BF16), wout_ref[...])
    x1_ref[...] = x1
    hn = x1 * lax.rsqrt(jnp.mean(x1 * x1, axis=-1, keepdims=True) + NORM_EPS) * n2g_ref[...]
    hn_ref[...] = hn
    hh = hn.astype(BF16)
    hl = (hn - hh.astype(F32)).astype(BF16)
    sc_ref[...] = _dot_nt(wch_ref[...], hh) + (_dot_nt(wcl_ref[...], hh) + _dot_nt(wch_ref[...], hl))


def _post_mix(x, o_a, y, bonus, g, zg, w, batch, seq):
    m = x.shape[0]
    tm = ROW_TILE
    row = lambda i: (i, 0)
    nsc = w["score_hi"].shape[0]
    wide = pl.BlockSpec((tm, D_MODEL), row)
    half = pl.BlockSpec((tm, WIDTH), row)
    _, _, scan_spec = _scan_layout(batch, seq)
    return pl.pallas_call(
        _postmix_kernel,
        grid=(m // tm,),
        in_specs=[wide, half, scan_spec, half, half, pl.BlockSpec((tm, G_COLS), row),
                  _resident((1, WIDTH)), _resident((1, WIDTH)), _resident((WIDTH, WIDTH)),
                  _resident((WIDTH, D_MODEL)), _resident((WIDTH, D_MODEL)), _resident((D_MODEL, D_MODEL)),
                  _resident((1, D_MODEL)), _resident((nsc, D_MODEL)), _resident((nsc, D_MODEL))],
        out_specs=[wide, wide, pl.BlockSpec((nsc, tm), lambda i: (0, i))],
        out_shape=[jax.ShapeDtypeStruct((m, D_MODEL), F32), jax.ShapeDtypeStruct((m, D_MODEL), F32),
                   jax.ShapeDtypeStruct((nsc, m), F32)],
        compiler_params=_params(dimension_semantics=("parallel",)),
        name="post_mix",
    )(x, o_a, y, bonus, g, zg, w["lnx_g"], w["lnx_b"], w["head_mean"], w["w_proj_a"], w["w_proj_b"], w["w_out"],
      w["norm2_g"], w["score_hi"], w["score_lo"])


def _score_weight_kernel(wq_ref, sk_ref, o_ref):
    for c in range(2):
        wq = wq_ref[:, c * P_HALF:(c + 1) * P_HALF]
        o_ref[c * N_KEYS:(c + 1) * N_KEYS, :] = _dot_nt(sk_ref[0, c], wq, precision=HIGHEST)


def _score_weights(w_query, sub_keys):
    return pl.pallas_call(
        _score_weight_kernel,
        grid=(P_HEADS,),
        in_specs=[pl.BlockSpec((D_MODEL, 2 * P_HALF), lambda h: (0, h)),
                  pl.BlockSpec((1, 2, N_KEYS, P_HALF), lambda h: (h, 0, 0, 0))],
        out_specs=pl.BlockSpec((2 * N_KEYS, D_MODEL), lambda h: (h, 0)),
        out_shape=jax.ShapeDtypeStruct((P_HEADS * 2 * N_KEYS, D_MODEL), F32),
        compiler_params=_params(dimension_semantics=("parallel",)),
        name="score_weights",
    )(w_query, sub_keys)


N_CAND = P_TOPK + 7 * 8 + 8


def _extract_topk(vals, ids, n, payload=None):
    top_v, top_i = [], []
    big = jnp.int32(2 ** 30)
    for _ in range(n):
        m = jnp.max(vals, axis=0, keepdims=True)
        pick = jnp.min(jnp.where(vals == m, ids, big), axis=0, keepdims=True)
        sel = ids == pick
        top_v.append(m)
        top_i.append(pick if payload is None else jnp.max(jnp.where(sel, payload, -1), axis=0, keepdims=True))
        vals = jnp.where(sel, -jnp.inf, vals)
    return top_v, top_i


def _topk_kernel(sc_ref, eidx_ref, gate_ref):
    lanes = sc_ref.shape[1]
    key_id = lax.broadcasted_iota(jnp.int32, (N_KEYS, lanes), 0)
    sub16 = lax.broadcasted_iota(jnp.int32, (P_TOPK, lanes), 0)
    sub8 = lax.broadcasted_iota(jnp.int32, (8, lanes), 0)
    cand_id = jnp.concatenate([sub16] + [a * P_TOPK + sub8 for a in range(1, 8)] + [(8 + sub8) * P_TOPK], axis=0)

    def head(h, carry):
        base = pl.multiple_of(h * 2 * N_KEYS, 2 * N_KEYS)
        v0, i0 = _extract_topk(sc_ref[pl.ds(base, N_KEYS), :], key_id, P_TOPK)
        v1, i1 = _extract_topk(sc_ref[pl.ds(base + N_KEYS, N_KEYS), :], key_id, P_TOPK)
        v1_16, i1_16 = jnp.concatenate(v1, axis=0), jnp.concatenate(i1, axis=0)
        v1_8, i1_8 = jnp.concatenate(v1[:8], axis=0), jnp.concatenate(i1[:8], axis=0)
        cand = jnp.concatenate([v0[0] + v1_16] + [v0[a] + v1_8 for a in range(1, 8)]
                               + [jnp.concatenate(v0[8:], axis=0) + v1[0]], axis=0)
        cidx = jnp.concatenate([i0[0] * N_KEYS + i1_16] + [i0[a] * N_KEYS + i1_8 for a in range(1, 8)]
                               + [jnp.concatenate(i0[8:], axis=0) * N_KEYS + i1[0]], axis=0) * TILE_WORDS
        fv, fe = _extract_topk(cand, cand_id, P_TOPK, payload=cidx)
        fv = jnp.concatenate(fv, axis=0)
        e = jnp.exp(fv - fv[0:1])
        out = pl.multiple_of(h * P_TOPK, P_TOPK)
        gate_ref[pl.ds(out, P_TOPK), :] = e / jnp.sum(e, axis=0, keepdims=True)
        eidx_ref[pl.ds(out, P_TOPK), :] = jnp.concatenate(fe, axis=0)
        return carry

    lax.fori_loop(0, P_HEADS, head, 0)


def _topk(scores):
    nsc, m = scores.shape
    tl = 2 * TOK_TILE
    col = lambda i: (0, i)
    return pl.pallas_call(
        _topk_kernel,
        grid=(m // tl,),
        in_specs=[pl.BlockSpec((nsc, tl), col)],
        out_specs=[pl.BlockSpec((P_HEADS * P_TOPK, tl), col)] * 2,
        out_shape=[jax.ShapeDtypeStruct((P_HEADS * P_TOPK, m), jnp.int32),
                   jax.ShapeDtypeStruct((P_HEADS * P_TOPK, m), F32)],
        compiler_params=_params(dimension_semantics=("parallel",)),
        name="topk",
    )(scores)


def _tile_table(t):
    pairs = t.astype(BF16).reshape(t.shape[0], TILE_WORDS, 2, ROW_SHAPE[1]).transpose(0, 1, 3, 2)
    return lax.bitcast_convert_type(pairs, jnp.uint32).reshape(t.shape[0] * TILE_WORDS, ROW_SHAPE[1])


def _expert_row(tbl_ref, word_row):
    words = tbl_ref[pl.ds(pl.multiple_of(word_row, TILE_WORDS), TILE_WORDS), :]
    return pltpu.bitcast(words, BF16).astype(F32)


def _sublane_fold(x, y, step, mask):
    if step == 4:
        return jnp.where(mask, x, y) + pltpu.roll(jnp.where(mask, y, x), 4, 0)
    return jnp.where(mask, x + pltpu.roll(x, 8 - step, 0), y + pltpu.roll(y, step, 0))


def _peer_u_kernel(idx_ref, hn_ref, gate_ref, tbl_ref, coef_ref, part_scr, hid_scr):
    toks = hn_ref.shape[0]
    sub = lax.broadcasted_iota(jnp.int32, ROW_SHAPE, 0)
    m4, m2, m1 = sub < 4, (sub & 3) < 2, (sub & 1) == 0
    lane_tok = lax.broadcasted_iota(jnp.int32, (N_SEL, toks), 1)
    part_scr[...] = jnp.zeros_like(part_scr)
    hid_scr[...] = jnp.zeros_like(hid_scr)

    def finish(t, slot):
        hid_scr[...] = jnp.where(lane_tok == t, jnp.sum(part_scr[slot], axis=1, keepdims=True), hid_scr[...])

    def token(t, carry):
        slot = t % 2
        x = hn_ref[t]
        finish(t - 1, 1 - slot)
        for h in range(N_SEL // 8):
            prod = [_expert_row(tbl_ref, idx_ref[t * N_SEL + h * 8 + j]) * x for j in range(8)]
            z = [_sublane_fold(prod[a], prod[a + 4], 4, m4) for a in (0, 2, 1, 3)]
            w0 = _sublane_fold(z[0], z[1], 2, m2)
            w1 = _sublane_fold(z[2], z[3], 2, m2)
            part_scr[slot, h * 8:(h + 1) * 8, :] = _sublane_fold(w0, w1, 1, m1)
        return carry

    lax.fori_loop(0, toks, token, 0)
    finish(toks - 1, (toks - 1) % 2)
    hid = hid_scr[...]
    coef_ref[...] = gate_ref[...] * (0.5 * hid * (1.0 + lax.erf(hid * (2.0 ** -0.5))))


def _peer_u(eidx_flat, hn3, gate, table):
    m = hn3.shape[0]
    tt = TOK_TILE
    return pl.pallas_call(
        _peer_u_kernel,
        grid=(m // tt,),
        in_specs=[pl.BlockSpec((tt * N_SEL,), lambda i: (i,), memory_space=pltpu.SMEM),
                  pl.BlockSpec((tt, *ROW_SHAPE), lambda i: (i, 0, 0)),
                  pl.BlockSpec((N_SEL, tt), lambda i: (0, i)),
                  _resident(table.shape)],
        out_specs=pl.BlockSpec((N_SEL, tt), lambda i: (0, i)),
        out_shape=jax.ShapeDtypeStruct((N_SEL, m), F32),
        scratch_shapes=[pltpu.VMEM((2, N_SEL, ROW_SHAPE[1]), F32), pltpu.VMEM((N_SEL, tt), F32)],
        compiler_params=_params(dimension_semantics=("parallel",)),
        name="peer_u",
    )(eidx_flat, hn3, gate, table)


def _peer_v_kernel(idx_ref, coef_ref, x_ref, tbl_ref, g_ref, y_ref, splat_scr):
    toks = x_ref.shape[0]
    n_acc = 4
    lane_tok = lax.broadcasted_iota(jnp.int32, (N_SEL, toks), 1)

    def splat(t):
        col = jnp.sum(jnp.where(lane_tok == t, coef_ref[...], 0.0), axis=1, keepdims=True)
        return jnp.broadcast_to(col, (N_SEL, ROW_SHAPE[1]))

    splat_scr[0] = splat(0)

    def token(t, carry):
        slot = t % 2
        nxt = splat(jnp.minimum(t + 1, toks - 1))
        acc = [jnp.zeros(ROW_SHAPE, F32) for _ in range(n_acc)]
        for h in range(N_SEL // 8):
            coef = splat_scr[slot, h * 8:(h + 1) * 8, :]
            for j in range(8):
                row = _expert_row(tbl_ref, idx_ref[t * N_SEL + h * 8 + j])
                acc[j % n_acc] = acc[j % n_acc] + coef[j:j + 1, :] * row
        y_ref[t] = x_ref[t] + ((acc[0] + acc[1]) + (acc[2] + acc[3]))
        splat_scr[1 - slot] = nxt
        return carry

    lax.fori_loop(0, toks, token, 0)
    x2 = y_ref[...]
    ms = jnp.sum(jnp.sum(x2 * x2, axis=2, keepdims=True), axis=1, keepdims=True) * (1.0 / D_MODEL)
    y_ref[...] = x2 * lax.rsqrt(ms + NORM_EPS) * g_ref[...]


def _peer_v(eidx_flat, coef, x3, table, normf_g):
    m = x3.shape[0]
    tt = TOK_TILE
    tok = pl.BlockSpec((tt, *ROW_SHAPE), lambda i: (i, 0, 0))
    return pl.pallas_call(
        _peer_v_kernel,
        grid=(m // tt,),
        in_specs=[pl.BlockSpec((tt * N_SEL,), lambda i: (i,), memory_space=pltpu.SMEM),
                  pl.BlockSpec((N_SEL, tt), lambda i: (0, i)), tok, _resident(table.shape), _resident(ROW_SHAPE)],
        out_specs=tok,
        out_shape=jax.ShapeDtypeStruct(x3.shape, F32),
        scratch_shapes=[pltpu.VMEM((2, N_SEL, ROW_SHAPE[1]), F32)],
        compiler_params=_params(dimension_semantics=("parallel",)),
        name="peer_v",
    )(eidx_flat, coef, x3, table, normf_g)


def _stream_step(x, k_cache, v_cache, wkv0, shift0, w):
    batch, seq, _ = x.shape
    m = batch * seq
    assert m % (2 * TOK_TILE) == 0 and batch % SCAN_BATCH == 0 and seq % SHIFT_GROUP == 0
    assert (seq % ROW_TILE == 0 and seq % SCAN_TILE == 0) or ROW_TILE % seq == 0
    za, zb, zg = _in_proj(x.reshape(m, D_MODEL), w["norm1_g"], w["w_in"])
    if k_cache is None:
        assert seq % ATT_TILE == 0
        o_a = _attn_prompt(za, w["rel_bias"], batch, seq)
    else:
        o_a = _attn_sample(za, k_cache, v_cache, w["rel_bias"], seq)
    r, dec, k, v, a, b, g, bonus = _rwkv_prep(zb, shift0, w, batch, seq)
    y, wkv = _wkv_scan(r, dec, k, v, a, b, wkv0, batch, seq)
    x1, hn, scores = _post_mix(x.reshape(m, D_MODEL), o_a, y, bonus, g, zg, w, batch, seq)
    eidx, gate = _topk(scores)
    eidx_flat = eidx.T.reshape(-1)
    coef = _peer_u(eidx_flat, hn.reshape(m, *ROW_SHAPE), gate, w["expert_u"])
    out = _peer_v(eidx_flat, coef, x1.reshape(m, *ROW_SHAPE), w["expert_v"], w["normf_g"])

    keep = min(BAND_CHUNKS * CHUNK, seq) if k_cache is None else seq
    zk = za.reshape(batch, seq, A_COLS)[:, seq - keep:]
    heads = lambda t: t.reshape(batch, keep, HEADS, HEAD_DIM).transpose(0, 2, 1, 3)
    return (out.reshape(batch, seq, D_MODEL), heads(zk[..., WIDTH:2 * WIDTH]), heads(zk[..., 2 * WIDTH:]), wkv,
            zb.reshape(batch, seq, B_COLS)[:, -1:])


def kernel(x_prompt, x_sample, cache_attn_k, cache_attn_v, state_wkv, state_shift, norm1_g, w_in, rel_bias, shift_mu,
           w_decay0, w_decay_up, a0, w_a_up, w_g_up, k_k, k_a, r_k, lnx_g, lnx_b, w_proj_a, w_proj_b, w_out, norm2_g,
           w_query, sub_keys, expert_u, expert_v, normf_g):
    assert norm1_g.shape[0] == 1, "single-layer step"
    zeros = jnp.zeros((DECAY_RANK, WIDTH), F32)
    score_w = _score_weights(w_query[0], sub_keys[0])
    score_hi = score_w.astype(BF16)
    w = dict(
        norm1_g=norm1_g[0], w_in=w_in[0].astype(BF16), rel_bias=rel_bias[0],
        shift_mu=_row_vec(shift_mu[0]), w_decay0=_row_vec(w_decay0[0]), a0=_row_vec(a0[0]),
        w_decay_up=jnp.concatenate([w_decay_up[0], zeros], axis=0), w_a_up=jnp.concatenate([zeros, w_a_up[0]], axis=0),
        w_g_up=w_g_up[0], k_k=_row_vec(k_k[0]), k_a=_row_vec(k_a[0]), r_k=_row_vec(r_k[0]),
        head_sum=_head_sum_matrix(1.0), head_mean=_head_sum_matrix(1.0 / HEAD_DIM),
        lnx_g=_row_vec(lnx_g[0]), lnx_b=_row_vec(lnx_b[0]),
        w_proj_a=w_proj_a[0].astype(BF16), w_proj_b=w_proj_b[0].astype(BF16), w_out=w_out[0].astype(BF16),
        norm2_g=_row_vec(norm2_g[0]), score_hi=score_hi, score_lo=(score_w - score_hi.astype(F32)).astype(BF16),
        expert_u=_tile_table(expert_u[0]), expert_v=_tile_table(expert_v[0]),
        normf_g=normf_g.reshape(ROW_SHAPE).astype(F32))

    batch = x_prompt.shape[0]
    yp, kp, vp, wp, sp = _stream_step(x_prompt, None, None, jnp.zeros((batch, HEADS, HEAD_DIM, HEAD_DIM), F32),
                                      jnp.zeros((batch, 1, B_COLS), F32), w)
    ys, ks, vs, ws, ss = _stream_step(x_sample, cache_attn_k[0], cache_attn_v[0], state_wkv[0], state_shift[0], w)
    return (yp, ys, kp[None], vp[None], wp[None], sp[None], ks[None], vs[None], ws[None], ss[None])
```

```python
import functools

import jax
import jax.numpy as jnp
from jax import lax
from jax.experimental import pallas as pl
from jax.experimental.pallas import tpu as pltpu

F32 = jnp.float32
BF16 = jnp.bfloat16
HIGHEST = lax.Precision.HIGHEST

D_MODEL = 1024
CHUNK = 64
BAND_CHUNKS = 8
HEADS = 8
HEAD_DIM = 64
WIDTH = HEADS * HEAD_DIM
REL_CLIP = 128
DECAY_RANK = 64
AAA_RANK = 64
GATE_RANK = 128
LNX_EPS = 64e-5
NORM_EPS = 1e-6
P_HEADS = 8
N_KEYS = 128
P_HALF = 64
P_TOPK = 16
A_COLS = 3 * WIDTH
B_COLS = 3 * WIDTH + DECAY_RANK + AAA_RANK + GATE_RANK
G_COLS = 2 * D_MODEL
NEG = -1e30

VMEM_LIMIT = 56 * 1024 * 1024
ROW_TILE = 256
ATT_TILE = BAND_CHUNKS * CHUNK
BAND_KEYS = (BAND_CHUNKS + 1) * CHUNK
SCAN_TILE = 128
SCAN_BATCH = 2
SCAN_UNROLL = 4
TOK_TILE = 128
SHIFT_GROUP = 32
PAIRS = HEADS // 2
PAIR_W = 2 * HEAD_DIM
N_SEL = P_HEADS * P_TOPK
ROW_SHAPE = (8, D_MODEL // 8)
TILE_WORDS = ROW_SHAPE[0] // 2


def _params(**kw):
    return pltpu.CompilerParams(vmem_limit_bytes=VMEM_LIMIT, **kw)


def _resident(shape):
    nd = len(shape)
    return pl.BlockSpec(shape, lambda *_: (0,) * nd, pipeline_mode=pl.Buffered(1))


def _sigmoid(x):
    return 1.0 / (1.0 + jnp.exp(-x))


def _dot(a, b, **kw):
    return jnp.dot(a, b, preferred_element_type=F32, **kw)


def _dot_nt(a, b, **kw):
    return lax.dot_general(a, b, (((1,), (1,)), ((), ())), preferred_element_type=F32, **kw)


def _row_vec(a):
    return a.reshape(1, -1).astype(F32)


def _inproj_kernel(x_ref, g_ref, w_ref, za_ref, zb_ref, zg_ref):
    x = x_ref[...]
    y = x * lax.rsqrt(jnp.mean(x * x, axis=-1, keepdims=True) + NORM_EPS) * g_ref[...]
    yb = y.astype(BF16)
    za_ref[...] = _dot(yb, w_ref[:, :A_COLS])
    zb_ref[...] = _dot(yb, w_ref[:, A_COLS:A_COLS + B_COLS])
    zg_ref[...] = _dot(yb, w_ref[:, A_COLS + B_COLS:])


def _in_proj(x, norm_g, w_in_bf16):
    m = x.shape[0]
    in_cols = w_in_bf16.shape[1]
    row = lambda i: (i, 0)
    return pl.pallas_call(
        _inproj_kernel,
        grid=(m // ROW_TILE,),
        in_specs=[pl.BlockSpec((ROW_TILE, D_MODEL), row), _resident((1, D_MODEL)), _resident((D_MODEL, in_cols))],
        out_specs=[pl.BlockSpec((ROW_TILE, A_COLS), row), pl.BlockSpec((ROW_TILE, B_COLS), row),
                   pl.BlockSpec((ROW_TILE, G_COLS), row)],
        out_shape=[jax.ShapeDtypeStruct((m, A_COLS), F32), jax.ShapeDtypeStruct((m, B_COLS), F32),
                   jax.ShapeDtypeStruct((m, G_COLS), F32)],
        compiler_params=_params(dimension_semantics=("parallel",)),
        name="in_proj",
    )(x, _row_vec(norm_g), w_in_bf16)


def _softmax_pv(scores, values):
    m = scores[0].max(axis=-1, keepdims=True)
    for s in scores[1:]:
        m = jnp.maximum(m, s.max(axis=-1, keepdims=True))
    acc, den = None, None
    for s, v in zip(scores, values):
        p = jnp.exp(s - m)
        d = p.sum(axis=-1, keepdims=True)
        o = _dot(p.astype(BF16), v)
        acc = o if acc is None else acc + o
        den = d if den is None else den + d
    return acc / den


def _attn_prompt_kernel(q_ref, kp_ref, kc_ref, vp_ref, vc_ref, bias_ref, o_ref):
    first = pl.program_id(1) == 0
    scale = HEAD_DIM ** -0.5
    for h in range(HEADS):
        sl = slice(h * HEAD_DIM, (h + 1) * HEAD_DIM)
        q = q_ref[:, sl].astype(BF16)
        s_prev = _dot_nt(q, kp_ref[:, sl].astype(BF16)) * scale + bias_ref[h, :, :ATT_TILE]
        s_cur = _dot_nt(q, kc_ref[:, sl].astype(BF16)) * scale + bias_ref[h, :, ATT_TILE:]
        s_prev = jnp.where(first, NEG, s_prev)
        o_ref[:, sl] = _softmax_pv([s_prev, s_cur], [vp_ref[:, sl].astype(BF16), vc_ref[:, sl].astype(BF16)])


def _prompt_bias_table(rel_bias):
    near = BAND_KEYS - (REL_CLIP // CHUNK + 1) * CHUNK
    dist = jnp.arange(CHUNK)[:, None] + BAND_CHUNKS * CHUNK - jnp.arange(near, BAND_KEYS)[None, :]
    varying = rel_bias[:, jnp.clip(dist, -REL_CLIP, REL_CLIP) + REL_CLIP]
    far = jnp.broadcast_to(rel_bias[:, -1][:, None, None], (HEADS, CHUNK, near))
    window = jnp.concatenate([far, varying], axis=2).astype(F32)
    rows = [jnp.pad(window, ((0, 0), (0, 0), (c * CHUNK, 2 * ATT_TILE - BAND_KEYS - c * CHUNK)), constant_values=NEG)
            for c in range(BAND_CHUNKS)]
    return jnp.concatenate(rows, axis=1)


def _attn_prompt(za, rel_bias, batch, seq):
    nb = seq // ATT_TILE
    blk = (ATT_TILE, WIDTH)
    cur = lambda col: (lambda b, i: (b * nb + i, col))
    prev = lambda col: (lambda b, i: (b * nb + jnp.maximum(i - 1, 0), col))
    return pl.pallas_call(
        _attn_prompt_kernel,
        grid=(batch, nb),
        in_specs=[pl.BlockSpec(blk, cur(0)), pl.BlockSpec(blk, prev(1)), pl.BlockSpec(blk, cur(1)),
                  pl.BlockSpec(blk, prev(2)), pl.BlockSpec(blk, cur(2)),
                  _resident((HEADS, ATT_TILE, 2 * ATT_TILE))],
        out_specs=pl.BlockSpec(blk, cur(0)),
        out_shape=jax.ShapeDtypeStruct((batch * seq, WIDTH), F32),
        compiler_params=_params(dimension_semantics=("parallel", "arbitrary")),
        name="attn_prompt",
    )(za, za, za, za, za, _prompt_bias_table(rel_bias))


def _attn_sample_kernel(q_ref, kn_ref, vn_ref, kc_ref, vc_ref, bc_ref, bn_ref, o_ref):
    scale = HEAD_DIM ** -0.5
    for h in range(HEADS):
        sl = slice(h * HEAD_DIM, (h + 1) * HEAD_DIM)
        q = q_ref[:, sl].astype(BF16)
        s_cache = _dot_nt(q, kc_ref[0, h].astype(BF16)) * scale + bc_ref[h]
        s_new = _dot_nt(q, kn_ref[:, sl].astype(BF16)) * scale + bn_ref[h]
        o_ref[:, sl] = _softmax_pv([s_cache, s_new], [vc_ref[0, h].astype(BF16), vn_ref[:, sl].astype(BF16)])


def _attn_sample(za, k_cache, v_cache, rel_bias, seq):
    nb, _, past, _ = k_cache.shape
    dist = jnp.arange(seq)[:, None] + past - jnp.arange(past + seq)[None, :]
    bias = rel_bias[:, jnp.clip(dist, -REL_CLIP, REL_CLIP) + REL_CLIP].astype(F32)
    blk = (seq, WIDTH)
    rows = lambda col: (lambda b: (b, col))
    cache = pl.BlockSpec((1, HEADS, past, HEAD_DIM), lambda b: (b, 0, 0, 0))
    return pl.pallas_call(
        _attn_sample_kernel,
        grid=(nb,),
        in_specs=[pl.BlockSpec(blk, rows(0)), pl.BlockSpec(blk, rows(1)), pl.BlockSpec(blk, rows(2)), cache, cache,
                  _resident((HEADS, seq, past)), _resident((HEADS, seq, seq))],
        out_specs=pl.BlockSpec(blk, rows(0)),
        out_shape=jax.ShapeDtypeStruct((nb * seq, WIDTH), F32),
        compiler_params=_params(dimension_semantics=("parallel",)),
        name="attn_sample",
    )(za, za, za, k_cache, v_cache, bias[:, :, :past], bias[:, :, past:])


def _softplus(x):
    return jnp.maximum(x, 0.0) + jnp.log(1.0 + jnp.exp(-jnp.abs(x)))


def _rwkv_prep_kernel(zb_ref, prev_ref, shift_ref, mu_ref, wd0_ref, wdu_ref, a0_ref, wau_ref, wgu_ref, kk_ref, ka_ref,
                      rk_ref, gsum_ref, r_o, w_o, k_o, v_o, a_o, b_o, g_o, bonus_o, *, seq):
    i = pl.program_id(0)
    zb = zb_ref[...]
    tm = zb.shape[0]
    row = lax.broadcasted_iota(jnp.int32, zb.shape, 0)
    prev = jnp.where(row == 0, prev_ref[7:8, :], pltpu.roll(zb, 1, 0))
    ngrp = tm // SHIFT_GROUP
    shift = jnp.broadcast_to(shift_ref[...][:, None, :], (ngrp, SHIFT_GROUP, B_COLS)).reshape(tm, B_COLS)
    prev = jnp.where(lax.rem(i * tm + row, seq) == 0, shift, prev)
    zm = zb + (prev - zb) * mu_ref[...]
    r = zm[:, 0:WIDTH]
    k = zm[:, WIDTH:2 * WIDTH]
    v = zm[:, 2 * WIDTH:3 * WIDTH]
    lora_in = zm[:, 3 * WIDTH:3 * WIDTH + DECAY_RANK + AAA_RANK]
    gate_in = zm[:, 3 * WIDTH + DECAY_RANK + AAA_RANK:]
    w_log = -_softplus(-(wd0_ref[...] + _dot(jnp.tanh(lora_in), wdu_ref[...], precision=HIGHEST))) - 0.5
    decay = jnp.exp(-jnp.exp(w_log))
    a = _sigmoid(a0_ref[...] + _dot(lora_in, wau_ref[...], precision=HIGHEST))
    g = _dot(_sigmoid(gate_in), wgu_ref[...], precision=HIGHEST)
    kk = k * kk_ref[...]
    kk = kk / jnp.maximum(jnp.sqrt(_dot(kk * kk, gsum_ref[...], precision=HIGHEST)), 1e-12)
    kmod = k * (1.0 + (a - 1.0) * ka_ref[...])
    for ref, val in ((r_o, r), (w_o, decay), (k_o, kmod), (v_o, v), (a_o, -kk), (b_o, kk * a)):
        for pair in range(PAIRS):
            ref[pair] = val[:, pair * PAIR_W:(pair + 1) * PAIR_W].reshape(ref.shape[1:])
    g_o[...] = g
    bonus_o[...] = _dot(r * kmod * rk_ref[...], gsum_ref[...], precision=HIGHEST) * v


def _head_sum_matrix(scale):
    h = jnp.arange(WIDTH) // HEAD_DIM
    return jnp.where(h[:, None] == h[None, :], scale, 0.0).astype(F32)


def _scan_layout(batch, seq):
    steps = min(seq, SCAN_TILE)
    shape = (PAIRS, seq // steps, batch, steps, PAIR_W)
    if seq >= ROW_TILE:
        per_seq = seq // ROW_TILE
        block = (PAIRS, ROW_TILE // steps, 1, steps, PAIR_W)
        index = lambda i: (0, i % per_seq, i // per_seq, 0, 0)
    else:
        block = (PAIRS, 1, ROW_TILE // seq, steps, PAIR_W)
        index = lambda i: (0, 0, i, 0, 0)
    return steps, shape, pl.BlockSpec(block, index)


def _rwkv_prep(zb, shift0, w, batch, seq):
    m = zb.shape[0]
    tm = ROW_TILE
    row = lambda i: (i, 0)
    groups_per_seq = seq // SHIFT_GROUP
    shift_rows = jnp.zeros((batch, groups_per_seq, B_COLS), F32).at[:, 0].set(shift0.reshape(batch, B_COLS))
    _, scan_shape, scan_spec = _scan_layout(batch, seq)
    flat = jax.ShapeDtypeStruct((m, WIDTH), F32)
    return pl.pallas_call(
        functools.partial(_rwkv_prep_kernel, seq=seq),
        grid=(m // tm,),
        in_specs=[pl.BlockSpec((tm, B_COLS), row),
                  pl.BlockSpec((8, B_COLS), lambda i: (jnp.maximum(i * (tm // 8) - 1, 0), 0)),
                  pl.BlockSpec((tm // SHIFT_GROUP, B_COLS), row),
                  _resident((1, B_COLS)), _resident((1, WIDTH)), _resident((DECAY_RANK + AAA_RANK, WIDTH)),
                  _resident((1, WIDTH)), _resident((DECAY_RANK + AAA_RANK, WIDTH)), _resident((GATE_RANK, WIDTH)),
                  _resident((1, WIDTH)), _resident((1, WIDTH)), _resident((1, WIDTH)), _resident((WIDTH, WIDTH))],
        out_specs=[scan_spec] * 6 + [pl.BlockSpec((tm, WIDTH), row)] * 2,
        out_shape=[jax.ShapeDtypeStruct(scan_shape, F32)] * 6 + [flat] * 2,
        compiler_params=_params(dimension_semantics=("parallel",)),
        name="rwkv_prep",
    )(zb, zb, shift_rows.reshape(m // SHIFT_GROUP, B_COLS), w["shift_mu"], w["w_decay0"], w["w_decay_up"], w["a0"],
      w["w_a_up"], w["w_g_up"], w["k_k"], w["k_a"], w["r_k"], w["head_sum"])


def _scan_kernel(r_ref, w_ref, k_ref, v_ref, a_ref, b_ref, s0_ref, y_ref, st_ref, s_scr, *, steps):
    tb = pl.program_id(1)
    chains = [(b, p) for b in range(s_scr.shape[0]) for p in range(PAIRS)]

    @pl.when(tb == 0)
    def _():
        s_scr[...] = s0_ref[...]

    lane = lax.broadcasted_iota(jnp.int32, (HEAD_DIM, PAIR_W), 1)
    sub = lax.broadcasted_iota(jnp.int32, (HEAD_DIM, PAIR_W), 0)
    lo = lane < HEAD_DIM
    diag = (lane & (HEAD_DIM - 1)) == sub
    sub8 = lax.broadcasted_iota(jnp.int32, (8, PAIR_W), 0)

    def head_sums(x):
        s_lo = jnp.sum(jnp.where(lo, x, 0.0), axis=1, keepdims=True)
        s_hi = jnp.sum(jnp.where(lo, 0.0, x), axis=1, keepdims=True)
        return jnp.where(lo, s_lo, s_hi)

    def row_of(ref, chain, t):
        b, p = chain
        tile8, j = t
        return ref[p, 0, b, pl.ds(pl.multiple_of(tile8 * 8, 8), 8), :][j:j + 1, :]

    same_head = ((lax.broadcasted_iota(jnp.int32, (PAIR_W, PAIR_W), 0) < HEAD_DIM)
                 == (lax.broadcasted_iota(jnp.int32, (PAIR_W, PAIR_W), 1) < HEAD_DIM))
    head_ones = jnp.where(same_head, 1.0, 0.0).astype(BF16)

    def head_sums_mxu(x):
        hi = x.astype(BF16)
        lo = (x - hi.astype(F32)).astype(BF16)
        sums = _dot(jnp.concatenate([hi, lo], axis=0), head_ones)
        return sums[:HEAD_DIM] + sums[HEAD_DIM:]

    def emit_y(chain, t, s):
        b, p = chain
        y_col = head_sums_mxu(s * row_of(r_ref, chain, t))
        y_ref[p, 0, b, pl.ds(t[0] * 8 + t[1], 1), :] = jnp.sum(jnp.where(diag, y_col, 0.0), axis=0, keepdims=True)

    def update(chain, t, s):
        sums = head_sums_mxu if chain[1] else head_sums
        sa = sums(s * row_of(a_ref, chain, t))
        v_col = head_sums(jnp.where(diag, row_of(v_ref, chain, t), 0.0))
        return s * row_of(w_ref, chain, t) + sa * row_of(b_ref, chain, t) + v_col * row_of(k_ref, chain, t)

    def step(tile8, j):
        prev = (tile8, j - 1) if j else (tile8 - 1, 7)
        for b, p in chains:
            s = s_scr[b, p]
            if not (isinstance(tile8, int) and tile8 == 0 and j == 0):
                emit_y((b, p), prev, s)
            s_scr[b, p] = update((b, p), (tile8, j), s)

    def trip(tile8, carry):
        for j in range(8):
            step(tile8, j)
        return carry

    trip(0, 0)
    lax.fori_loop(1, steps // 8, trip, 0)
    for b, p in chains:
        emit_y((b, p), (steps // 8 - 1, 7), s_scr[b, p])

    @pl.when(tb == pl.num_programs(1) - 1)
    def _():
        st_ref[...] = s_scr[...]


def _pair_state(s):
    b = s.shape[0]
    return s.reshape(b, PAIRS, 2, HEAD_DIM, HEAD_DIM).transpose(0, 1, 3, 2, 4).reshape(b, PAIRS, HEAD_DIM, PAIR_W)


def _unpair_state(s):
    b = s.shape[0]
    return s.reshape(b, PAIRS, HEAD_DIM, 2, HEAD_DIM).transpose(0, 1, 3, 2, 4).reshape(b, HEADS, HEAD_DIM, HEAD_DIM)


def _wkv_scan(r, w, k, v, a, b, s0, batch, seq):
    steps, shape, _ = _scan_layout(batch, seq)
    bg = SCAN_BATCH
    blk = pl.BlockSpec((PAIRS, 1, bg, steps, PAIR_W), lambda bi, ti: (0, ti, bi, 0, 0))
    state = pl.BlockSpec((bg, PAIRS, HEAD_DIM, PAIR_W), lambda bi, ti: (bi, 0, 0, 0))
    y, st = pl.pallas_call(
        functools.partial(_scan_kernel, steps=steps),
        grid=(batch // bg, seq // steps),
        in_specs=[blk] * 6 + [state],
        out_specs=[blk, state],
        out_shape=[jax.ShapeDtypeStruct(shape, F32), jax.ShapeDtypeStruct((batch, PAIRS, HEAD_DIM, PAIR_W), F32)],
        scratch_shapes=[pltpu.VMEM((bg, PAIRS, HEAD_DIM, PAIR_W), F32)],
        compiler_params=_params(dimension_semantics=("parallel", "arbitrary")),
        name="wkv_scan",
    )(r, w, k, v, a, b, _pair_state(s0.astype(F32)))
    return y, _unpair_state(st)


def _postmix_kernel(x_ref, oa_ref, y_ref, bonus_ref, g_ref, zg_ref, lng_ref, lnb_ref, gmean_ref, wpa_ref, wpb_ref,
                    wout_ref, n2g_ref, wch_ref, wcl_ref, x1_ref, hn_ref, eidx_ref, gate_ref, sc_scr):
    tm = x_ref.shape[0]
    y = jnp.concatenate([y_ref[pair].reshape(tm, PAIR_W) for pair in range(PAIRS)], axis=1)
    mu = _dot(y, gmean_ref[...], precision=HIGHEST)
    d = y - mu
    var = _dot(d * d, gmean_ref[...], precision=HIGHEST)
    yn = d * lax.rsqrt(var + LNX_EPS) * lng_ref[...] + lnb_ref[...]
    ob = (yn + bonus_ref[...]) * g_ref[...]
    pa = _dot(oa_ref[...].astype(BF16), wpa_ref[...])
    pb = _dot(ob.astype(BF16), wpb_ref[...])
    merged = _sigmoid(zg_ref[:, :D_MODEL]) * pa + _sigmoid(zg_ref[:, D_MODEL:]) * pb
    x1 = x_ref[...] + _dot(merged.astype(BF16), wout_ref[...])
    hn = x1 * lax.rsqrt(jnp.mean(x1 * x1, axis=-1, keepdims=True) + NORM_EPS) * n2g_ref[...]
    for s in range(ROW_SHAPE[0]):
        cols = slice(s * ROW_SHAPE[1], (s + 1) * ROW_SHAPE[1])
        x1_ref[:, s, :] = x1[:, cols]
        hn_ref[:, s, :] = hn[:, cols]
    hh = hn.astype(BF16)
    hl = (hn - hh.astype(F32)).astype(BF16)
    sc_scr[...] = _dot_nt(wch_ref[...], hh) + (_dot_nt(wcl_ref[...], hh) + _dot_nt(wch_ref[...], hl))
    _retrieve(sc_scr, eidx_ref, gate_ref)


def _post_mix(x, o_a, y, bonus, g, zg, w, batch, seq):
    m = x.shape[0]
    tm = ROW_TILE
    row = lambda i: (i, 0)
    nsc = w["score_hi"].shape[0]
    wide = pl.BlockSpec((tm, D_MODEL), row)
    half = pl.BlockSpec((tm, WIDTH), row)
    tiles = pl.BlockSpec((tm, *ROW_SHAPE), lambda i: (i, 0, 0))
    picks = pl.BlockSpec((N_SEL, tm), lambda i: (0, i))
    _, _, scan_spec = _scan_layout(batch, seq)
    return pl.pallas_call(
        _postmix_kernel,
        grid=(m // tm,),
        in_specs=[wide, half, scan_spec, half, half, pl.BlockSpec((tm, G_COLS), row),
                  _resident((1, WIDTH)), _resident((1, WIDTH)), _resident((WIDTH, WIDTH)),
                  _resident((WIDTH, D_MODEL)), _resident((WIDTH, D_MODEL)), _resident((D_MODEL, D_MODEL)),
                  _resident((1, D_MODEL)), _resident((nsc, D_MODEL)), _resident((nsc, D_MODEL))],
        out_specs=[tiles, tiles, picks, picks],
        out_shape=[jax.ShapeDtypeStruct((m, *ROW_SHAPE), F32), jax.ShapeDtypeStruct((m, *ROW_SHAPE), F32),
                   jax.ShapeDtypeStruct((N_SEL, m), jnp.int32), jax.ShapeDtypeStruct((N_SEL, m), F32)],
        scratch_shapes=[pltpu.VMEM((nsc, tm), F32)],
        compiler_params=_params(dimension_semantics=("parallel",)),
        name="post_mix",
    )(x, o_a, y, bonus, g, zg, w["lnx_g"], w["lnx_b"], w["head_mean"], w["w_proj_a"], w["w_proj_b"], w["w_out"],
      w["norm2_g"], w["score_hi"], w["score_lo"])


def _score_weight_kernel(wq_ref, sk_ref, o_ref):
    for c in range(2):
        wq = wq_ref[:, c * P_HALF:(c + 1) * P_HALF]
        o_ref[c * N_KEYS:(c + 1) * N_KEYS, :] = _dot_nt(sk_ref[0, c], wq, precision=HIGHEST)


def _score_weights(w_query, sub_keys):
    return pl.pallas_call(
        _score_weight_kernel,
        grid=(P_HEADS,),
        in_specs=[pl.BlockSpec((D_MODEL, 2 * P_HALF), lambda h: (0, h)),
                  pl.BlockSpec((1, 2, N_KEYS, P_HALF), lambda h: (h, 0, 0, 0))],
        out_specs=pl.BlockSpec((2 * N_KEYS, D_MODEL), lambda h: (h, 0)),
        out_shape=jax.ShapeDtypeStruct((P_HEADS * 2 * N_KEYS, D_MODEL), F32),
        compiler_params=_params(dimension_semantics=("parallel",)),
        name="score_weights",
    )(w_query, sub_keys)


N_CAND = P_TOPK + 7 * 8 + 8


def _extract_topk(vals, ids, n, payload=None):
    top_v, top_i = [], []
    big = jnp.int32(2 ** 30)
    for _ in range(n):
        m = jnp.max(vals, axis=0, keepdims=True)
        pick = jnp.min(jnp.where(vals == m, ids, big), axis=0, keepdims=True)
        sel = ids == pick
        top_v.append(m)
        top_i.append(pick if payload is None else jnp.max(jnp.where(sel, payload, -1), axis=0, keepdims=True))
        vals = jnp.where(sel, -jnp.inf, vals)
    return top_v, top_i


def _retrieve(sc_ref, eidx_ref, gate_ref):
    lanes = sc_ref.shape[1]
    key_id = lax.broadcasted_iota(jnp.int32, (N_KEYS, lanes), 0)
    sub16 = lax.broadcasted_iota(jnp.int32, (P_TOPK, lanes), 0)
    sub8 = lax.broadcasted_iota(jnp.int32, (8, lanes), 0)
    cand_id = jnp.concatenate([sub16] + [a * P_TOPK + sub8 for a in range(1, 8)] + [(8 + sub8) * P_TOPK], axis=0)

    def head(h, carry):
        base = pl.multiple_of(h * 2 * N_KEYS, 2 * N_KEYS)
        v0, i0 = _extract_topk(sc_ref[pl.ds(base, N_KEYS), :], key_id, P_TOPK)
        v1, i1 = _extract_topk(sc_ref[pl.ds(base + N_KEYS, N_KEYS), :], key_id, P_TOPK)
        v1_16, i1_16 = jnp.concatenate(v1, axis=0), jnp.concatenate(i1, axis=0)
        v1_8, i1_8 = jnp.concatenate(v1[:8], axis=0), jnp.concatenate(i1[:8], axis=0)
        cand = jnp.concatenate([v0[0] + v1_16] + [v0[a] + v1_8 for a in range(1, 8)]
                               + [jnp.concatenate(v0[8:], axis=0) + v1[0]], axis=0)
        cidx = jnp.concatenate([i0[0] * N_KEYS + i1_16] + [i0[a] * N_KEYS + i1_8 for a in range(1, 8)]
                               + [jnp.concatenate(i0[8:], axis=0) * N_KEYS + i1[0]], axis=0) * TILE_WORDS
        fv, fe = _extract_topk(cand, cand_id, P_TOPK, payload=cidx)
        fv = jnp.concatenate(fv, axis=0)
        e = jnp.exp(fv - fv[0:1])
        out = pl.multiple_of(h * P_TOPK, P_TOPK)
        gate_ref[pl.ds(out, P_TOPK), :] = e / jnp.sum(e, axis=0, keepdims=True)
        eidx_ref[pl.ds(out, P_TOPK), :] = jnp.concatenate(fe, axis=0)
        return carry

    lax.fori_loop(0, P_HEADS, head, 0)


def _tile_table(t):
    pairs = t.astype(BF16).reshape(t.shape[0], TILE_WORDS, 2, ROW_SHAPE[1]).transpose(0, 1, 3, 2)
    return lax.bitcast_convert_type(pairs, jnp.uint32).reshape(t.shape[0] * TILE_WORDS, ROW_SHAPE[1])


def _expert_row(tbl_ref, word_row):
    words = tbl_ref[pl.ds(pl.multiple_of(word_row, TILE_WORDS), TILE_WORDS), :]
    return pltpu.bitcast(words, BF16).astype(F32)


def _sublane_fold(x, y, step, mask):
    if step == 4:
        return jnp.where(mask, x, y) + pltpu.roll(jnp.where(mask, y, x), 4, 0)
    return jnp.where(mask, x + pltpu.roll(x, 8 - step, 0), y + pltpu.roll(y, step, 0))


def _peer_u_kernel(idx_ref, hn_ref, gate_ref, tbl_ref, coef_ref, part_scr, hid_scr):
    toks = hn_ref.shape[0]
    sub = lax.broadcasted_iota(jnp.int32, ROW_SHAPE, 0)
    m4, m2, m1 = sub < 4, (sub & 3) < 2, (sub & 1) == 0
    lane_tok = lax.broadcasted_iota(jnp.int32, (N_SEL, toks), 1)
    part_scr[...] = jnp.zeros_like(part_scr)
    hid_scr[...] = jnp.zeros_like(hid_scr)

    def finish(t, slot):
        hid_scr[...] = jnp.where(lane_tok == t, jnp.sum(part_scr[slot], axis=1, keepdims=True), hid_scr[...])

    def token(t, carry):
        slot = t % 2
        x = hn_ref[t]
        finish(t - 1, 1 - slot)
        for h in range(N_SEL // 8):
            prod = [_expert_row(tbl_ref, idx_ref[t * N_SEL + h * 8 + j]) * x for j in range(8)]
            z = [_sublane_fold(prod[a], prod[a + 4], 4, m4) for a in (0, 2, 1, 3)]
            w0 = _sublane_fold(z[0], z[1], 2, m2)
            w1 = _sublane_fold(z[2], z[3], 2, m2)
            part_scr[slot, h * 8:(h + 1) * 8, :] = _sublane_fold(w0, w1, 1, m1)
        return carry

    lax.fori_loop(0, toks, token, 0)
    finish(toks - 1, (toks - 1) % 2)
    hid = hid_scr[...]
    coef_ref[...] = gate_ref[...] * (0.5 * hid * (1.0 + lax.erf(hid * (2.0 ** -0.5))))


def _peer_u(eidx_flat, hn3, gate, table):
    m = hn3.shape[0]
    tt = TOK_TILE
    return pl.pallas_call(
        _peer_u_kernel,
        grid=(m // tt,),
        in_specs=[pl.BlockSpec((tt * N_SEL,), lambda i: (i,), memory_space=pltpu.SMEM),
                  pl.BlockSpec((tt, *ROW_SHAPE), lambda i: (i, 0, 0)),
                  pl.BlockSpec((N_SEL, tt), lambda i: (0, i)),
                  _resident(table.shape)],
        out_specs=pl.BlockSpec((N_SEL, tt), lambda i: (0, i)),
        out_shape=jax.ShapeDtypeStruct((N_SEL, m), F32),
        scratch_shapes=[pltpu.VMEM((2, N_SEL, ROW_SHAPE[1]), F32), pltpu.VMEM((N_SEL, tt), F32)],
        compiler_params=_params(dimension_semantics=("parallel",)),
        name="peer_u",
    )(eidx_flat, hn3, gate, table)


def _peer_v_kernel(idx_ref, coef_ref, x_ref, tbl_ref, g_ref, y_ref, splat_scr, x2_scr):
    toks = x_ref.shape[0]
    n_acc = 4
    lane_tok = lax.broadcasted_iota(jnp.int32, (N_SEL, toks), 1)

    def splat(t):
        col = jnp.sum(jnp.where(lane_tok == t, coef_ref[...], 0.0), axis=1, keepdims=True)
        return jnp.broadcast_to(col, (N_SEL, ROW_SHAPE[1]))

    splat_scr[0] = splat(0)

    def token(t, carry):
        slot = t % 2
        nxt = splat(jnp.minimum(t + 1, toks - 1))
        acc = [jnp.zeros(ROW_SHAPE, F32) for _ in range(n_acc)]
        for h in range(N_SEL // 8):
            coef = splat_scr[slot, h * 8:(h + 1) * 8, :]
            for j in range(8):
                row = _expert_row(tbl_ref, idx_ref[t * N_SEL + h * 8 + j])
                acc[j % n_acc] = acc[j % n_acc] + coef[j:j + 1, :] * row
        x2_scr[t] = x_ref[t] + ((acc[0] + acc[1]) + (acc[2] + acc[3]))
        splat_scr[1 - slot] = nxt
        return carry

    lax.fori_loop(0, toks, token, 0)
    x2 = jnp.concatenate([x2_scr[:, s, :] for s in range(ROW_SHAPE[0])], axis=1)
    ms = jnp.mean(x2 * x2, axis=-1, keepdims=True)
    y_ref[...] = x2 * lax.rsqrt(ms + NORM_EPS) * g_ref[...]


def _peer_v(eidx_flat, coef, x3, table, normf_g):
    m = x3.shape[0]
    tt = TOK_TILE
    tok = pl.BlockSpec((tt, *ROW_SHAPE), lambda i: (i, 0, 0))
    return pl.pallas_call(
        _peer_v_kernel,
        grid=(m // tt,),
        in_specs=[pl.BlockSpec((tt * N_SEL,), lambda i: (i,), memory_space=pltpu.SMEM),
                  pl.BlockSpec((N_SEL, tt), lambda i: (0, i)), tok, _resident(table.shape), _resident((1, D_MODEL))],
        out_specs=pl.BlockSpec((tt, D_MODEL), lambda i: (i, 0)),
        out_shape=jax.ShapeDtypeStruct((m, D_MODEL), F32),
        scratch_shapes=[pltpu.VMEM((2, N_SEL, ROW_SHAPE[1]), F32), pltpu.VMEM((tt, *ROW_SHAPE), F32)],
        compiler_params=_params(dimension_semantics=("parallel",)),
        name="peer_v",
    )(eidx_flat, coef, x3, table, normf_g)


def _stream_step(x, k_cache, v_cache, wkv0, shift0, w):
    batch, seq, _ = x.shape
    m = batch * seq
    assert m % ROW_TILE == 0 and m % TOK_TILE == 0 and batch % SCAN_BATCH == 0 and seq % SHIFT_GROUP == 0
    assert (seq % ROW_TILE == 0 and seq % SCAN_TILE == 0) or ROW_TILE % seq == 0
    za, zb, zg = _in_proj(x.reshape(m, D_MODEL), w["norm1_g"], w["w_in"])
    if k_cache is None:
        assert seq % ATT_TILE == 0
        o_a = _attn_prompt(za, w["rel_bias"], batch, seq)
    else:
        o_a = _attn_sample(za, k_cache, v_cache, w["rel_bias"], seq)
    r, dec, k, v, a, b, g, bonus = _rwkv_prep(zb, shift0, w, batch, seq)
    y, wkv = _wkv_scan(r, dec, k, v, a, b, wkv0, batch, seq)
    x1, hn, eidx, gate = _post_mix(x.reshape(m, D_MODEL), o_a, y, bonus, g, zg, w, batch, seq)
    eidx_flat = eidx.T.reshape(-1)
    coef = _peer_u(eidx_flat, hn, gate, w["expert_u"])
    out = _peer_v(eidx_flat, coef, x1, w["expert_v"], w["normf_g"])

    keep = min(BAND_CHUNKS * CHUNK, seq) if k_cache is None else seq
    zk = za.reshape(batch, seq, A_COLS)[:, seq - keep:]
    heads = lambda t: t.reshape(batch, keep, HEADS, HEAD_DIM).transpose(0, 2, 1, 3)
    return (out.reshape(batch, seq, D_MODEL), heads(zk[..., WIDTH:2 * WIDTH]), heads(zk[..., 2 * WIDTH:]), wkv,
            zb.reshape(batch, seq, B_COLS)[:, -1:])


def kernel(x_prompt, x_sample, cache_attn_k, cache_attn_v, state_wkv, state_shift, norm1_g, w_in, rel_bias, shift_mu,
           w_decay0, w_decay_up, a0, w_a_up, w_g_up, k_k, k_a, r_k, lnx_g, lnx_b, w_proj_a, w_proj_b, w_out, norm2_g,
           w_query, sub_keys, expert_u, expert_v, normf_g):
    assert norm1_g.shape[0] == 1, "single-layer step"
    zeros = jnp.zeros((DECAY_RANK, WIDTH), F32)
    score_w = _score_weights(w_query[0], sub_keys[0])
    score_hi = score_w.astype(BF16)
    w = dict(
        norm1_g=norm1_g[0], w_in=w_in[0].astype(BF16), rel_bias=rel_bias[0],
        shift_mu=_row_vec(shift_mu[0]), w_decay0=_row_vec(w_decay0[0]), a0=_row_vec(a0[0]),
        w_decay_up=jnp.concatenate([w_decay_up[0], zeros], axis=0), w_a_up=jnp.concatenate([zeros, w_a_up[0]], axis=0),
        w_g_up=w_g_up[0], k_k=_row_vec(k_k[0]), k_a=_row_vec(k_a[0]), r_k=_row_vec(r_k[0]),
        head_sum=_head_sum_matrix(1.0), head_mean=_head_sum_matrix(1.0 / HEAD_DIM),
        lnx_g=_row_vec(lnx_g[0]), lnx_b=_row_vec(lnx_b[0]),
        w_proj_a=w_proj_a[0].astype(BF16), w_proj_b=w_proj_b[0].astype(BF16), w_out=w_out[0].astype(BF16),
        norm2_g=_row_vec(norm2_g[0]), score_hi=score_hi, score_lo=(score_w - score_hi.astype(F32)).astype(BF16),
        expert_u=_tile_table(expert_u[0]), expert_v=_tile_table(expert_v[0]),
        normf_g=_row_vec(normf_g))

    batch = x_prompt.shape[0]
    yp, kp, vp, wp, sp = _stream_step(x_prompt, None, None, jnp.zeros((batch, HEADS, HEAD_DIM, HEAD_DIM), F32),
                                      jnp.zeros((batch, 1, B_COLS), F32), w)
    ys, ks, vs, ws, ss = _stream_step(x_sample, cache_attn_k[0], cache_attn_v[0], state_wkv[0], state_shift[0], w)
    return (yp, ys, kp[None], vp[None], wp[None], sp[None], ks[None], vs[None], ws[None], ss[None])
```

```python
import functools

import jax
import jax.numpy as jnp
from jax import lax
from jax.experimental import pallas as pl
from jax.experimental.pallas import tpu as pltpu

F32 = jnp.float32
BF16 = jnp.bfloat16
HIGHEST = lax.Precision.HIGHEST

D_MODEL = 1024
CHUNK = 64
BAND_CHUNKS = 8
HEADS = 8
HEAD_DIM = 64
WIDTH = HEADS * HEAD_DIM
REL_CLIP = 128
DECAY_RANK = 64
AAA_RANK = 64
GATE_RANK = 128
LNX_EPS = 64e-5
NORM_EPS = 1e-6
P_HEADS = 8
N_KEYS = 128
P_HALF = 64
P_TOPK = 16
A_COLS = 3 * WIDTH
B_COLS = 3 * WIDTH + DECAY_RANK + AAA_RANK + GATE_RANK
G_COLS = 2 * D_MODEL
NEG = -1e30

VMEM_LIMIT = 56 * 1024 * 1024
ROW_TILE = 256
ATT_TILE = BAND_CHUNKS * CHUNK
BAND_KEYS = (BAND_CHUNKS + 1) * CHUNK
SCAN_TILE = 128
SCAN_BATCH = 2
SCAN_UNROLL = 4
TOK_TILE = 128
SHIFT_GROUP = 32
PAIRS = HEADS // 2
PAIR_W = 2 * HEAD_DIM
N_SEL = P_HEADS * P_TOPK
ROW_SHAPE = (8, D_MODEL // 8)
TILE_WORDS = ROW_SHAPE[0] // 2


def _params(**kw):
    return pltpu.CompilerParams(vmem_limit_bytes=VMEM_LIMIT, **kw)


def _resident(shape):
    nd = len(shape)
    return pl.BlockSpec(shape, lambda *_: (0,) * nd, pipeline_mode=pl.Buffered(1))


def _sigmoid(x):
    return 1.0 / (1.0 + jnp.exp(-x))


def _dot(a, b, **kw):
    return jnp.dot(a, b, preferred_element_type=F32, **kw)


def _dot_nt(a, b, **kw):
    return lax.dot_general(a, b, (((1,), (1,)), ((), ())), preferred_element_type=F32, **kw)


def _row_vec(a):
    return a.reshape(1, -1).astype(F32)


def _split_sums(x, ones_bf16):
    hi = x.astype(BF16)
    lo = (x - hi.astype(F32)).astype(BF16)
    return _dot(hi, ones_bf16) + _dot(lo, ones_bf16)


def _inproj_kernel(x_ref, g_ref, w_ref, za_ref, zb_ref, zg_ref):
    x = x_ref[...]
    y = x * lax.rsqrt(jnp.mean(x * x, axis=-1, keepdims=True) + NORM_EPS) * g_ref[...]
    yb = y.astype(BF16)
    za_ref[...] = _dot(yb, w_ref[:, :A_COLS])
    zb_ref[...] = _dot(yb, w_ref[:, A_COLS:A_COLS + B_COLS])
    zg_ref[...] = _dot(yb, w_ref[:, A_COLS + B_COLS:])


def _in_proj(x, norm_g, w_in_bf16):
    m = x.shape[0]
    in_cols = w_in_bf16.shape[1]
    row = lambda i: (i, 0)
    return pl.pallas_call(
        _inproj_kernel,
        grid=(m // ROW_TILE,),
        in_specs=[pl.BlockSpec((ROW_TILE, D_MODEL), row), _resident((1, D_MODEL)), _resident((D_MODEL, in_cols))],
        out_specs=[pl.BlockSpec((ROW_TILE, A_COLS), row), pl.BlockSpec((ROW_TILE, B_COLS), row),
                   pl.BlockSpec((ROW_TILE, G_COLS), row)],
        out_shape=[jax.ShapeDtypeStruct((m, A_COLS), F32), jax.ShapeDtypeStruct((m, B_COLS), F32),
                   jax.ShapeDtypeStruct((m, G_COLS), F32)],
        compiler_params=_params(dimension_semantics=("parallel",)),
        name="in_proj",
    )(x, _row_vec(norm_g), w_in_bf16)


def _softmax_pv(scores, values):
    m = scores[0].max(axis=-1, keepdims=True)
    for s in scores[1:]:
        m = jnp.maximum(m, s.max(axis=-1, keepdims=True))
    acc, den = None, None
    for s, v in zip(scores, values):
        p = jnp.exp(s - m)
        d = p.sum(axis=-1, keepdims=True)
        o = _dot(p.astype(BF16), v)
        acc = o if acc is None else acc + o
        den = d if den is None else den + d
    return acc / den


def _attn_prompt_kernel(q_ref, kp_ref, kc_ref, vp_ref, vc_ref, bias_ref, o_ref):
    first = pl.program_id(1) == 0
    scale = HEAD_DIM ** -0.5
    for h in range(HEADS):
        sl = slice(h * HEAD_DIM, (h + 1) * HEAD_DIM)
        q = q_ref[:, sl].astype(BF16)
        s_prev = _dot_nt(q, kp_ref[:, sl].astype(BF16)) * scale + bias_ref[h, :, :ATT_TILE]
        s_cur = _dot_nt(q, kc_ref[:, sl].astype(BF16)) * scale + bias_ref[h, :, ATT_TILE:]
        s_prev = jnp.where(first, NEG, s_prev)
        o_ref[:, sl] = _softmax_pv([s_prev, s_cur], [vp_ref[:, sl].astype(BF16), vc_ref[:, sl].astype(BF16)])


def _prompt_bias_table(rel_bias):
    near = BAND_KEYS - (REL_CLIP // CHUNK + 1) * CHUNK
    dist = jnp.arange(CHUNK)[:, None] + BAND_CHUNKS * CHUNK - jnp.arange(near, BAND_KEYS)[None, :]
    varying = rel_bias[:, jnp.clip(dist, -REL_CLIP, REL_CLIP) + REL_CLIP]
    far = jnp.broadcast_to(rel_bias[:, -1][:, None, None], (HEADS, CHUNK, near))
    window = jnp.concatenate([far, varying], axis=2).astype(F32)
    rows = [jnp.pad(window, ((0, 0), (0, 0), (c * CHUNK, 2 * ATT_TILE - BAND_KEYS - c * CHUNK)), constant_values=NEG)
            for c in range(BAND_CHUNKS)]
    return jnp.concatenate(rows, axis=1)


def _attn_prompt(za, rel_bias, batch, seq):
    nb = seq // ATT_TILE
    blk = (ATT_TILE, WIDTH)
    cur = lambda col: (lambda b, i: (b * nb + i, col))
    prev = lambda col: (lambda b, i: (b * nb + jnp.maximum(i - 1, 0), col))
    return pl.pallas_call(
        _attn_prompt_kernel,
        grid=(batch, nb),
        in_specs=[pl.BlockSpec(blk, cur(0)), pl.BlockSpec(blk, prev(1)), pl.BlockSpec(blk, cur(1)),
                  pl.BlockSpec(blk, prev(2)), pl.BlockSpec(blk, cur(2)),
                  _resident((HEADS, ATT_TILE, 2 * ATT_TILE))],
        out_specs=pl.BlockSpec(blk, cur(0)),
        out_shape=jax.ShapeDtypeStruct((batch * seq, WIDTH), F32),
        compiler_params=_params(dimension_semantics=("parallel", "arbitrary")),
        name="attn_prompt",
    )(za, za, za, za, za, _prompt_bias_table(rel_bias))


def _attn_sample_kernel(q_ref, kn_ref, vn_ref, kc_ref, vc_ref, bc_ref, bn_ref, o_ref):
    scale = HEAD_DIM ** -0.5
    for h in range(HEADS):
        sl = slice(h * HEAD_DIM, (h + 1) * HEAD_DIM)
        q = q_ref[:, sl].astype(BF16)
        s_cache = _dot_nt(q, kc_ref[0, h].astype(BF16)) * scale + bc_ref[h]
        s_new = _dot_nt(q, kn_ref[:, sl].astype(BF16)) * scale + bn_ref[h]
        o_ref[:, sl] = _softmax_pv([s_cache, s_new], [vc_ref[0, h].astype(BF16), vn_ref[:, sl].astype(BF16)])


def _attn_sample(za, k_cache, v_cache, rel_bias, seq):
    nb, _, past, _ = k_cache.shape
    dist = jnp.arange(seq)[:, None] + past - jnp.arange(past + seq)[None, :]
    bias = rel_bias[:, jnp.clip(dist, -REL_CLIP, REL_CLIP) + REL_CLIP].astype(F32)
    blk = (seq, WIDTH)
    rows = lambda col: (lambda b: (b, col))
    cache = pl.BlockSpec((1, HEADS, past, HEAD_DIM), lambda b: (b, 0, 0, 0))
    return pl.pallas_call(
        _attn_sample_kernel,
        grid=(nb,),
        in_specs=[pl.BlockSpec(blk, rows(0)), pl.BlockSpec(blk, rows(1)), pl.BlockSpec(blk, rows(2)), cache, cache,
                  _resident((HEADS, seq, past)), _resident((HEADS, seq, seq))],
        out_specs=pl.BlockSpec(blk, rows(0)),
        out_shape=jax.ShapeDtypeStruct((nb * seq, WIDTH), F32),
        compiler_params=_params(dimension_semantics=("parallel",)),
        name="attn_sample",
    )(za, za, za, k_cache, v_cache, bias[:, :, :past], bias[:, :, past:])


def _softplus(x):
    return jnp.maximum(x, 0.0) + jnp.log(1.0 + jnp.exp(-jnp.abs(x)))


def _rwkv_prep_kernel(zb_ref, prev_ref, shift_ref, mu_ref, wd0_ref, wdu_ref, a0_ref, wau_ref, wgu_ref, kk_ref, ka_ref,
                      rk_ref, gsum_ref, r_o, w_o, k_o, v_o, a_o, b_o, g_o, bonus_o, *, seq):
    i = pl.program_id(0)
    zb = zb_ref[...]
    tm = zb.shape[0]
    row = lax.broadcasted_iota(jnp.int32, zb.shape, 0)
    prev = jnp.where(row == 0, prev_ref[7:8, :], pltpu.roll(zb, 1, 0))
    ngrp = tm // SHIFT_GROUP
    shift = jnp.broadcast_to(shift_ref[...][:, None, :], (ngrp, SHIFT_GROUP, B_COLS)).reshape(tm, B_COLS)
    prev = jnp.where(lax.rem(i * tm + row, seq) == 0, shift, prev)
    zm = zb + (prev - zb) * mu_ref[...]
    r = zm[:, 0:WIDTH]
    k = zm[:, WIDTH:2 * WIDTH]
    v = zm[:, 2 * WIDTH:3 * WIDTH]
    lora_in = zm[:, 3 * WIDTH:3 * WIDTH + DECAY_RANK + AAA_RANK]
    gate_in = zm[:, 3 * WIDTH + DECAY_RANK + AAA_RANK:]
    w_log = -_softplus(-(wd0_ref[...] + _dot(jnp.tanh(lora_in), wdu_ref[...], precision=HIGHEST))) - 0.5
    decay = jnp.exp(-jnp.exp(w_log))
    a = _sigmoid(a0_ref[...] + _dot(lora_in, wau_ref[...], precision=HIGHEST))
    g = _dot(_sigmoid(gate_in), wgu_ref[...], precision=HIGHEST)
    kk = k * kk_ref[...]
    kk = kk / jnp.maximum(jnp.sqrt(_split_sums(kk * kk, gsum_ref[...])), 1e-12)
    kmod = k * (1.0 + (a - 1.0) * ka_ref[...])
    for ref, val in ((r_o, r), (w_o, decay), (k_o, kmod), (v_o, v), (a_o, -kk), (b_o, kk * a)):
        for pair in range(PAIRS):
            ref[pair] = val[:, pair * PAIR_W:(pair + 1) * PAIR_W].reshape(ref.shape[1:])
    g_o[...] = g
    bonus_o[...] = _split_sums(r * kmod * rk_ref[...], gsum_ref[...]) * v


def _head_sum_matrix():
    h = jnp.arange(WIDTH) // HEAD_DIM
    return jnp.where(h[:, None] == h[None, :], 1.0, 0.0).astype(BF16)


def _scan_layout(batch, seq):
    steps = min(seq, SCAN_TILE)
    shape = (PAIRS, seq // steps, batch, steps, PAIR_W)
    if seq >= ROW_TILE:
        per_seq = seq // ROW_TILE
        block = (PAIRS, ROW_TILE // steps, 1, steps, PAIR_W)
        index = lambda i: (0, i % per_seq, i // per_seq, 0, 0)
    else:
        block = (PAIRS, 1, ROW_TILE // seq, steps, PAIR_W)
        index = lambda i: (0, 0, i, 0, 0)
    return steps, shape, pl.BlockSpec(block, index)


def _rwkv_prep(zb, shift0, w, batch, seq):
    m = zb.shape[0]
    tm = ROW_TILE
    row = lambda i: (i, 0)
    groups_per_seq = seq // SHIFT_GROUP
    shift_rows = jnp.zeros((batch, groups_per_seq, B_COLS), F32).at[:, 0].set(shift0.reshape(batch, B_COLS))
    _, scan_shape, scan_spec = _scan_layout(batch, seq)
    flat = jax.ShapeDtypeStruct((m, WIDTH), F32)
    return pl.pallas_call(
        functools.partial(_rwkv_prep_kernel, seq=seq),
        grid=(m // tm,),
        in_specs=[pl.BlockSpec((tm, B_COLS), row),
                  pl.BlockSpec((8, B_COLS), lambda i: (jnp.maximum(i * (tm // 8) - 1, 0), 0)),
                  pl.BlockSpec((tm // SHIFT_GROUP, B_COLS), row),
                  _resident((1, B_COLS)), _resident((1, WIDTH)), _resident((DECAY_RANK + AAA_RANK, WIDTH)),
                  _resident((1, WIDTH)), _resident((DECAY_RANK + AAA_RANK, WIDTH)), _resident((GATE_RANK, WIDTH)),
                  _resident((1, WIDTH)), _resident((1, WIDTH)), _resident((1, WIDTH)), _resident((WIDTH, WIDTH))],
        out_specs=[scan_spec] * 6 + [pl.BlockSpec((tm, WIDTH), row)] * 2,
        out_shape=[jax.ShapeDtypeStruct(scan_shape, F32)] * 6 + [flat] * 2,
        compiler_params=_params(dimension_semantics=("parallel",)),
        name="rwkv_prep",
    )(zb, zb, shift_rows.reshape(m // SHIFT_GROUP, B_COLS), w["shift_mu"], w["w_decay0"], w["w_decay_up"], w["a0"],
      w["w_a_up"], w["w_g_up"], w["k_k"], w["k_a"], w["r_k"], w["head_sum"])


def _scan_kernel(r_ref, w_ref, k_ref, v_ref, a_ref, b_ref, s0_ref, y_ref, st_ref, s_scr, *, steps):
    tb = pl.program_id(1)
    chains = [(b, p) for b in range(s_scr.shape[0]) for p in range(PAIRS)]

    @pl.when(tb == 0)
    def _():
        s_scr[...] = s0_ref[...]

    lane = lax.broadcasted_iota(jnp.int32, (HEAD_DIM, PAIR_W), 1)
    sub = lax.broadcasted_iota(jnp.int32, (HEAD_DIM, PAIR_W), 0)
    lo = lane < HEAD_DIM
    diag = (lane & (HEAD_DIM - 1)) == sub
    sub8 = lax.broadcasted_iota(jnp.int32, (8, PAIR_W), 0)

    def head_sums(x):
        s_lo = jnp.sum(jnp.where(lo, x, 0.0), axis=1, keepdims=True)
        s_hi = jnp.sum(jnp.where(lo, 0.0, x), axis=1, keepdims=True)
        return jnp.where(lo, s_lo, s_hi)

    def row_of(ref, chain, t):
        b, p = chain
        tile8, j = t
        return ref[p, 0, b, pl.ds(pl.multiple_of(tile8 * 8, 8), 8), :][j:j + 1, :]

    same_head = ((lax.broadcasted_iota(jnp.int32, (PAIR_W, PAIR_W), 0) < HEAD_DIM)
                 == (lax.broadcasted_iota(jnp.int32, (PAIR_W, PAIR_W), 1) < HEAD_DIM))
    head_ones = jnp.where(same_head, 1.0, 0.0).astype(BF16)

    def head_sums_mxu(x):
        hi = x.astype(BF16)
        lo = (x - hi.astype(F32)).astype(BF16)
        sums = _dot(jnp.concatenate([hi, lo], axis=0), head_ones)
        return sums[:HEAD_DIM] + sums[HEAD_DIM:]

    def emit_y(chain, t, s):
        b, p = chain
        y_col = head_sums_mxu(s * row_of(r_ref, chain, t))
        y_ref[p, 0, b, pl.ds(t[0] * 8 + t[1], 1), :] = jnp.sum(jnp.where(diag, y_col, 0.0), axis=0, keepdims=True)

    def update(chain, t, s):
        sums = head_sums_mxu if chain[1] else head_sums
        sa = sums(s * row_of(a_ref, chain, t))
        v_col = head_sums(jnp.where(diag, row_of(v_ref, chain, t), 0.0))
        return s * row_of(w_ref, chain, t) + sa * row_of(b_ref, chain, t) + v_col * row_of(k_ref, chain, t)

    def step(tile8, j):
        prev = (tile8, j - 1) if j else (tile8 - 1, 7)
        for b, p in chains:
            s = s_scr[b, p]
            if not (isinstance(tile8, int) and tile8 == 0 and j == 0):
                emit_y((b, p), prev, s)
            s_scr[b, p] = update((b, p), (tile8, j), s)

    def trip(tile8, carry):
        for j in range(8):
            step(tile8, j)
        return carry

    trip(0, 0)
    lax.fori_loop(1, steps // 8, trip, 0)
    for b, p in chains:
        emit_y((b, p), (steps // 8 - 1, 7), s_scr[b, p])

    @pl.when(tb == pl.num_programs(1) - 1)
    def _():
        st_ref[...] = s_scr[...]


def _pair_state(s):
    b = s.shape[0]
    return s.reshape(b, PAIRS, 2, HEAD_DIM, HEAD_DIM).transpose(0, 1, 3, 2, 4).reshape(b, PAIRS, HEAD_DIM, PAIR_W)


def _unpair_state(s):
    b = s.shape[0]
    return s.reshape(b, PAIRS, HEAD_DIM, 2, HEAD_DIM).transpose(0, 1, 3, 2, 4).reshape(b, HEADS, HEAD_DIM, HEAD_DIM)


def _wkv_scan(r, w, k, v, a, b, s0, batch, seq):
    steps, shape, _ = _scan_layout(batch, seq)
    bg = SCAN_BATCH
    blk = pl.BlockSpec((PAIRS, 1, bg, steps, PAIR_W), lambda bi, ti: (0, ti, bi, 0, 0))
    state = pl.BlockSpec((bg, PAIRS, HEAD_DIM, PAIR_W), lambda bi, ti: (bi, 0, 0, 0))
    y, st = pl.pallas_call(
        functools.partial(_scan_kernel, steps=steps),
        grid=(batch // bg, seq // steps),
        in_specs=[blk] * 6 + [state],
        out_specs=[blk, state],
        out_shape=[jax.ShapeDtypeStruct(shape, F32), jax.ShapeDtypeStruct((batch, PAIRS, HEAD_DIM, PAIR_W), F32)],
        scratch_shapes=[pltpu.VMEM((bg, PAIRS, HEAD_DIM, PAIR_W), F32)],
        compiler_params=_params(dimension_semantics=("parallel", "arbitrary")),
        name="wkv_scan",
    )(r, w, k, v, a, b, _pair_state(s0.astype(F32)))
    return y, _unpair_state(st)


def _postmix_kernel(x_ref, oa_ref, y_ref, bonus_ref, g_ref, zg_ref, lng_ref, lnb_ref, gsum_ref, wpa_ref, wpb_ref,
                    wout_ref, n2g_ref, wch_ref, wcl_ref, x1_ref, hn_ref, eidx_ref, gate_ref, sc_scr, eidx_scr):
    tm = x_ref.shape[0]
    y = jnp.concatenate([y_ref[pair].reshape(tm, PAIR_W) for pair in range(PAIRS)], axis=1)
    mu = _split_sums(y, gsum_ref[...]) * (1.0 / HEAD_DIM)
    d = y - mu
    var = _split_sums(d * d, gsum_ref[...]) * (1.0 / HEAD_DIM)
    yn = d * lax.rsqrt(var + LNX_EPS) * lng_ref[...] + lnb_ref[...]
    ob = (yn + bonus_ref[...]) * g_ref[...]
    pa = _dot(oa_ref[...].astype(BF16), wpa_ref[...])
    pb = _dot(ob.astype(BF16), wpb_ref[...])
    merged = _sigmoid(zg_ref[:, :D_MODEL]) * pa + _sigmoid(zg_ref[:, D_MODEL:]) * pb
    x1 = x_ref[...] + _dot(merged.astype(BF16), wout_ref[...])
    hn = x1 * lax.rsqrt(jnp.mean(x1 * x1, axis=-1, keepdims=True) + NORM_EPS) * n2g_ref[...]
    for s in range(ROW_SHAPE[0]):
        cols = slice(s * ROW_SHAPE[1], (s + 1) * ROW_SHAPE[1])
        x1_ref[:, s, :] = x1[:, cols]
        hn_ref[:, s, :] = hn[:, cols]
    hh = hn.astype(BF16)
    hl = (hn - hh.astype(F32)).astype(BF16)
    sc_scr[...] = _dot_nt(wch_ref[...], hh) + (_dot_nt(wcl_ref[...], hh) + _dot_nt(wch_ref[...], hl))
    _retrieve(sc_scr, eidx_scr, gate_ref)
    eidx_ref[...] = eidx_scr[...].T


def _post_mix(x, o_a, y, bonus, g, zg, w, batch, seq):
    m = x.shape[0]
    tm = ROW_TILE
    row = lambda i: (i, 0)
    nsc = w["score_hi"].shape[0]
    wide = pl.BlockSpec((tm, D_MODEL), row)
    half = pl.BlockSpec((tm, WIDTH), row)
    tiles = pl.BlockSpec((tm, *ROW_SHAPE), lambda i: (i, 0, 0))
    picks = pl.BlockSpec((N_SEL, tm), lambda i: (0, i))
    _, _, scan_spec = _scan_layout(batch, seq)
    return pl.pallas_call(
        _postmix_kernel,
        grid=(m // tm,),
        in_specs=[wide, half, scan_spec, half, half, pl.BlockSpec((tm, G_COLS), row),
                  _resident((1, WIDTH)), _resident((1, WIDTH)), _resident((WIDTH, WIDTH)),
                  _resident((WIDTH, D_MODEL)), _resident((WIDTH, D_MODEL)), _resident((D_MODEL, D_MODEL)),
                  _resident((1, D_MODEL)), _resident((nsc, D_MODEL)), _resident((nsc, D_MODEL))],
        out_specs=[tiles, tiles, pl.BlockSpec((tm, N_SEL), row), picks],
        out_shape=[jax.ShapeDtypeStruct((m, *ROW_SHAPE), F32), jax.ShapeDtypeStruct((m, *ROW_SHAPE), F32),
                   jax.ShapeDtypeStruct((m, N_SEL), jnp.int32), jax.ShapeDtypeStruct((N_SEL, m), F32)],
        scratch_shapes=[pltpu.VMEM((nsc, tm), F32), pltpu.VMEM((N_SEL, tm), jnp.int32)],
        compiler_params=_params(dimension_semantics=("parallel",)),
        name="post_mix",
    )(x, o_a, y, bonus, g, zg, w["lnx_g"], w["lnx_b"], w["head_sum"], w["w_proj_a"], w["w_proj_b"], w["w_out"],
      w["norm2_g"], w["score_hi"], w["score_lo"])


def _score_weight_kernel(wq_ref, sk_ref, o_ref):
    for c in range(2):
        wq = wq_ref[:, c * P_HALF:(c + 1) * P_HALF]
        o_ref[c * N_KEYS:(c + 1) * N_KEYS, :] = _dot_nt(sk_ref[0, c], wq, precision=HIGHEST)


def _score_weights(w_query, sub_keys):
    return pl.pallas_call(
        _score_weight_kernel,
        grid=(P_HEADS,),
        in_specs=[pl.BlockSpec((D_MODEL, 2 * P_HALF), lambda h: (0, h)),
                  pl.BlockSpec((1, 2, N_KEYS, P_HALF), lambda h: (h, 0, 0, 0))],
        out_specs=pl.BlockSpec((2 * N_KEYS, D_MODEL), lambda h: (h, 0)),
        out_shape=jax.ShapeDtypeStruct((P_HEADS * 2 * N_KEYS, D_MODEL), F32),
        compiler_params=_params(dimension_semantics=("parallel",)),
        name="score_weights",
    )(w_query, sub_keys)


N_CAND = P_TOPK + 7 * 8 + 8


def _extract_topk(vals, ids, n, payload=None):
    top_v, top_i = [], []
    big = jnp.int32(2 ** 30)
    for _ in range(n):
        m = jnp.max(vals, axis=0, keepdims=True)
        pick = jnp.min(jnp.where(vals == m, ids, big), axis=0, keepdims=True)
        sel = ids == pick
        top_v.append(m)
        top_i.append(pick if payload is None else jnp.max(jnp.where(sel, payload, -1), axis=0, keepdims=True))
        vals = jnp.where(sel, -jnp.inf, vals)
    return top_v, top_i


def _retrieve(sc_ref, eidx_ref, gate_ref):
    lanes = sc_ref.shape[1]
    key_id = lax.broadcasted_iota(jnp.int32, (N_KEYS, lanes), 0)
    sub16 = lax.broadcasted_iota(jnp.int32, (P_TOPK, lanes), 0)
    sub8 = lax.broadcasted_iota(jnp.int32, (8, lanes), 0)
    cand_id = jnp.concatenate([sub16] + [a * P_TOPK + sub8 for a in range(1, 8)] + [(8 + sub8) * P_TOPK], axis=0)

    def head(h, carry):
        base = pl.multiple_of(h * 2 * N_KEYS, 2 * N_KEYS)
        v0, i0 = _extract_topk(sc_ref[pl.ds(base, N_KEYS), :], key_id, P_TOPK)
        v1, i1 = _extract_topk(sc_ref[pl.ds(base + N_KEYS, N_KEYS), :], key_id, P_TOPK)
        v1_16, i1_16 = jnp.concatenate(v1, axis=0), jnp.concatenate(i1, axis=0)
        v1_8, i1_8 = jnp.concatenate(v1[:8], axis=0), jnp.concatenate(i1[:8], axis=0)
        cand = jnp.concatenate([v0[0] + v1_16] + [v0[a] + v1_8 for a in range(1, 8)]
                               + [jnp.concatenate(v0[8:], axis=0) + v1[0]], axis=0)
        cidx = jnp.concatenate([i0[0] * N_KEYS + i1_16] + [i0[a] * N_KEYS + i1_8 for a in range(1, 8)]
                               + [jnp.concatenate(i0[8:], axis=0) * N_KEYS + i1[0]], axis=0) * TILE_WORDS
        fv, fe = _extract_topk(cand, cand_id, P_TOPK, payload=cidx)
        fv = jnp.concatenate(fv, axis=0)
        e = jnp.exp(fv - fv[0:1])
        out = pl.multiple_of(h * P_TOPK, P_TOPK)
        gate_ref[pl.ds(out, P_TOPK), :] = e / jnp.sum(e, axis=0, keepdims=True)
        eidx_ref[pl.ds(out, P_TOPK), :] = jnp.concatenate(fe, axis=0)
        return carry

    lax.fori_loop(0, P_HEADS, head, 0)


def _tile_table_kernel(t_ref, o_ref):
    rows = t_ref.shape[0]
    lanes = ROW_SHAPE[1]
    for r in range(TILE_WORDS):
        lo = t_ref[:, (2 * r) * lanes:(2 * r + 1) * lanes].astype(BF16).astype(F32)
        hi = t_ref[:, (2 * r + 1) * lanes:(2 * r + 2) * lanes].astype(BF16).astype(F32)
        word = (pltpu.bitcast(lo, jnp.uint32) >> 16) | (pltpu.bitcast(hi, jnp.uint32) & jnp.uint32(0xFFFF0000))
        o_ref[pl.ds(r, rows, stride=TILE_WORDS), :] = word


def _tile_table(t):
    e = t.shape[0]
    rows = ROW_TILE
    return pl.pallas_call(
        _tile_table_kernel,
        grid=(e // rows,),
        in_specs=[pl.BlockSpec((rows, D_MODEL), lambda i: (i, 0))],
        out_specs=pl.BlockSpec((rows * TILE_WORDS, ROW_SHAPE[1]), lambda i: (i, 0)),
        out_shape=jax.ShapeDtypeStruct((e * TILE_WORDS, ROW_SHAPE[1]), jnp.uint32),
        compiler_params=_params(dimension_semantics=("parallel",)),
        name="tile_table",
    )(t)


def _expert_row(tbl_ref, word_row):
    words = tbl_ref[pl.ds(pl.multiple_of(word_row, TILE_WORDS), TILE_WORDS), :]
    return pltpu.bitcast(words, BF16).astype(F32)


def _sublane_fold(x, y, step, mask):
    if step == 4:
        return jnp.where(mask, x, y) + pltpu.roll(jnp.where(mask, y, x), 4, 0)
    return jnp.where(mask, x + pltpu.roll(x, 8 - step, 0), y + pltpu.roll(y, step, 0))


def _peer_u_kernel(idx_ref, hn_ref, gate_ref, tbl_ref, coef_ref, part_scr, hid_scr):
    toks = hn_ref.shape[0]
    sub = lax.broadcasted_iota(jnp.int32, ROW_SHAPE, 0)
    m4, m2, m1 = sub < 4, (sub & 3) < 2, (sub & 1) == 0
    lane_tok = lax.broadcasted_iota(jnp.int32, (N_SEL, toks), 1)
    part_scr[...] = jnp.zeros_like(part_scr)
    hid_scr[...] = jnp.zeros_like(hid_scr)

    def finish(t, slot):
        hid_scr[...] = jnp.where(lane_tok == t, jnp.sum(part_scr[slot], axis=1, keepdims=True), hid_scr[...])

    def token(t, carry):
        slot = t % 2
        x = hn_ref[t]
        finish(t - 1, 1 - slot)
        for h in range(N_SEL // 8):
            prod = [_expert_row(tbl_ref, idx_ref[t * N_SEL + h * 8 + j]) * x for j in range(8)]
            z = [_sublane_fold(prod[a], prod[a + 4], 4, m4) for a in (0, 2, 1, 3)]
            w0 = _sublane_fold(z[0], z[1], 2, m2)
            w1 = _sublane_fold(z[2], z[3], 2, m2)
            part_scr[slot, h * 8:(h + 1) * 8, :] = _sublane_fold(w0, w1, 1, m1)
        return carry

    lax.fori_loop(0, toks, token, 0)
    finish(toks - 1, (toks - 1) % 2)
    hid = hid_scr[...]
    coef_ref[...] = gate_ref[...] * (0.5 * hid * (1.0 + lax.erf(hid * (2.0 ** -0.5))))


def _peer_u(eidx_flat, hn3, gate, table):
    m = hn3.shape[0]
    tt = TOK_TILE
    return pl.pallas_call(
        _peer_u_kernel,
        grid=(m // tt,),
        in_specs=[pl.BlockSpec((tt * N_SEL,), lambda i: (i,), memory_space=pltpu.SMEM),
                  pl.BlockSpec((tt, *ROW_SHAPE), lambda i: (i, 0, 0)),
                  pl.BlockSpec((N_SEL, tt), lambda i: (0, i)),
                  _resident(table.shape)],
        out_specs=pl.BlockSpec((N_SEL, tt), lambda i: (0, i)),
        out_shape=jax.ShapeDtypeStruct((N_SEL, m), F32),
        scratch_shapes=[pltpu.VMEM((2, N_SEL, ROW_SHAPE[1]), F32), pltpu.VMEM((N_SEL, tt), F32)],
        compiler_params=_params(dimension_semantics=("parallel",)),
        name="peer_u",
    )(eidx_flat, hn3, gate, table)


def _peer_v_kernel(idx_ref, coef_ref, x_ref, tbl_ref, g_ref, y_ref, splat_scr, x2_scr):
    toks = x_ref.shape[0]
    n_acc = 3
    lane_tok = lax.broadcasted_iota(jnp.int32, (N_SEL, toks), 1)

    def splat(t):
        col = jnp.sum(jnp.where(lane_tok == t, coef_ref[...], 0.0), axis=1, keepdims=True)
        return jnp.broadcast_to(col, (N_SEL, ROW_SHAPE[1]))

    splat_scr[0] = splat(0)

    def token(t, carry):
        slot = t % 2
        nxt = splat(jnp.minimum(t + 1, toks - 1))
        acc = [jnp.zeros(ROW_SHAPE, F32) for _ in range(n_acc)]
        for h in range(N_SEL // 8):
            coef = splat_scr[slot, h * 8:(h + 1) * 8, :]
            for j in range(8):
                row = _expert_row(tbl_ref, idx_ref[t * N_SEL + h * 8 + j])
                acc[j % n_acc] = acc[j % n_acc] + coef[j:j + 1, :] * row
        x2_scr[t] = x_ref[t] + ((acc[0] + acc[1]) + acc[2])
        splat_scr[1 - slot] = nxt
        return carry

    lax.fori_loop(0, toks, token, 0)
    x2 = jnp.concatenate([x2_scr[:, s, :] for s in range(ROW_SHAPE[0])], axis=1)
    ms = jnp.mean(x2 * x2, axis=-1, keepdims=True)
    y_ref[...] = x2 * lax.rsqrt(ms + NORM_EPS) * g_ref[...]


def _peer_v(eidx_flat, coef, x3, table, normf_g):
    m = x3.shape[0]
    tt = TOK_TILE
    tok = pl.BlockSpec((tt, *ROW_SHAPE), lambda i: (i, 0, 0))
    return pl.pallas_call(
        _peer_v_kernel,
        grid=(m // tt,),
        in_specs=[pl.BlockSpec((tt * N_SEL,), lambda i: (i,), memory_space=pltpu.SMEM),
                  pl.BlockSpec((N_SEL, tt), lambda i: (0, i)), tok, _resident(table.shape), _resident((1, D_MODEL))],
        out_specs=pl.BlockSpec((tt, D_MODEL), lambda i: (i, 0)),
        out_shape=jax.ShapeDtypeStruct((m, D_MODEL), F32),
        scratch_shapes=[pltpu.VMEM((2, N_SEL, ROW_SHAPE[1]), F32), pltpu.VMEM((tt, *ROW_SHAPE), F32)],
        compiler_params=_params(dimension_semantics=("parallel",)),
        name="peer_v",
    )(eidx_flat, coef, x3, table, normf_g)


def _stream_step(x, k_cache, v_cache, wkv0, shift0, w):
    batch, seq, _ = x.shape
    m = batch * seq
    assert m % ROW_TILE == 0 and m % TOK_TILE == 0 and batch % SCAN_BATCH == 0 and seq % SHIFT_GROUP == 0
    assert (seq % ROW_TILE == 0 and seq % SCAN_TILE == 0) or ROW_TILE % seq == 0
    za, zb, zg = _in_proj(x.reshape(m, D_MODEL), w["norm1_g"], w["w_in"])
    if k_cache is None:
        assert seq % ATT_TILE == 0
        o_a = _attn_prompt(za, w["rel_bias"], batch, seq)
    else:
        o_a = _attn_sample(za, k_cache, v_cache, w["rel_bias"], seq)
    r, dec, k, v, a, b, g, bonus = _rwkv_prep(zb, shift0, w, batch, seq)
    y, wkv = _wkv_scan(r, dec, k, v, a, b, wkv0, batch, seq)
    x1, hn, eidx, gate = _post_mix(x.reshape(m, D_MODEL), o_a, y, bonus, g, zg, w, batch, seq)
    eidx_flat = eidx.reshape(-1)
    coef = _peer_u(eidx_flat, hn, gate, w["expert_u"])
    out = _peer_v(eidx_flat, coef, x1, w["expert_v"], w["normf_g"])

    keep = min(BAND_CHUNKS * CHUNK, seq) if k_cache is None else seq
    zk = za.reshape(batch, seq, A_COLS)[:, seq - keep:]
    heads = lambda t: t.reshape(batch, keep, HEADS, HEAD_DIM).transpose(0, 2, 1, 3)
    return (out.reshape(batch, seq, D_MODEL), heads(zk[..., WIDTH:2 * WIDTH]), heads(zk[..., 2 * WIDTH:]), wkv,
            zb.reshape(batch, seq, B_COLS)[:, -1:])


def kernel(x_prompt, x_sample, cache_attn_k, cache_attn_v, state_wkv, state_shift, norm1_g, w_in, rel_bias, shift_mu,
           w_decay0, w_decay_up, a0, w_a_up, w_g_up, k_k, k_a, r_k, lnx_g, lnx_b, w_proj_a, w_proj_b, w_out, norm2_g,
           w_query, sub_keys, expert_u, expert_v, normf_g):
    assert norm1_g.shape[0] == 1, "single-layer step"
    zeros = jnp.zeros((DECAY_RANK, WIDTH), F32)
    score_w = _score_weights(w_query[0], sub_keys[0])
    score_hi = score_w.astype(BF16)
    w = dict(
        norm1_g=norm1_g[0], w_in=w_in[0].astype(BF16), rel_bias=rel_bias[0],
        shift_mu=_row_vec(shift_mu[0]), w_decay0=_row_vec(w_decay0[0]), a0=_row_vec(a0[0]),
        w_decay_up=jnp.concatenate([w_decay_up[0], zeros], axis=0), w_a_up=jnp.concatenate([zeros, w_a_up[0]], axis=0),
        w_g_up=w_g_up[0], k_k=_row_vec(k_k[0]), k_a=_row_vec(k_a[0]), r_k=_row_vec(r_k[0]),
        head_sum=_head_sum_matrix(),
        lnx_g=_row_vec(lnx_g[0]), lnx_b=_row_vec(lnx_b[0]),
        w_proj_a=w_proj_a[0].astype(BF16), w_proj_b=w_proj_b[0].astype(BF16), w_out=w_out[0].astype(BF16),
        norm2_g=_row_vec(norm2_g[0]), score_hi=score_hi, score_lo=(score_w - score_hi.astype(F32)).astype(BF16),
        expert_u=_tile_table(expert_u[0]), expert_v=_tile_table(expert_v[0]),
        normf_g=_row_vec(normf_g))

    batch = x_prompt.shape[0]
    yp, kp, vp, wp, sp = _stream_step(x_prompt, None, None, jnp.zeros((batch, HEADS, HEAD_DIM, HEAD_DIM), F32),
                                      jnp.zeros((batch, 1, B_COLS), F32), w)
    ys, ks, vs, ws, ss = _stream_step(x_sample, cache_attn_k[0], cache_attn_v[0], state_wkv[0], state_shift[0], w)
    return (yp, ys, kp[None], vp[None], wp[None], sp[None], ks[None], vs[None], ws[None], ss[None])
```

```python
import functools

import jax
import jax.numpy as jnp
from jax import lax
from jax.experimental import pallas as pl
from jax.experimental.pallas import tpu as pltpu

F32 = jnp.float32
BF16 = jnp.bfloat16
HIGHEST = lax.Precision.HIGHEST

D_MODEL = 1024
CHUNK = 64
BAND_CHUNKS = 8
HEADS = 8
HEAD_DIM = 64
WIDTH = HEADS * HEAD_DIM
REL_CLIP = 128
DECAY_RANK = 64
AAA_RANK = 64
GATE_RANK = 128
LNX_EPS = 64e-5
NORM_EPS = 1e-6
P_HEADS = 8
N_KEYS = 128
P_HALF = 64
P_TOPK = 16
A_COLS = 3 * WIDTH
B_COLS = 3 * WIDTH + DECAY_RANK + AAA_RANK + GATE_RANK
G_COLS = 2 * D_MODEL
NEG = -1e30

VMEM_LIMIT = 56 * 1024 * 1024
ROW_TILE = 256
ATT_TILE = BAND_CHUNKS * CHUNK
BAND_KEYS = (BAND_CHUNKS + 1) * CHUNK
SCAN_TILE = 128
SCAN_BATCH = 2
TOK_TILE = 128
SHIFT_GROUP = 32
PAIRS = HEADS // 2
PAIR_W = 2 * HEAD_DIM
N_SEL = P_HEADS * P_TOPK
ROW_SHAPE = (8, D_MODEL // 8)
TILE_WORDS = ROW_SHAPE[0] // 2


def _params(**kw):
    return pltpu.CompilerParams(vmem_limit_bytes=VMEM_LIMIT, **kw)


def _resident(shape):
    nd = len(shape)
    return pl.BlockSpec(shape, lambda *_: (0,) * nd, pipeline_mode=pl.Buffered(1))


def _sigmoid(x):
    return 1.0 / (1.0 + jnp.exp(-x))


def _dot(a, b, **kw):
    return jnp.dot(a, b, preferred_element_type=F32, **kw)


def _dot_nt(a, b, **kw):
    return lax.dot_general(a, b, (((1,), (1,)), ((), ())), preferred_element_type=F32, **kw)


def _row_vec(a):
    return a.reshape(1, -1).astype(F32)


def _split_sums(x, ones_bf16):
    hi = x.astype(BF16)
    lo = (x - hi.astype(F32)).astype(BF16)
    return _dot(hi, ones_bf16) + _dot(lo, ones_bf16)


def _inproj_kernel(x_ref, g_ref, w_ref, za_ref, zb_ref, zg_ref):
    x = x_ref[...]
    y = x * lax.rsqrt(jnp.mean(x * x, axis=-1, keepdims=True) + NORM_EPS) * g_ref[...]
    yb = y.astype(BF16)
    za_ref[...] = _dot(yb, w_ref[:, :A_COLS])
    zb_ref[...] = _dot(yb, w_ref[:, A_COLS:A_COLS + B_COLS])
    zg_ref[...] = _dot(yb, w_ref[:, A_COLS + B_COLS:])


def _in_proj(x, norm_g, w_in_bf16):
    m = x.shape[0]
    in_cols = w_in_bf16.shape[1]
    row = lambda i: (i, 0)
    return pl.pallas_call(
        _inproj_kernel,
        grid=(m // ROW_TILE,),
        in_specs=[pl.BlockSpec((ROW_TILE, D_MODEL), row), _resident((1, D_MODEL)), _resident((D_MODEL, in_cols))],
        out_specs=[pl.BlockSpec((ROW_TILE, A_COLS), row), pl.BlockSpec((ROW_TILE, B_COLS), row),
                   pl.BlockSpec((ROW_TILE, G_COLS), row)],
        out_shape=[jax.ShapeDtypeStruct((m, A_COLS), F32), jax.ShapeDtypeStruct((m, B_COLS), F32),
                   jax.ShapeDtypeStruct((m, G_COLS), F32)],
        compiler_params=_params(dimension_semantics=("parallel",)),
        name="in_proj",
    )(x, _row_vec(norm_g), w_in_bf16)


def _softmax_pv(scores, values):
    m = scores[0].max(axis=-1, keepdims=True)
    for s in scores[1:]:
        m = jnp.maximum(m, s.max(axis=-1, keepdims=True))
    acc, den = None, None
    for s, v in zip(scores, values):
        p = jnp.exp(s - m)
        d = p.sum(axis=-1, keepdims=True)
        o = _dot(p.astype(BF16), v)
        acc = o if acc is None else acc + o
        den = d if den is None else den + d
    return acc / den


def _attn_prompt_kernel(q_ref, kp_ref, kc_ref, vp_ref, vc_ref, bias_ref, o_ref):
    first = pl.program_id(1) == 0
    scale = HEAD_DIM ** -0.5
    for h in range(HEADS):
        sl = slice(h * HEAD_DIM, (h + 1) * HEAD_DIM)
        q = q_ref[:, sl].astype(BF16)
        s_prev = _dot_nt(q, kp_ref[:, sl].astype(BF16)) * scale + bias_ref[h, :, :ATT_TILE]
        s_cur = _dot_nt(q, kc_ref[:, sl].astype(BF16)) * scale + bias_ref[h, :, ATT_TILE:]
        s_prev = jnp.where(first, NEG, s_prev)
        o_ref[:, sl] = _softmax_pv([s_prev, s_cur], [vp_ref[:, sl].astype(BF16), vc_ref[:, sl].astype(BF16)])


def _prompt_bias_table(rel_bias):
    near = BAND_KEYS - (REL_CLIP // CHUNK + 1) * CHUNK
    dist = jnp.arange(CHUNK)[:, None] + BAND_CHUNKS * CHUNK - jnp.arange(near, BAND_KEYS)[None, :]
    varying = rel_bias[:, jnp.clip(dist, -REL_CLIP, REL_CLIP) + REL_CLIP]
    far = jnp.broadcast_to(rel_bias[:, -1][:, None, None], (HEADS, CHUNK, near))
    window = jnp.concatenate([far, varying], axis=2).astype(F32)
    rows = [jnp.pad(window, ((0, 0), (0, 0), (c * CHUNK, 2 * ATT_TILE - BAND_KEYS - c * CHUNK)), constant_values=NEG)
            for c in range(BAND_CHUNKS)]
    return jnp.concatenate(rows, axis=1)


def _attn_prompt(za, rel_bias, batch, seq):
    nb = seq // ATT_TILE
    blk = (ATT_TILE, WIDTH)
    cur = lambda col: (lambda b, i: (b * nb + i, col))
    prev = lambda col: (lambda b, i: (b * nb + jnp.maximum(i - 1, 0), col))
    return pl.pallas_call(
        _attn_prompt_kernel,
        grid=(batch, nb),
        in_specs=[pl.BlockSpec(blk, cur(0)), pl.BlockSpec(blk, prev(1)), pl.BlockSpec(blk, cur(1)),
                  pl.BlockSpec(blk, prev(2)), pl.BlockSpec(blk, cur(2)),
                  _resident((HEADS, ATT_TILE, 2 * ATT_TILE))],
        out_specs=pl.BlockSpec(blk, cur(0)),
        out_shape=jax.ShapeDtypeStruct((batch * seq, WIDTH), F32),
        compiler_params=_params(dimension_semantics=("parallel", "arbitrary")),
        name="attn_prompt",
    )(za, za, za, za, za, _prompt_bias_table(rel_bias))


def _attn_sample_kernel(q_ref, kn_ref, vn_ref, kc_ref, vc_ref, bc_ref, bn_ref, o_ref):
    scale = HEAD_DIM ** -0.5
    for h in range(HEADS):
        sl = slice(h * HEAD_DIM, (h + 1) * HEAD_DIM)
        q = q_ref[:, sl].astype(BF16)
        s_cache = _dot_nt(q, kc_ref[0, h].astype(BF16)) * scale + bc_ref[h]
        s_new = _dot_nt(q, kn_ref[:, sl].astype(BF16)) * scale + bn_ref[h]
        o_ref[:, sl] = _softmax_pv([s_cache, s_new], [vc_ref[0, h].astype(BF16), vn_ref[:, sl].astype(BF16)])


def _attn_sample(za, k_cache, v_cache, rel_bias, seq):
    nb, _, past, _ = k_cache.shape
    near = max(past - REL_CLIP, 0)
    dist = jnp.arange(seq)[:, None] + past - jnp.arange(near, past + seq)[None, :]
    far = jnp.broadcast_to(rel_bias[:, -1][:, None, None], (HEADS, seq, near))
    bias = jnp.concatenate([far, rel_bias[:, jnp.clip(dist, -REL_CLIP, REL_CLIP) + REL_CLIP]], axis=2).astype(F32)
    blk = (seq, WIDTH)
    rows = lambda col: (lambda b: (b, col))
    cache = pl.BlockSpec((1, HEADS, past, HEAD_DIM), lambda b: (b, 0, 0, 0))
    return pl.pallas_call(
        _attn_sample_kernel,
        grid=(nb,),
        in_specs=[pl.BlockSpec(blk, rows(0)), pl.BlockSpec(blk, rows(1)), pl.BlockSpec(blk, rows(2)), cache, cache,
                  _resident((HEADS, seq, past)), _resident((HEADS, seq, seq))],
        out_specs=pl.BlockSpec(blk, rows(0)),
        out_shape=jax.ShapeDtypeStruct((nb * seq, WIDTH), F32),
        compiler_params=_params(dimension_semantics=("parallel",)),
        name="attn_sample",
    )(za, za, za, k_cache, v_cache, bias[:, :, :past], bias[:, :, past:])


def _softplus(x):
    return jnp.maximum(x, 0.0) + jnp.log(1.0 + jnp.exp(-jnp.abs(x)))


def _rwkv_prep_kernel(zb_ref, prev_ref, shift_ref, mu_ref, wd0_ref, wdu_ref, a0_ref, wau_ref, wgu_ref, kk_ref, ka_ref,
                      rk_ref, gsum_ref, r_o, w_o, k_o, v_o, a_o, b_o, g_o, bonus_o, *, seq):
    i = pl.program_id(0)
    zb = zb_ref[...]
    tm = zb.shape[0]
    row = lax.broadcasted_iota(jnp.int32, zb.shape, 0)
    prev = jnp.where(row == 0, prev_ref[7:8, :], pltpu.roll(zb, 1, 0))
    ngrp = tm // SHIFT_GROUP
    shift = jnp.broadcast_to(shift_ref[...][:, None, :], (ngrp, SHIFT_GROUP, B_COLS)).reshape(tm, B_COLS)
    prev = jnp.where(lax.rem(i * tm + row, seq) == 0, shift, prev)
    zm = zb + (prev - zb) * mu_ref[...]
    r = zm[:, 0:WIDTH]
    k = zm[:, WIDTH:2 * WIDTH]
    v = zm[:, 2 * WIDTH:3 * WIDTH]
    lora_in = zm[:, 3 * WIDTH:3 * WIDTH + DECAY_RANK + AAA_RANK]
    gate_in = zm[:, 3 * WIDTH + DECAY_RANK + AAA_RANK:]
    w_log = -_softplus(-(wd0_ref[...] + _dot(jnp.tanh(lora_in), wdu_ref[...], precision=HIGHEST))) - 0.5
    decay = jnp.exp(-jnp.exp(w_log))
    a = _sigmoid(a0_ref[...] + _dot(lora_in, wau_ref[...], precision=HIGHEST))
    g = _dot(_sigmoid(gate_in), wgu_ref[...], precision=HIGHEST)
    kk = k * kk_ref[...]
    kk = kk / jnp.maximum(jnp.sqrt(_split_sums(kk * kk, gsum_ref[...])), 1e-12)
    kmod = k * (1.0 + (a - 1.0) * ka_ref[...])
    for ref, val in ((r_o, r), (w_o, decay), (k_o, kmod), (v_o, v), (a_o, -kk), (b_o, kk * a)):
        for pair in range(PAIRS):
            ref[pair] = val[:, pair * PAIR_W:(pair + 1) * PAIR_W].reshape(ref.shape[1:])
    g_o[...] = g
    bonus_o[...] = _split_sums(r * kmod * rk_ref[...], gsum_ref[...]) * v


def _head_sum_matrix():
    h = jnp.arange(WIDTH) // HEAD_DIM
    return jnp.where(h[:, None] == h[None, :], 1.0, 0.0).astype(BF16)


def _scan_layout(batch, seq):
    steps = min(seq, SCAN_TILE)
    shape = (PAIRS, seq // steps, batch, steps, PAIR_W)
    if seq >= ROW_TILE:
        per_seq = seq // ROW_TILE
        block = (PAIRS, ROW_TILE // steps, 1, steps, PAIR_W)
        index = lambda i: (0, i % per_seq, i // per_seq, 0, 0)
    else:
        block = (PAIRS, 1, ROW_TILE // seq, steps, PAIR_W)
        index = lambda i: (0, 0, i, 0, 0)
    return steps, shape, pl.BlockSpec(block, index)


def _rwkv_prep(zb, shift0, w, batch, seq):
    m = zb.shape[0]
    tm = ROW_TILE
    row = lambda i: (i, 0)
    groups_per_seq = seq // SHIFT_GROUP
    shift_rows = jnp.zeros((batch, groups_per_seq, B_COLS), F32).at[:, 0].set(shift0.reshape(batch, B_COLS))
    _, scan_shape, scan_spec = _scan_layout(batch, seq)
    flat = jax.ShapeDtypeStruct((m, WIDTH), F32)
    return pl.pallas_call(
        functools.partial(_rwkv_prep_kernel, seq=seq),
        grid=(m // tm,),
        in_specs=[pl.BlockSpec((tm, B_COLS), row),
                  pl.BlockSpec((8, B_COLS), lambda i: (jnp.maximum(i * (tm // 8) - 1, 0), 0)),
                  pl.BlockSpec((tm // SHIFT_GROUP, B_COLS), row),
                  _resident((1, B_COLS)), _resident((1, WIDTH)), _resident((DECAY_RANK + AAA_RANK, WIDTH)),
                  _resident((1, WIDTH)), _resident((DECAY_RANK + AAA_RANK, WIDTH)), _resident((GATE_RANK, WIDTH)),
                  _resident((1, WIDTH)), _resident((1, WIDTH)), _resident((1, WIDTH)), _resident((WIDTH, WIDTH))],
        out_specs=[scan_spec] * 6 + [pl.BlockSpec((tm, WIDTH), row)] * 2,
        out_shape=[jax.ShapeDtypeStruct(scan_shape, F32)] * 6 + [flat] * 2,
        compiler_params=_params(dimension_semantics=("parallel",)),
        name="rwkv_prep",
    )(zb, zb, shift_rows.reshape(m // SHIFT_GROUP, B_COLS), w["shift_mu"], w["w_decay0"], w["w_decay_up"], w["a0"],
      w["w_a_up"], w["w_g_up"], w["k_k"], w["k_a"], w["r_k"], w["head_sum"])


def _scan_kernel(r_ref, w_ref, k_ref, v_ref, a_ref, b_ref, s0_ref, y_ref, st_ref, s_scr, *, steps):
    tb = pl.program_id(1)
    chains = [(b, p) for b in range(s_scr.shape[0]) for p in range(PAIRS)]

    @pl.when(tb == 0)
    def _():
        s_scr[...] = s0_ref[...]

    lane = lax.broadcasted_iota(jnp.int32, (HEAD_DIM, PAIR_W), 1)
    sub = lax.broadcasted_iota(jnp.int32, (HEAD_DIM, PAIR_W), 0)
    lo = lane < HEAD_DIM
    diag = (lane & (HEAD_DIM - 1)) == sub
    sub8 = lax.broadcasted_iota(jnp.int32, (8, PAIR_W), 0)

    def head_sums(x):
        s_lo = jnp.sum(jnp.where(lo, x, 0.0), axis=1, keepdims=True)
        s_hi = jnp.sum(jnp.where(lo, 0.0, x), axis=1, keepdims=True)
        return jnp.where(lo, s_lo, s_hi)

    def row_of(ref, chain, t):
        b, p = chain
        tile8, j = t
        return ref[p, 0, b, pl.ds(pl.multiple_of(tile8 * 8, 8), 8), :][j:j + 1, :]

    same_head = ((lax.broadcasted_iota(jnp.int32, (PAIR_W, PAIR_W), 0) < HEAD_DIM)
                 == (lax.broadcasted_iota(jnp.int32, (PAIR_W, PAIR_W), 1) < HEAD_DIM))
    head_ones = jnp.where(same_head, 1.0, 0.0).astype(BF16)

    def head_sums_mxu(x):
        hi = x.astype(BF16)
        lo = (x - hi.astype(F32)).astype(BF16)
        sums = _dot(jnp.concatenate([hi, lo], axis=0), head_ones)
        return sums[:HEAD_DIM] + sums[HEAD_DIM:]

    def emit_y(chain, t, s):
        b, p = chain
        y_col = head_sums_mxu(s * row_of(r_ref, chain, t))
        y_ref[p, 0, b, pl.ds(t[0] * 8 + t[1], 1), :] = jnp.sum(jnp.where(diag, y_col, 0.0), axis=0, keepdims=True)

    def update(chain, t, s):
        sums = head_sums_mxu if chain[1] else head_sums
        sa = sums(s * row_of(a_ref, chain, t))
        v_col = head_sums(jnp.where(diag, row_of(v_ref, chain, t), 0.0))
        return s * row_of(w_ref, chain, t) + sa * row_of(b_ref, chain, t) + v_col * row_of(k_ref, chain, t)

    def step(tile8, j):
        prev = (tile8, j - 1) if j else (tile8 - 1, 7)
        for b, p in chains:
            s = s_scr[b, p]
            if not (isinstance(tile8, int) and tile8 == 0 and j == 0):
                emit_y((b, p), prev, s)
            s_scr[b, p] = update((b, p), (tile8, j), s)

    def trip(tile8, carry):
        for j in range(8):
            step(tile8, j)
        return carry

    trip(0, 0)
    lax.fori_loop(1, steps // 8, trip, 0)
    for b, p in chains:
        emit_y((b, p), (steps // 8 - 1, 7), s_scr[b, p])

    @pl.when(tb == pl.num_programs(1) - 1)
    def _():
        st_ref[...] = s_scr[...]


def _pair_state(s):
    b = s.shape[0]
    return s.reshape(b, PAIRS, 2, HEAD_DIM, HEAD_DIM).transpose(0, 1, 3, 2, 4).reshape(b, PAIRS, HEAD_DIM, PAIR_W)


def _unpair_state(s):
    b = s.shape[0]
    return s.reshape(b, PAIRS, HEAD_DIM, 2, HEAD_DIM).transpose(0, 1, 3, 2, 4).reshape(b, HEADS, HEAD_DIM, HEAD_DIM)


def _wkv_scan(r, w, k, v, a, b, s0, batch, seq):
    steps, shape, _ = _scan_layout(batch, seq)
    bg = SCAN_BATCH
    blk = pl.BlockSpec((PAIRS, 1, bg, steps, PAIR_W), lambda bi, ti: (0, ti, bi, 0, 0))
    state = pl.BlockSpec((bg, PAIRS, HEAD_DIM, PAIR_W), lambda bi, ti: (bi, 0, 0, 0))
    y, st = pl.pallas_call(
        functools.partial(_scan_kernel, steps=steps),
        grid=(batch // bg, seq // steps),
        in_specs=[blk] * 6 + [state],
        out_specs=[blk, state],
        out_shape=[jax.ShapeDtypeStruct(shape, F32), jax.ShapeDtypeStruct((batch, PAIRS, HEAD_DIM, PAIR_W), F32)],
        scratch_shapes=[pltpu.VMEM((bg, PAIRS, HEAD_DIM, PAIR_W), F32)],
        compiler_params=_params(dimension_semantics=("parallel", "arbitrary")),
        name="wkv_scan",
    )(r, w, k, v, a, b, _pair_state(s0.astype(F32)))
    return y, _unpair_state(st)


def _postmix_kernel(x_ref, oa_ref, y_ref, bonus_ref, g_ref, zg_ref, lng_ref, lnb_ref, gsum_ref, wpa_ref, wpb_ref,
                    wout_ref, n2g_ref, wch_ref, wcl_ref, x1_ref, hn_ref, eidx_ref, gate_ref, sc_scr, eidx_scr):
    tm = x_ref.shape[0]
    y = jnp.concatenate([y_ref[pair].reshape(tm, PAIR_W) for pair in range(PAIRS)], axis=1)
    mu = _split_sums(y, gsum_ref[...]) * (1.0 / HEAD_DIM)
    d = y - mu
    var = _split_sums(d * d, gsum_ref[...]) * (1.0 / HEAD_DIM)
    yn = d * lax.rsqrt(var + LNX_EPS) * lng_ref[...] + lnb_ref[...]
    ob = (yn + bonus_ref[...]) * g_ref[...]
    pa = _dot(oa_ref[...].astype(BF16), wpa_ref[...])
    pb = _dot(ob.astype(BF16), wpb_ref[...])
    merged = _sigmoid(zg_ref[:, :D_MODEL]) * pa + _sigmoid(zg_ref[:, D_MODEL:]) * pb
    x1 = x_ref[...] + _dot(merged.astype(BF16), wout_ref[...])
    hn = x1 * lax.rsqrt(jnp.mean(x1 * x1, axis=-1, keepdims=True) + NORM_EPS) * n2g_ref[...]
    for s in range(ROW_SHAPE[0]):
        cols = slice(s * ROW_SHAPE[1], (s + 1) * ROW_SHAPE[1])
        x1_ref[:, s, :] = x1[:, cols]
        hn_ref[:, s, :] = hn[:, cols]
    hh = hn.astype(BF16)
    hl = (hn - hh.astype(F32)).astype(BF16)
    sc_scr[...] = _dot_nt(wch_ref[...], hh) + (_dot_nt(wcl_ref[...], hh) + _dot_nt(wch_ref[...], hl))
    _retrieve(sc_scr, eidx_scr, gate_ref)
    eidx_ref[...] = eidx_scr[...].T


def _post_mix(x, o_a, y, bonus, g, zg, w, batch, seq):
    m = x.shape[0]
    tm = ROW_TILE
    row = lambda i: (i, 0)
    nsc = w["score_hi"].shape[0]
    wide = pl.BlockSpec((tm, D_MODEL), row)
    half = pl.BlockSpec((tm, WIDTH), row)
    tiles = pl.BlockSpec((tm, *ROW_SHAPE), lambda i: (i, 0, 0))
    picks = pl.BlockSpec((N_SEL, tm), lambda i: (0, i))
    _, _, scan_spec = _scan_layout(batch, seq)
    return pl.pallas_call(
        _postmix_kernel,
        grid=(m // tm,),
        in_specs=[wide, half, scan_spec, half, half, pl.BlockSpec((tm, G_COLS), row),
                  _resident((1, WIDTH)), _resident((1, WIDTH)), _resident((WIDTH, WIDTH)),
                  _resident((WIDTH, D_MODEL)), _resident((WIDTH, D_MODEL)), _resident((D_MODEL, D_MODEL)),
                  _resident((1, D_MODEL)), _resident((nsc, D_MODEL)), _resident((nsc, D_MODEL))],
        out_specs=[tiles, tiles, pl.BlockSpec((tm, N_SEL), row), picks],
        out_shape=[jax.ShapeDtypeStruct((m, *ROW_SHAPE), F32), jax.ShapeDtypeStruct((m, *ROW_SHAPE), F32),
                   jax.ShapeDtypeStruct((m, N_SEL), jnp.int32), jax.ShapeDtypeStruct((N_SEL, m), F32)],
        scratch_shapes=[pltpu.VMEM((nsc, tm), F32), pltpu.VMEM((N_SEL, tm), jnp.int32)],
        compiler_params=_params(dimension_semantics=("parallel",)),
        name="post_mix",
    )(x, o_a, y, bonus, g, zg, w["lnx_g"], w["lnx_b"], w["head_sum"], w["w_proj_a"], w["w_proj_b"], w["w_out"],
      w["norm2_g"], w["score_hi"], w["score_lo"])


def _score_weight_kernel(wq_ref, sk_ref, o_ref):
    for c in range(2):
        wq = wq_ref[:, c * P_HALF:(c + 1) * P_HALF]
        o_ref[c * N_KEYS:(c + 1) * N_KEYS, :] = _dot_nt(sk_ref[0, c], wq, precision=HIGHEST)


def _score_weights(w_query, sub_keys):
    return pl.pallas_call(
        _score_weight_kernel,
        grid=(P_HEADS,),
        in_specs=[pl.BlockSpec((D_MODEL, 2 * P_HALF), lambda h: (0, h)),
                  pl.BlockSpec((1, 2, N_KEYS, P_HALF), lambda h: (h, 0, 0, 0))],
        out_specs=pl.BlockSpec((2 * N_KEYS, D_MODEL), lambda h: (h, 0)),
        out_shape=jax.ShapeDtypeStruct((P_HEADS * 2 * N_KEYS, D_MODEL), F32),
        compiler_params=_params(dimension_semantics=("parallel",)),
        name="score_weights",
    )(w_query, sub_keys)


N_CAND = P_TOPK + 7 * 8 + 8


def _extract_topk(vals, ids, n, payload=None):
    top_v, top_i = [], []
    big = jnp.int32(2 ** 30)
    for _ in range(n):
        m = jnp.max(vals, axis=0, keepdims=True)
        pick = jnp.min(jnp.where(vals == m, ids, big), axis=0, keepdims=True)
        sel = ids == pick
        top_v.append(m)
        top_i.append(pick if payload is None else jnp.max(jnp.where(sel, payload, -1), axis=0, keepdims=True))
        vals = jnp.where(sel, -jnp.inf, vals)
    return top_v, top_i


def _retrieve(sc_ref, eidx_ref, gate_ref):
    lanes = sc_ref.shape[1]
    key_id = lax.broadcasted_iota(jnp.int32, (N_KEYS, lanes), 0)
    sub16 = lax.broadcasted_iota(jnp.int32, (P_TOPK, lanes), 0)
    sub8 = lax.broadcasted_iota(jnp.int32, (8, lanes), 0)
    cand_id = jnp.concatenate([sub16] + [a * P_TOPK + sub8 for a in range(1, 8)] + [(8 + sub8) * P_TOPK], axis=0)

    def head(h, carry):
        base = pl.multiple_of(h * 2 * N_KEYS, 2 * N_KEYS)
        v0, i0 = _extract_topk(sc_ref[pl.ds(base, N_KEYS), :], key_id, P_TOPK)
        v1, i1 = _extract_topk(sc_ref[pl.ds(base + N_KEYS, N_KEYS), :], key_id, P_TOPK)
        v1_16, i1_16 = jnp.concatenate(v1, axis=0), jnp.concatenate(i1, axis=0)
        v1_8, i1_8 = jnp.concatenate(v1[:8], axis=0), jnp.concatenate(i1[:8], axis=0)
        cand = jnp.concatenate([v0[0] + v1_16] + [v0[a] + v1_8 for a in range(1, 8)]
                               + [jnp.concatenate(v0[8:], axis=0) + v1[0]], axis=0)
        cidx = jnp.concatenate([i0[0] * N_KEYS + i1_16] + [i0[a] * N_KEYS + i1_8 for a in range(1, 8)]
                               + [jnp.concatenate(i0[8:], axis=0) * N_KEYS + i1[0]], axis=0) * TILE_WORDS
        fv, fe = _extract_topk(cand, cand_id, P_TOPK, payload=cidx)
        fv = jnp.concatenate(fv, axis=0)
        e = jnp.exp(fv - fv[0:1])
        out = pl.multiple_of(h * P_TOPK, P_TOPK)
        gate_ref[pl.ds(out, P_TOPK), :] = e / jnp.sum(e, axis=0, keepdims=True)
        eidx_ref[pl.ds(out, P_TOPK), :] = jnp.concatenate(fe, axis=0)
        return carry

    lax.fori_loop(0, P_HEADS, head, 0)


def _tile_table_kernel(t_ref, o_ref):
    rows = t_ref.shape[0]
    lanes = ROW_SHAPE[1]
    for r in range(TILE_WORDS):
        lo = t_ref[:, (2 * r) * lanes:(2 * r + 1) * lanes].astype(BF16).astype(F32)
        hi = t_ref[:, (2 * r + 1) * lanes:(2 * r + 2) * lanes].astype(BF16).astype(F32)
        word = (pltpu.bitcast(lo, jnp.uint32) >> 16) | (pltpu.bitcast(hi, jnp.uint32) & jnp.uint32(0xFFFF0000))
        o_ref[pl.ds(r, rows, stride=TILE_WORDS), :] = word


def _tile_table(t):
    e = t.shape[0]
    rows = ROW_TILE
    return pl.pallas_call(
        _tile_table_kernel,
        grid=(e // rows,),
        in_specs=[pl.BlockSpec((rows, D_MODEL), lambda i: (i, 0))],
        out_specs=pl.BlockSpec((rows * TILE_WORDS, ROW_SHAPE[1]), lambda i: (i, 0)),
        out_shape=jax.ShapeDtypeStruct((e * TILE_WORDS, ROW_SHAPE[1]), jnp.uint32),
        compiler_params=_params(dimension_semantics=("parallel",)),
        name="tile_table",
    )(t)


def _expert_row(tbl_ref, word_row):
    words = tbl_ref[pl.ds(pl.multiple_of(word_row, TILE_WORDS), TILE_WORDS), :]
    return pltpu.bitcast(words, BF16).astype(F32)


def _sublane_fold(x, y, step, mask):
    if step == 4:
        return jnp.where(mask, x, y) + pltpu.roll(jnp.where(mask, y, x), 4, 0)
    return jnp.where(mask, x + pltpu.roll(x, 8 - step, 0), y + pltpu.roll(y, step, 0))


def _peer_u_kernel(idx_ref, hn_ref, gate_ref, tbl_ref, coef_ref, part_scr, hid_scr):
    toks = hn_ref.shape[0]
    sub = lax.broadcasted_iota(jnp.int32, ROW_SHAPE, 0)
    m4, m2, m1 = sub < 4, (sub & 3) < 2, (sub & 1) == 0
    lane_tok = lax.broadcasted_iota(jnp.int32, (N_SEL, toks), 1)
    part_scr[...] = jnp.zeros_like(part_scr)
    hid_scr[...] = jnp.zeros_like(hid_scr)

    def finish(t, slot):
        hid_scr[...] = jnp.where(lane_tok == t, jnp.sum(part_scr[slot], axis=1, keepdims=True), hid_scr[...])

    def token(t, carry):
        slot = t % 2
        x = hn_ref[t]
        finish(t - 1, 1 - slot)
        for h in range(N_SEL // 8):
            prod = [_expert_row(tbl_ref, idx_ref[t * N_SEL + h * 8 + j]) * x for j in range(8)]
            z = [_sublane_fold(prod[a], prod[a + 4], 4, m4) for a in (0, 2, 1, 3)]
            w0 = _sublane_fold(z[0], z[1], 2, m2)
            w1 = _sublane_fold(z[2], z[3], 2, m2)
            part_scr[slot, h * 8:(h + 1) * 8, :] = _sublane_fold(w0, w1, 1, m1)
        return carry

    lax.fori_loop(0, toks, token, 0)
    finish(toks - 1, (toks - 1) % 2)
    hid = hid_scr[...]
    coef_ref[...] = gate_ref[...] * (0.5 * hid * (1.0 + lax.erf(hid * (2.0 ** -0.5))))


def _peer_u(eidx_flat, hn3, gate, table):
    m = hn3.shape[0]
    tt = TOK_TILE
    return pl.pallas_call(
        _peer_u_kernel,
        grid=(m // tt,),
        in_specs=[pl.BlockSpec((tt * N_SEL,), lambda i: (i,), memory_space=pltpu.SMEM),
                  pl.BlockSpec((tt, *ROW_SHAPE), lambda i: (i, 0, 0)),
                  pl.BlockSpec((N_SEL, tt), lambda i: (0, i)),
                  _resident(table.shape)],
        out_specs=pl.BlockSpec((N_SEL, tt), lambda i: (0, i)),
        out_shape=jax.ShapeDtypeStruct((N_SEL, m), F32),
        scratch_shapes=[pltpu.VMEM((2, N_SEL, ROW_SHAPE[1]), F32), pltpu.VMEM((N_SEL, tt), F32)],
        compiler_params=_params(dimension_semantics=("parallel",)),
        name="peer_u",
    )(eidx_flat, hn3, gate, table)


def _peer_v_kernel(idx_ref, coef_ref, x_ref, tbl_ref, g_ref, y_ref, splat_scr, x2_scr):
    toks = x_ref.shape[0]
    n_acc = 3
    lane_tok = lax.broadcasted_iota(jnp.int32, (N_SEL, toks), 1)

    def splat(t):
        col = jnp.sum(jnp.where(lane_tok == t, coef_ref[...], 0.0), axis=1, keepdims=True)
        return jnp.broadcast_to(col, (N_SEL, ROW_SHAPE[1]))

    splat_scr[0] = splat(0)

    def token(t, carry):
        slot = t % 2
        nxt = splat(jnp.minimum(t + 1, toks - 1))
        acc = [jnp.zeros(ROW_SHAPE, F32) for _ in range(n_acc)]
        for h in range(N_SEL // 8):
            coef = splat_scr[slot, h * 8:(h + 1) * 8, :]
            for j in range(8):
                row = _expert_row(tbl_ref, idx_ref[t * N_SEL + h * 8 + j])
                acc[j % n_acc] = acc[j % n_acc] + coef[j:j + 1, :] * row
        x2_scr[t] = x_ref[t] + ((acc[0] + acc[1]) + acc[2])
        splat_scr[1 - slot] = nxt
        return carry

    lax.fori_loop(0, toks, token, 0)
    x2 = jnp.concatenate([x2_scr[:, s, :] for s in range(ROW_SHAPE[0])], axis=1)
    ms = jnp.mean(x2 * x2, axis=-1, keepdims=True)
    y_ref[...] = x2 * lax.rsqrt(ms + NORM_EPS) * g_ref[...]


def _peer_v(eidx_flat, coef, x3, table, normf_g):
    m = x3.shape[0]
    tt = TOK_TILE
    tok = pl.BlockSpec((tt, *ROW_SHAPE), lambda i: (i, 0, 0))
    return pl.pallas_call(
        _peer_v_kernel,
        grid=(m // tt,),
        in_specs=[pl.BlockSpec((tt * N_SEL,), lambda i: (i,), memory_space=pltpu.SMEM),
                  pl.BlockSpec((N_SEL, tt), lambda i: (0, i)), tok, _resident(table.shape), _resident((1, D_MODEL))],
        out_specs=pl.BlockSpec((tt, D_MODEL), lambda i: (i, 0)),
        out_shape=jax.ShapeDtypeStruct((m, D_MODEL), F32),
        scratch_shapes=[pltpu.VMEM((2, N_SEL, ROW_SHAPE[1]), F32), pltpu.VMEM((tt, *ROW_SHAPE), F32)],
        compiler_params=_params(dimension_semantics=("parallel",)),
        name="peer_v",
    )(eidx_flat, coef, x3, table, normf_g)


def _stream_step(x, k_cache, v_cache, wkv0, shift0, w):
    batch, seq, _ = x.shape
    m = batch * seq
    assert m % ROW_TILE == 0 and m % TOK_TILE == 0 and batch % SCAN_BATCH == 0 and seq % SHIFT_GROUP == 0
    assert (seq % ROW_TILE == 0 and seq % SCAN_TILE == 0) or ROW_TILE % seq == 0
    za, zb, zg = _in_proj(x.reshape(m, D_MODEL), w["norm1_g"], w["w_in"])
    if k_cache is None:
        assert seq % ATT_TILE == 0
        o_a = _attn_prompt(za, w["rel_bias"], batch, seq)
    else:
        o_a = _attn_sample(za, k_cache, v_cache, w["rel_bias"], seq)
    r, dec, k, v, a, b, g, bonus = _rwkv_prep(zb, shift0, w, batch, seq)
    y, wkv = _wkv_scan(r, dec, k, v, a, b, wkv0, batch, seq)
    x1, hn, eidx, gate = _post_mix(x.reshape(m, D_MODEL), o_a, y, bonus, g, zg, w, batch, seq)
    eidx_flat = eidx.reshape(-1)
    coef = _peer_u(eidx_flat, hn, gate, w["expert_u"])
    out = _peer_v(eidx_flat, coef, x1, w["expert_v"], w["normf_g"])

    keep = min(BAND_CHUNKS * CHUNK, seq) if k_cache is None else seq
    zk = za.reshape(batch, seq, A_COLS)[:, seq - keep:]
    heads = lambda t: t.reshape(batch, keep, HEADS, HEAD_DIM).transpose(0, 2, 1, 3)
    return (out.reshape(batch, seq, D_MODEL), heads(zk[..., WIDTH:2 * WIDTH]), heads(zk[..., 2 * WIDTH:]), wkv,
            zb.reshape(batch, seq, B_COLS)[:, -1:])


def kernel(x_prompt, x_sample, cache_attn_k, cache_attn_v, state_wkv, state_shift, norm1_g, w_in, rel_bias, shift_mu,
           w_decay0, w_decay_up, a0, w_a_up, w_g_up, k_k, k_a, r_k, lnx_g, lnx_b, w_proj_a, w_proj_b, w_out, norm2_g,
           w_query, sub_keys, expert_u, expert_v, normf_g):
    assert norm1_g.shape[0] == 1, "single-layer step"
    zeros = jnp.zeros((DECAY_RANK, WIDTH), F32)
    score_w = _score_weights(w_query[0], sub_keys[0])
    score_hi = score_w.astype(BF16)
    w = dict(
        norm1_g=norm1_g[0], w_in=w_in[0].astype(BF16), rel_bias=rel_bias[0],
        shift_mu=_row_vec(shift_mu[0]), w_decay0=_row_vec(w_decay0[0]), a0=_row_vec(a0[0]),
        w_decay_up=jnp.concatenate([w_decay_up[0], zeros], axis=0), w_a_up=jnp.concatenate([zeros, w_a_up[0]], axis=0),
        w_g_up=w_g_up[0], k_k=_row_vec(k_k[0]), k_a=_row_vec(k_a[0]), r_k=_row_vec(r_k[0]),
        head_sum=_head_sum_matrix(),
        lnx_g=_row_vec(lnx_g[0]), lnx_b=_row_vec(lnx_b[0]),
        w_proj_a=w_proj_a[0].astype(BF16), w_proj_b=w_proj_b[0].astype(BF16), w_out=w_out[0].astype(BF16),
        norm2_g=_row_vec(norm2_g[0]), score_hi=score_hi, score_lo=(score_w - score_hi.astype(F32)).astype(BF16),
        expert_u=_tile_table(expert_u[0]), expert_v=_tile_table(expert_v[0]),
        normf_g=_row_vec(normf_g))

    batch = x_prompt.shape[0]
    yp, kp, vp, wp, sp = _stream_step(x_prompt, None, None, jnp.zeros((batch, HEADS, HEAD_DIM, HEAD_DIM), F32),
                                      jnp.zeros((batch, 1, B_COLS), F32), w)
    ys, ks, vs, ws, ss = _stream_step(x_sample, cache_attn_k[0], cache_attn_v[0], state_wkv[0], state_shift[0], w)
    return (yp, ys, kp[None], vp[None], wp[None], sp[None], ks[None], vs[None], ws[None], ss[None])
```

```python
import functools

import jax
import jax.numpy as jnp
from jax import lax
from jax.experimental import pallas as pl
from jax.experimental.pallas import tpu as pltpu

F32 = jnp.float32
BF16 = jnp.bfloat16
HIGHEST = lax.Precision.HIGHEST

D_MODEL = 1024
CHUNK = 64
BAND_CHUNKS = 8
HEADS = 8
HEAD_DIM = 64
WIDTH = HEADS * HEAD_DIM
REL_CLIP = 128
DECAY_RANK = 64
AAA_RANK = 64
GATE_RANK = 128
LNX_EPS = 64e-5
NORM_EPS = 1e-6
P_HEADS = 8
N_KEYS = 128
P_HALF = 64
P_TOPK = 16
A_COLS = 3 * WIDTH
B_COLS = 3 * WIDTH + DECAY_RANK + AAA_RANK + GATE_RANK
G_COLS = 2 * D_MODEL
NEG = -1e30

VMEM_LIMIT = 56 * 1024 * 1024
ROW_TILE = 256
ATT_TILE = BAND_CHUNKS * CHUNK
BAND_KEYS = (BAND_CHUNKS + 1) * CHUNK
SCAN_TILE = 128
SCAN_BATCH = 2
TOK_TILE = 128
SHIFT_GROUP = 32
PAIRS = HEADS // 2
PAIR_W = 2 * HEAD_DIM
N_SEL = P_HEADS * P_TOPK
ROW_SHAPE = (8, D_MODEL // 8)
TILE_WORDS = ROW_SHAPE[0] // 2


def _params(**kw):
    return pltpu.CompilerParams(vmem_limit_bytes=VMEM_LIMIT, **kw)


def _resident(shape):
    nd = len(shape)
    return pl.BlockSpec(shape, lambda *_: (0,) * nd, pipeline_mode=pl.Buffered(1))


def _sigmoid(x):
    return 1.0 / (1.0 + jnp.exp(-x))


def _dot(a, b, **kw):
    return jnp.dot(a, b, preferred_element_type=F32, **kw)


def _dot_nt(a, b, **kw):
    return lax.dot_general(a, b, (((1,), (1,)), ((), ())), preferred_element_type=F32, **kw)


def _row_vec(a):
    return a.reshape(1, -1).astype(F32)


def _split_sums(x, ones_bf16):
    hi = x.astype(BF16)
    lo = (x - hi.astype(F32)).astype(BF16)
    return _dot(hi, ones_bf16) + _dot(lo, ones_bf16)


def _inproj_kernel(x_ref, g_ref, w_ref, za_ref, zb_ref, zg_ref):
    x = x_ref[...]
    y = x * lax.rsqrt(jnp.mean(x * x, axis=-1, keepdims=True) + NORM_EPS) * g_ref[...]
    yb = y.astype(BF16)
    za_ref[...] = _dot(yb, w_ref[:, :A_COLS])
    zb_ref[...] = _dot(yb, w_ref[:, A_COLS:A_COLS + B_COLS])
    zg_ref[...] = _dot(yb, w_ref[:, A_COLS + B_COLS:])


def _in_proj(x, norm_g, w_in_bf16):
    m = x.shape[0]
    in_cols = w_in_bf16.shape[1]
    row = lambda i: (i, 0)
    return pl.pallas_call(
        _inproj_kernel,
        grid=(m // ROW_TILE,),
        in_specs=[pl.BlockSpec((ROW_TILE, D_MODEL), row), _resident((1, D_MODEL)), _resident((D_MODEL, in_cols))],
        out_specs=[pl.BlockSpec((ROW_TILE, A_COLS), row), pl.BlockSpec((ROW_TILE, B_COLS), row),
                   pl.BlockSpec((ROW_TILE, G_COLS), row)],
        out_shape=[jax.ShapeDtypeStruct((m, A_COLS), F32), jax.ShapeDtypeStruct((m, B_COLS), F32),
                   jax.ShapeDtypeStruct((m, G_COLS), F32)],
        compiler_params=_params(dimension_semantics=("parallel",)),
        name="in_proj",
    )(x, _row_vec(norm_g), w_in_bf16)


def _softmax_pv(scores, values):
    m = scores[0].max(axis=-1, keepdims=True)
    for s in scores[1:]:
        m = jnp.maximum(m, s.max(axis=-1, keepdims=True))
    acc, den = None, None
    for s, v in zip(scores, values):
        p = jnp.exp(s - m)
        d = p.sum(axis=-1, keepdims=True)
        o = _dot(p.astype(BF16), v)
        acc = o if acc is None else acc + o
        den = d if den is None else den + d
    return acc / den


def _attn_prompt_kernel(q_ref, kp_ref, kc_ref, vp_ref, vc_ref, bias_ref, o_ref):
    first = pl.program_id(1) == 0
    scale = HEAD_DIM ** -0.5
    for h in range(HEADS):
        sl = slice(h * HEAD_DIM, (h + 1) * HEAD_DIM)
        q = q_ref[:, sl].astype(BF16)
        s_prev = _dot_nt(q, kp_ref[:, sl].astype(BF16)) * scale + bias_ref[h, :, :ATT_TILE]
        s_cur = _dot_nt(q, kc_ref[:, sl].astype(BF16)) * scale + bias_ref[h, :, ATT_TILE:]
        s_prev = jnp.where(first, NEG, s_prev)
        o_ref[:, sl] = _softmax_pv([s_prev, s_cur], [vp_ref[:, sl].astype(BF16), vc_ref[:, sl].astype(BF16)])


def _prompt_bias_table(rel_bias):
    near = BAND_KEYS - (REL_CLIP // CHUNK + 1) * CHUNK
    dist = jnp.arange(CHUNK)[:, None] + BAND_CHUNKS * CHUNK - jnp.arange(near, BAND_KEYS)[None, :]
    varying = rel_bias[:, jnp.clip(dist, -REL_CLIP, REL_CLIP) + REL_CLIP]
    far = jnp.broadcast_to(rel_bias[:, -1][:, None, None], (HEADS, CHUNK, near))
    window = jnp.concatenate([far, varying], axis=2).astype(F32)
    rows = [jnp.pad(window, ((0, 0), (0, 0), (c * CHUNK, 2 * ATT_TILE - BAND_KEYS - c * CHUNK)), constant_values=NEG)
            for c in range(BAND_CHUNKS)]
    return jnp.concatenate(rows, axis=1)


def _attn_prompt(za, rel_bias, batch, seq):
    nb = seq // ATT_TILE
    blk = (ATT_TILE, WIDTH)
    cur = lambda col: (lambda b, i: (b * nb + i, col))
    prev = lambda col: (lambda b, i: (b * nb + jnp.maximum(i - 1, 0), col))
    return pl.pallas_call(
        _attn_prompt_kernel,
        grid=(batch, nb),
        in_specs=[pl.BlockSpec(blk, cur(0)), pl.BlockSpec(blk, prev(1)), pl.BlockSpec(blk, cur(1)),
                  pl.BlockSpec(blk, prev(2)), pl.BlockSpec(blk, cur(2)),
                  _resident((HEADS, ATT_TILE, 2 * ATT_TILE))],
        out_specs=pl.BlockSpec(blk, cur(0)),
        out_shape=jax.ShapeDtypeStruct((batch * seq, WIDTH), F32),
        compiler_params=_params(dimension_semantics=("parallel", "arbitrary")),
        name="attn_prompt",
    )(za, za, za, za, za, _prompt_bias_table(rel_bias))


def _attn_sample_kernel(q_ref, kn_ref, vn_ref, kc_ref, vc_ref, bc_ref, bn_ref, o_ref):
    scale = HEAD_DIM ** -0.5
    for h in range(HEADS):
        sl = slice(h * HEAD_DIM, (h + 1) * HEAD_DIM)
        q = q_ref[:, sl].astype(BF16)
        s_cache = _dot_nt(q, kc_ref[0, h].astype(BF16)) * scale + bc_ref[h]
        s_new = _dot_nt(q, kn_ref[:, sl].astype(BF16)) * scale + bn_ref[h]
        o_ref[:, sl] = _softmax_pv([s_cache, s_new], [vc_ref[0, h].astype(BF16), vn_ref[:, sl].astype(BF16)])


def _attn_sample(za, k_cache, v_cache, rel_bias, seq):
    nb, _, past, _ = k_cache.shape
    near = max(past - REL_CLIP, 0)
    dist = jnp.arange(seq)[:, None] + past - jnp.arange(near, past + seq)[None, :]
    far = jnp.broadcast_to(rel_bias[:, -1][:, None, None], (HEADS, seq, near))
    bias = jnp.concatenate([far, rel_bias[:, jnp.clip(dist, -REL_CLIP, REL_CLIP) + REL_CLIP]], axis=2).astype(F32)
    blk = (seq, WIDTH)
    rows = lambda col: (lambda b: (b, col))
    cache = pl.BlockSpec((1, HEADS, past, HEAD_DIM), lambda b: (b, 0, 0, 0))
    return pl.pallas_call(
        _attn_sample_kernel,
        grid=(nb,),
        in_specs=[pl.BlockSpec(blk, rows(0)), pl.BlockSpec(blk, rows(1)), pl.BlockSpec(blk, rows(2)), cache, cache,
                  _resident((HEADS, seq, past)), _resident((HEADS, seq, seq))],
        out_specs=pl.BlockSpec(blk, rows(0)),
        out_shape=jax.ShapeDtypeStruct((nb * seq, WIDTH), F32),
        compiler_params=_params(dimension_semantics=("parallel",)),
        name="attn_sample",
    )(za, za, za, k_cache, v_cache, bias[:, :, :past], bias[:, :, past:])


def _softplus(x):
    return jnp.maximum(x, 0.0) + jnp.log(1.0 + jnp.exp(-jnp.abs(x)))


def _rwkv_prep_kernel(zb_ref, prev_ref, shift_ref, mu_ref, wd0_ref, wdu_ref, a0_ref, wau_ref, wgu_ref, kk_ref, ka_ref,
                      rk_ref, gsum_ref, r_o, w_o, k_o, v_o, a_o, b_o, g_o, bonus_o, *, seq):
    i = pl.program_id(0)
    zb = zb_ref[...]
    tm = zb.shape[0]
    row = lax.broadcasted_iota(jnp.int32, zb.shape, 0)
    prev = jnp.where(row == 0, prev_ref[7:8, :], pltpu.roll(zb, 1, 0))
    ngrp = tm // SHIFT_GROUP
    shift = jnp.broadcast_to(shift_ref[...][:, None, :], (ngrp, SHIFT_GROUP, B_COLS)).reshape(tm, B_COLS)
    prev = jnp.where(lax.rem(i * tm + row, seq) == 0, shift, prev)
    zm = zb + (prev - zb) * mu_ref[...]
    r = zm[:, 0:WIDTH]
    k = zm[:, WIDTH:2 * WIDTH]
    v = zm[:, 2 * WIDTH:3 * WIDTH]
    lora_in = zm[:, 3 * WIDTH:3 * WIDTH + DECAY_RANK + AAA_RANK]
    gate_in = zm[:, 3 * WIDTH + DECAY_RANK + AAA_RANK:]
    w_log = -_softplus(-(wd0_ref[...] + _dot(jnp.tanh(lora_in), wdu_ref[...], precision=HIGHEST))) - 0.5
    decay = jnp.exp(-jnp.exp(w_log))
    a = _sigmoid(a0_ref[...] + _dot(lora_in, wau_ref[...], precision=HIGHEST))
    g = _dot(_sigmoid(gate_in), wgu_ref[...], precision=HIGHEST)
    kk = k * kk_ref[...]
    kk = kk / jnp.maximum(jnp.sqrt(_split_sums(kk * kk, gsum_ref[...])), 1e-12)
    kmod = k * (1.0 + (a - 1.0) * ka_ref[...])
    for ref, val in ((r_o, r), (w_o, decay), (k_o, kmod), (v_o, v), (a_o, -kk), (b_o, kk * a)):
        for pair in range(PAIRS):
            ref[pair] = val[:, pair * PAIR_W:(pair + 1) * PAIR_W].reshape(ref.shape[1:])
    g_o[...] = g
    bonus_o[...] = _split_sums(r * kmod * rk_ref[...], gsum_ref[...]) * v


def _head_sum_matrix():
    h = jnp.arange(WIDTH) // HEAD_DIM
    return jnp.where(h[:, None] == h[None, :], 1.0, 0.0).astype(BF16)


def _scan_layout(batch, seq):
    steps = min(seq, SCAN_TILE)
    shape = (PAIRS, seq // steps, batch, steps, PAIR_W)
    if seq >= ROW_TILE:
        per_seq = seq // ROW_TILE
        block = (PAIRS, ROW_TILE // steps, 1, steps, PAIR_W)
        index = lambda i: (0, i % per_seq, i // per_seq, 0, 0)
    else:
        block = (PAIRS, 1, ROW_TILE // seq, steps, PAIR_W)
        index = lambda i: (0, 0, i, 0, 0)
    return steps, shape, pl.BlockSpec(block, index)


def _rwkv_prep(zb, shift0, w, batch, seq):
    m = zb.shape[0]
    tm = ROW_TILE
    row = lambda i: (i, 0)
    groups_per_seq = seq // SHIFT_GROUP
    shift_rows = jnp.zeros((batch, groups_per_seq, B_COLS), F32).at[:, 0].set(shift0.reshape(batch, B_COLS))
    _, scan_shape, scan_spec = _scan_layout(batch, seq)
    flat = jax.ShapeDtypeStruct((m, WIDTH), F32)
    return pl.pallas_call(
        functools.partial(_rwkv_prep_kernel, seq=seq),
        grid=(m // tm,),
        in_specs=[pl.BlockSpec((tm, B_COLS), row),
                  pl.BlockSpec((8, B_COLS), lambda i: (jnp.maximum(i * (tm // 8) - 1, 0), 0)),
                  pl.BlockSpec((tm // SHIFT_GROUP, B_COLS), row),
                  _resident((1, B_COLS)), _resident((1, WIDTH)), _resident((DECAY_RANK + AAA_RANK, WIDTH)),
                  _resident((1, WIDTH)), _resident((DECAY_RANK + AAA_RANK, WIDTH)), _resident((GATE_RANK, WIDTH)),
                  _resident((1, WIDTH)), _resident((1, WIDTH)), _resident((1, WIDTH)), _resident((WIDTH, WIDTH))],
        out_specs=[scan_spec] * 6 + [pl.BlockSpec((tm, WIDTH), row)] * 2,
        out_shape=[jax.ShapeDtypeStruct(scan_shape, F32)] * 6 + [flat] * 2,
        compiler_params=_params(dimension_semantics=("parallel",)),
        name="rwkv_prep",
    )(zb, zb, shift_rows.reshape(m // SHIFT_GROUP, B_COLS), w["shift_mu"], w["w_decay0"], w["w_decay_up"], w["a0"],
      w["w_a_up"], w["w_g_up"], w["k_k"], w["k_a"], w["r_k"], w["head_sum"])


def _scan_kernel(r_ref, w_ref, k_ref, v_ref, a_ref, b_ref, s0_ref, y_ref, st_ref, s_scr, *, steps):
    tb = pl.program_id(1)
    chains = [(b, p) for b in range(s_scr.shape[0]) for p in range(PAIRS)]

    @pl.when(tb == 0)
    def _():
        s_scr[...] = s0_ref[...]

    lane = lax.broadcasted_iota(jnp.int32, (HEAD_DIM, PAIR_W), 1)
    sub = lax.broadcasted_iota(jnp.int32, (HEAD_DIM, PAIR_W), 0)
    lo = lane < HEAD_DIM
    diag = (lane & (HEAD_DIM - 1)) == sub
    sub8 = lax.broadcasted_iota(jnp.int32, (8, PAIR_W), 0)

    def head_sums(x):
        s_lo = jnp.sum(jnp.where(lo, x, 0.0), axis=1, keepdims=True)
        s_hi = jnp.sum(jnp.where(lo, 0.0, x), axis=1, keepdims=True)
        return jnp.where(lo, s_lo, s_hi)

    def row_of(ref, chain, t):
        b, p = chain
        tile8, j = t
        return ref[p, 0, b, pl.ds(pl.multiple_of(tile8 * 8, 8), 8), :][j:j + 1, :]

    same_head = ((lax.broadcasted_iota(jnp.int32, (PAIR_W, PAIR_W), 0) < HEAD_DIM)
                 == (lax.broadcasted_iota(jnp.int32, (PAIR_W, PAIR_W), 1) < HEAD_DIM))
    head_ones = jnp.where(same_head, 1.0, 0.0).astype(BF16)

    def head_sums_mxu(x):
        hi = x.astype(BF16)
        lo = (x - hi.astype(F32)).astype(BF16)
        sums = _dot(jnp.concatenate([hi, lo], axis=0), head_ones)
        return sums[:HEAD_DIM] + sums[HEAD_DIM:]

    def emit_y(chain, t, s):
        b, p = chain
        y_col = head_sums_mxu(s * row_of(r_ref, chain, t))
        y_ref[p, 0, b, pl.ds(t[0] * 8 + t[1], 1), :] = jnp.sum(jnp.where(diag, y_col, 0.0), axis=0, keepdims=True)

    def update(chain, t, s):
        sums = head_sums_mxu if chain[1] else head_sums
        sa = sums(s * row_of(a_ref, chain, t))
        v_col = head_sums(jnp.where(diag, row_of(v_ref, chain, t), 0.0))
        return s * row_of(w_ref, chain, t) + sa * row_of(b_ref, chain, t) + v_col * row_of(k_ref, chain, t)

    def step(tile8, j):
        prev = (tile8, j - 1) if j else (tile8 - 1, 7)
        for b, p in chains:
            s = s_scr[b, p]
            if not (isinstance(tile8, int) and tile8 == 0 and j == 0):
                emit_y((b, p), prev, s)
            s_scr[b, p] = update((b, p), (tile8, j), s)

    def trip(tile8, carry):
        for j in range(8):
            step(tile8, j)
        return carry

    trip(0, 0)
    lax.fori_loop(1, steps // 8, trip, 0)
    for b, p in chains:
        emit_y((b, p), (steps // 8 - 1, 7), s_scr[b, p])

    @pl.when(tb == pl.num_programs(1) - 1)
    def _():
        st_ref[...] = s_scr[...]


def _pair_state(s):
    b = s.shape[0]
    return s.reshape(b, PAIRS, 2, HEAD_DIM, HEAD_DIM).transpose(0, 1, 3, 2, 4).reshape(b, PAIRS, HEAD_DIM, PAIR_W)


def _unpair_state(s):
    b = s.shape[0]
    return s.reshape(b, PAIRS, HEAD_DIM, 2, HEAD_DIM).transpose(0, 1, 3, 2, 4).reshape(b, HEADS, HEAD_DIM, HEAD_DIM)


def _wkv_scan(r, w, k, v, a, b, s0, batch, seq):
    steps, shape, _ = _scan_layout(batch, seq)
    bg = SCAN_BATCH
    blk = pl.BlockSpec((PAIRS, 1, bg, steps, PAIR_W), lambda bi, ti: (0, ti, bi, 0, 0))
    state = pl.BlockSpec((bg, PAIRS, HEAD_DIM, PAIR_W), lambda bi, ti: (bi, 0, 0, 0))
    y, st = pl.pallas_call(
        functools.partial(_scan_kernel, steps=steps),
        grid=(batch // bg, seq // steps),
        in_specs=[blk] * 6 + [state],
        out_specs=[blk, state],
        out_shape=[jax.ShapeDtypeStruct(shape, F32), jax.ShapeDtypeStruct((batch, PAIRS, HEAD_DIM, PAIR_W), F32)],
        scratch_shapes=[pltpu.VMEM((bg, PAIRS, HEAD_DIM, PAIR_W), F32)],
        compiler_params=_params(dimension_semantics=("parallel", "arbitrary")),
        name="wkv_scan",
    )(r, w, k, v, a, b, _pair_state(s0.astype(F32)))
    return y, _unpair_state(st)


def _postmix_kernel(x_ref, oa_ref, y_ref, bonus_ref, g_ref, zg_ref, lng_ref, lnb_ref, gsum_ref, wpa_ref, wpb_ref,
                    wout_ref, n2g_ref, wch_ref, wcl_ref, x1_ref, hn_ref, eidx_ref, gate_ref, sc_scr, eidx_scr):
    tm = x_ref.shape[0]
    y = jnp.concatenate([y_ref[pair].reshape(tm, PAIR_W) for pair in range(PAIRS)], axis=1)
    mu = _split_sums(y, gsum_ref[...]) * (1.0 / HEAD_DIM)
    d = y - mu
    var = _split_sums(d * d, gsum_ref[...]) * (1.0 / HEAD_DIM)
    yn = d * lax.rsqrt(var + LNX_EPS) * lng_ref[...] + lnb_ref[...]
    ob = (yn + bonus_ref[...]) * g_ref[...]
    pa = _dot(oa_ref[...].astype(BF16), wpa_ref[...])
    pb = _dot(ob.astype(BF16), wpb_ref[...])
    merged = _sigmoid(zg_ref[:, :D_MODEL]) * pa + _sigmoid(zg_ref[:, D_MODEL:]) * pb
    x1 = x_ref[...] + _dot(merged.astype(BF16), wout_ref[...])
    hn = x1 * lax.rsqrt(jnp.mean(x1 * x1, axis=-1, keepdims=True) + NORM_EPS) * n2g_ref[...]
    for s in range(ROW_SHAPE[0]):
        cols = slice(s * ROW_SHAPE[1], (s + 1) * ROW_SHAPE[1])
        x1_ref[:, s, :] = x1[:, cols]
        hn_ref[:, s, :] = hn[:, cols]
    hh = hn.astype(BF16)
    hl = (hn - hh.astype(F32)).astype(BF16)
    sc_scr[...] = _dot_nt(wch_ref[...], hh) + (_dot_nt(wcl_ref[...], hh) + _dot_nt(wch_ref[...], hl))
    _retrieve(sc_scr, eidx_scr, gate_ref)
    eidx_ref[...] = eidx_scr[...].T


def _post_mix(x, o_a, y, bonus, g, zg, w, batch, seq):
    m = x.shape[0]
    tm = ROW_TILE
    row = lambda i: (i, 0)
    nsc = w["score_hi"].shape[0]
    wide = pl.BlockSpec((tm, D_MODEL), row)
    half = pl.BlockSpec((tm, WIDTH), row)
    tiles = pl.BlockSpec((tm, *ROW_SHAPE), lambda i: (i, 0, 0))
    picks = pl.BlockSpec((N_SEL, tm), lambda i: (0, i))
    _, _, scan_spec = _scan_layout(batch, seq)
    return pl.pallas_call(
        _postmix_kernel,
        grid=(m // tm,),
        in_specs=[wide, half, scan_spec, half, half, pl.BlockSpec((tm, G_COLS), row),
                  _resident((1, WIDTH)), _resident((1, WIDTH)), _resident((WIDTH, WIDTH)),
                  _resident((WIDTH, D_MODEL)), _resident((WIDTH, D_MODEL)), _resident((D_MODEL, D_MODEL)),
                  _resident((1, D_MODEL)), _resident((nsc, D_MODEL)), _resident((nsc, D_MODEL))],
        out_specs=[tiles, tiles, pl.BlockSpec((tm, N_SEL), row), picks],
        out_shape=[jax.ShapeDtypeStruct((m, *ROW_SHAPE), F32), jax.ShapeDtypeStruct((m, *ROW_SHAPE), F32),
                   jax.ShapeDtypeStruct((m, N_SEL), jnp.int32), jax.ShapeDtypeStruct((N_SEL, m), F32)],
        scratch_shapes=[pltpu.VMEM((nsc, tm), F32), pltpu.VMEM((N_SEL, tm), jnp.int32)],
        compiler_params=_params(dimension_semantics=("parallel",)),
        name="post_mix",
    )(x, o_a, y, bonus, g, zg, w["lnx_g"], w["lnx_b"], w["head_sum"], w["w_proj_a"], w["w_proj_b"], w["w_out"],
      w["norm2_g"], w["score_hi"], w["score_lo"])


def _score_weight_kernel(wq_ref, sk_ref, o_ref):
    for c in range(2):
        wq = wq_ref[:, c * P_HALF:(c + 1) * P_HALF]
        o_ref[c * N_KEYS:(c + 1) * N_KEYS, :] = _dot_nt(sk_ref[0, c], wq, precision=HIGHEST)


def _score_weights(w_query, sub_keys):
    return pl.pallas_call(
        _score_weight_kernel,
        grid=(P_HEADS,),
        in_specs=[pl.BlockSpec((D_MODEL, 2 * P_HALF), lambda h: (0, h)),
                  pl.BlockSpec((1, 2, N_KEYS, P_HALF), lambda h: (h, 0, 0, 0))],
        out_specs=pl.BlockSpec((2 * N_KEYS, D_MODEL), lambda h: (h, 0)),
        out_shape=jax.ShapeDtypeStruct((P_HEADS * 2 * N_KEYS, D_MODEL), F32),
        compiler_params=_params(dimension_semantics=("parallel",)),
        name="score_weights",
    )(w_query, sub_keys)


N_CAND = P_TOPK + 7 * 8 + 8


PAYLOAD_BITS = 17


def _extract_topk(vals, ids, n):
    top_v, top_i = [], []
    big = jnp.int32(2 ** 30)
    for _ in range(n):
        m = jnp.max(vals, axis=0, keepdims=True)
        pick = jnp.min(jnp.where(vals == m, ids, big), axis=0, keepdims=True)
        top_v.append(m)
        top_i.append(pick)
        vals = jnp.where(ids == pick, -jnp.inf, vals)
    return top_v, top_i


def _retrieve(sc_ref, eidx_ref, gate_ref):
    lanes = sc_ref.shape[1]
    key_id = lax.broadcasted_iota(jnp.int32, (N_KEYS, lanes), 0)
    sub16 = lax.broadcasted_iota(jnp.int32, (P_TOPK, lanes), 0)
    sub8 = lax.broadcasted_iota(jnp.int32, (8, lanes), 0)
    cand_id = jnp.concatenate([sub16] + [a * P_TOPK + sub8 for a in range(1, 8)] + [(8 + sub8) * P_TOPK], axis=0)

    def head(h, carry):
        base = pl.multiple_of(h * 2 * N_KEYS, 2 * N_KEYS)
        v0, i0 = _extract_topk(sc_ref[pl.ds(base, N_KEYS), :], key_id, P_TOPK)
        v1, i1 = _extract_topk(sc_ref[pl.ds(base + N_KEYS, N_KEYS), :], key_id, P_TOPK)
        v1_16, i1_16 = jnp.concatenate(v1, axis=0), jnp.concatenate(i1, axis=0)
        v1_8, i1_8 = jnp.concatenate(v1[:8], axis=0), jnp.concatenate(i1[:8], axis=0)
        cand = jnp.concatenate([v0[0] + v1_16] + [v0[a] + v1_8 for a in range(1, 8)]
                               + [jnp.concatenate(v0[8:], axis=0) + v1[0]], axis=0)
        cidx = jnp.concatenate([i0[0] * N_KEYS + i1_16] + [i0[a] * N_KEYS + i1_8 for a in range(1, 8)]
                               + [jnp.concatenate(i0[8:], axis=0) * N_KEYS + i1[0]], axis=0) * TILE_WORDS
        fv, fe = _extract_topk(cand, (cand_id << PAYLOAD_BITS) | cidx, P_TOPK)
        fv = jnp.concatenate(fv, axis=0)
        e = jnp.exp(fv - fv[0:1])
        out = pl.multiple_of(h * P_TOPK, P_TOPK)
        gate_ref[pl.ds(out, P_TOPK), :] = e / jnp.sum(e, axis=0, keepdims=True)
        eidx_ref[pl.ds(out, P_TOPK), :] = jnp.concatenate(fe, axis=0) & ((1 << PAYLOAD_BITS) - 1)
        return carry

    lax.fori_loop(0, P_HEADS, head, 0)


def _tile_table_kernel(t_ref, o_ref):
    rows = t_ref.shape[0]
    lanes = ROW_SHAPE[1]
    for r in range(TILE_WORDS):
        lo = t_ref[:, (2 * r) * lanes:(2 * r + 1) * lanes].astype(BF16).astype(F32)
        hi = t_ref[:, (2 * r + 1) * lanes:(2 * r + 2) * lanes].astype(BF16).astype(F32)
        word = (pltpu.bitcast(lo, jnp.uint32) >> 16) | (pltpu.bitcast(hi, jnp.uint32) & jnp.uint32(0xFFFF0000))
        o_ref[pl.ds(r, rows, stride=TILE_WORDS), :] = word


def _tile_table(t):
    e = t.shape[0]
    rows = ROW_TILE
    return pl.pallas_call(
        _tile_table_kernel,
        grid=(e // rows,),
        in_specs=[pl.BlockSpec((rows, D_MODEL), lambda i: (i, 0))],
        out_specs=pl.BlockSpec((rows * TILE_WORDS, ROW_SHAPE[1]), lambda i: (i, 0)),
        out_shape=jax.ShapeDtypeStruct((e * TILE_WORDS, ROW_SHAPE[1]), jnp.uint32),
        compiler_params=_params(dimension_semantics=("parallel",)),
        name="tile_table",
    )(t)


def _expert_row(tbl_ref, word_row):
    words = tbl_ref[pl.ds(pl.multiple_of(word_row, TILE_WORDS), TILE_WORDS), :]
    return pltpu.bitcast(words, BF16).astype(F32)


def _sublane_fold(x, y, step, mask):
    if step == 4:
        return jnp.where(mask, x, y) + pltpu.roll(jnp.where(mask, y, x), 4, 0)
    return jnp.where(mask, x + pltpu.roll(x, 8 - step, 0), y + pltpu.roll(y, step, 0))


def _peer_u_kernel(idx_ref, hn_ref, gate_ref, tbl_ref, coef_ref, part_scr, hid_scr):
    toks = hn_ref.shape[0]
    sub = lax.broadcasted_iota(jnp.int32, ROW_SHAPE, 0)
    m4, m2, m1 = sub < 4, (sub & 3) < 2, (sub & 1) == 0
    lane_tok = lax.broadcasted_iota(jnp.int32, (N_SEL, toks), 1)
    part_scr[...] = jnp.zeros_like(part_scr)
    hid_scr[...] = jnp.zeros_like(hid_scr)

    def finish(t, slot):
        hid_scr[...] = jnp.where(lane_tok == t, jnp.sum(part_scr[slot], axis=1, keepdims=True), hid_scr[...])

    def token(t, carry):
        slot = t % 2
        x = hn_ref[t]
        finish(t - 1, 1 - slot)
        for h in range(N_SEL // 8):
            prod = [_expert_row(tbl_ref, idx_ref[t * N_SEL + h * 8 + j]) * x for j in range(8)]
            z = [_sublane_fold(prod[a], prod[a + 4], 4, m4) for a in (0, 2, 1, 3)]
            w0 = _sublane_fold(z[0], z[1], 2, m2)
            w1 = _sublane_fold(z[2], z[3], 2, m2)
            part_scr[slot, h * 8:(h + 1) * 8, :] = _sublane_fold(w0, w1, 1, m1)
        return carry

    lax.fori_loop(0, toks, token, 0)
    finish(toks - 1, (toks - 1) % 2)
    hid = hid_scr[...]
    coef_ref[...] = gate_ref[...] * (0.5 * hid * (1.0 + lax.erf(hid * (2.0 ** -0.5))))


def _peer_u(eidx_flat, hn3, gate, table):
    m = hn3.shape[0]
    tt = TOK_TILE
    return pl.pallas_call(
        _peer_u_kernel,
        grid=(m // tt,),
        in_specs=[pl.BlockSpec((tt * N_SEL,), lambda i: (i,), memory_space=pltpu.SMEM),
                  pl.BlockSpec((tt, *ROW_SHAPE), lambda i: (i, 0, 0)),
                  pl.BlockSpec((N_SEL, tt), lambda i: (0, i)),
                  _resident(table.shape)],
        out_specs=pl.BlockSpec((N_SEL, tt), lambda i: (0, i)),
        out_shape=jax.ShapeDtypeStruct((N_SEL, m), F32),
        scratch_shapes=[pltpu.VMEM((2, N_SEL, ROW_SHAPE[1]), F32), pltpu.VMEM((N_SEL, tt), F32)],
        compiler_params=_params(dimension_semantics=("parallel",)),
        name="peer_u",
    )(eidx_flat, hn3, gate, table)


def _peer_v_kernel(idx_ref, coef_ref, x_ref, tbl_ref, g_ref, y_ref, splat_scr, x2_scr):
    toks = x_ref.shape[0]
    n_acc = 3
    lane_tok = lax.broadcasted_iota(jnp.int32, (N_SEL, toks), 1)

    def splat(t):
        col = jnp.sum(jnp.where(lane_tok == t, coef_ref[...], 0.0), axis=1, keepdims=True)
        return jnp.broadcast_to(col, (N_SEL, ROW_SHAPE[1]))

    splat_scr[0] = splat(0)

    def token(t, carry):
        slot = t % 2
        nxt = splat(jnp.minimum(t + 1, toks - 1))
        acc = [jnp.zeros(ROW_SHAPE, F32) for _ in range(n_acc)]
        for h in range(N_SEL // 8):
            coef = splat_scr[slot, h * 8:(h + 1) * 8, :]
            for j in range(8):
                row = _expert_row(tbl_ref, idx_ref[t * N_SEL + h * 8 + j])
                acc[j % n_acc] = acc[j % n_acc] + coef[j:j + 1, :] * row
        x2_scr[t] = x_ref[t] + ((acc[0] + acc[1]) + acc[2])
        splat_scr[1 - slot] = nxt
        return carry

    lax.fori_loop(0, toks, token, 0)
    x2 = jnp.concatenate([x2_scr[:, s, :] for s in range(ROW_SHAPE[0])], axis=1)
    ms = jnp.mean(x2 * x2, axis=-1, keepdims=True)
    y_ref[...] = x2 * lax.rsqrt(ms + NORM_EPS) * g_ref[...]


def _peer_v(eidx_flat, coef, x3, table, normf_g):
    m = x3.shape[0]
    tt = TOK_TILE
    tok = pl.BlockSpec((tt, *ROW_SHAPE), lambda i: (i, 0, 0))
    return pl.pallas_call(
        _peer_v_kernel,
        grid=(m // tt,),
        in_specs=[pl.BlockSpec((tt * N_SEL,), lambda i: (i,), memory_space=pltpu.SMEM),
                  pl.BlockSpec((N_SEL, tt), lambda i: (0, i)), tok, _resident(table.shape), _resident((1, D_MODEL))],
        out_specs=pl.BlockSpec((tt, D_MODEL), lambda i: (i, 0)),
        out_shape=jax.ShapeDtypeStruct((m, D_MODEL), F32),
        scratch_shapes=[pltpu.VMEM((2, N_SEL, ROW_SHAPE[1]), F32), pltpu.VMEM((tt, *ROW_SHAPE), F32)],
        compiler_params=_params(dimension_semantics=("parallel",)),
        name="peer_v",
    )(eidx_flat, coef, x3, table, normf_g)


def _stream_step(x, k_cache, v_cache, wkv0, shift0, w):
    batch, seq, _ = x.shape
    m = batch * seq
    assert m % ROW_TILE == 0 and m % TOK_TILE == 0 and batch % SCAN_BATCH == 0 and seq % SHIFT_GROUP == 0
    assert (seq % ROW_TILE == 0 and seq % SCAN_TILE == 0) or ROW_TILE % seq == 0
    za, zb, zg = _in_proj(x.reshape(m, D_MODEL), w["norm1_g"], w["w_in"])
    if k_cache is None:
        assert seq % ATT_TILE == 0
        o_a = _attn_prompt(za, w["rel_bias"], batch, seq)
    else:
        o_a = _attn_sample(za, k_cache, v_cache, w["rel_bias"], seq)
    r, dec, k, v, a, b, g, bonus = _rwkv_prep(zb, shift0, w, batch, seq)
    y, wkv = _wkv_scan(r, dec, k, v, a, b, wkv0, batch, seq)
    x1, hn, eidx, gate = _post_mix(x.reshape(m, D_MODEL), o_a, y, bonus, g, zg, w, batch, seq)
    eidx_flat = eidx.reshape(-1)
    coef = _peer_u(eidx_flat, hn, gate, w["expert_u"])
    out = _peer_v(eidx_flat, coef, x1, w["expert_v"], w["normf_g"])

    keep = min(BAND_CHUNKS * CHUNK, seq) if k_cache is None else seq
    zk = za.reshape(batch, seq, A_COLS)[:, seq - keep:]
    heads = lambda t: t.reshape(batch, keep, HEADS, HEAD_DIM).transpose(0, 2, 1, 3)
    return (out.reshape(batch, seq, D_MODEL), heads(zk[..., WIDTH:2 * WIDTH]), heads(zk[..., 2 * WIDTH:]), wkv,
            zb.reshape(batch, seq, B_COLS)[:, -1:])


def kernel(x_prompt, x_sample, cache_attn_k, cache_attn_v, state_wkv, state_shift, norm1_g, w_in, rel_bias, shift_mu,
           w_decay0, w_decay_up, a0, w_a_up, w_g_up, k_k, k_a, r_k, lnx_g, lnx_b, w_proj_a, w_proj_b, w_out, norm2_g,
           w_query, sub_keys, expert_u, expert_v, normf_g):
    assert norm1_g.shape[0] == 1, "single-layer step"
    zeros = jnp.zeros((DECAY_RANK, WIDTH), F32)
    score_w = _score_weights(w_query[0], sub_keys[0])
    score_hi = score_w.astype(BF16)
    w = dict(
        norm1_g=norm1_g[0], w_in=w_in[0].astype(BF16), rel_bias=rel_bias[0],
        shift_mu=_row_vec(shift_mu[0]), w_decay0=_row_vec(w_decay0[0]), a0=_row_vec(a0[0]),
        w_decay_up=jnp.concatenate([w_decay_up[0], zeros], axis=0), w_a_up=jnp.concatenate([zeros, w_a_up[0]], axis=0),
        w_g_up=w_g_up[0], k_k=_row_vec(k_k[0]), k_a=_row_vec(k_a[0]), r_k=_row_vec(r_k[0]),
        head_sum=_head_sum_matrix(),
        lnx_g=_row_vec(lnx_g[0]), lnx_b=_row_vec(lnx_b[0]),
        w_proj_a=w_proj_a[0].astype(BF16), w_proj_b=w_proj_b[0].astype(BF16), w_out=w_out[0].astype(BF16),
        norm2_g=_row_vec(norm2_g[0]), score_hi=score_hi, score_lo=(score_w - score_hi.astype(F32)).astype(BF16),
        expert_u=_tile_table(expert_u[0]), expert_v=_tile_table(expert_v[0]),
        normf_g=_row_vec(normf_g))

    batch = x_prompt.shape[0]
    yp, kp, vp, wp, sp = _stream_step(x_prompt, None, None, jnp.zeros((batch, HEADS, HEAD_DIM, HEAD_DIM), F32),
                                      jnp.zeros((batch, 1, B_COLS), F32), w)
    ys, ks, vs, ws, ss = _stream_step(x_sample, cache_attn_k[0], cache_attn_v[0], state_wkv[0], state_shift[0], w)
    return (yp, ys, kp[None], vp[None], wp[None], sp[None], ks[None], vs[None], ws[None], ss[None])
```

```python
import functools

import jax
import jax.numpy as jnp
from jax import lax
from jax.experimental import pallas as pl
from jax.experimental.pallas import tpu as pltpu

F32 = jnp.float32
BF16 = jnp.bfloat16
HIGHEST = lax.Precision.HIGHEST

D_MODEL = 1024
CHUNK = 64
BAND_CHUNKS = 8
HEADS = 8
HEAD_DIM = 64
WIDTH = HEADS * HEAD_DIM
REL_CLIP = 128
DECAY_RANK = 64
AAA_RANK = 64
GATE_RANK = 128
LNX_EPS = 64e-5
NORM_EPS = 1e-6
P_HEADS = 8
N_KEYS = 128
P_HALF = 64
P_TOPK = 16
A_COLS = 3 * WIDTH
B_COLS = 3 * WIDTH + DECAY_RANK + AAA_RANK + GATE_RANK
G_COLS = 2 * D_MODEL
NEG = -1e30

VMEM_LIMIT = 56 * 1024 * 1024
ROW_TILE = 256
ATT_TILE = BAND_CHUNKS * CHUNK
BAND_KEYS = (BAND_CHUNKS + 1) * CHUNK
SCAN_TILE = 128
SCAN_BATCH = 2
TOK_TILE = 128
SHIFT_GROUP = 32
PAIRS = HEADS // 2
PAIR_W = 2 * HEAD_DIM
N_SEL = P_HEADS * P_TOPK
ROW_SHAPE = (8, D_MODEL // 8)
TILE_WORDS = ROW_SHAPE[0] // 2


def _params(**kw):
    return pltpu.CompilerParams(vmem_limit_bytes=VMEM_LIMIT, **kw)


def _resident(shape):
    nd = len(shape)
    return pl.BlockSpec(shape, lambda *_: (0,) * nd, pipeline_mode=pl.Buffered(1))


def _sigmoid(x):
    return 1.0 / (1.0 + jnp.exp(-x))


def _dot(a, b, **kw):
    return jnp.dot(a, b, preferred_element_type=F32, **kw)


def _dot_nt(a, b, **kw):
    return lax.dot_general(a, b, (((1,), (1,)), ((), ())), preferred_element_type=F32, **kw)


def _row_vec(a):
    return a.reshape(1, -1).astype(F32)


def _split_sums(x, ones_bf16):
    hi = x.astype(BF16)
    lo = (x - hi.astype(F32)).astype(BF16)
    return _dot(hi, ones_bf16) + _dot(lo, ones_bf16)


def _inproj_kernel(x_ref, g_ref, w_ref, za_ref, zb_ref, zg_ref):
    x = x_ref[...]
    y = x * lax.rsqrt(jnp.mean(x * x, axis=-1, keepdims=True) + NORM_EPS) * g_ref[...]
    yb = y.astype(BF16)
    za_ref[...] = _dot(yb, w_ref[:, :A_COLS])
    zb_ref[...] = _dot(yb, w_ref[:, A_COLS:A_COLS + B_COLS])
    zg_ref[...] = _dot(yb, w_ref[:, A_COLS + B_COLS:])


def _in_proj(x, norm_g, w_in_bf16):
    m = x.shape[0]
    in_cols = w_in_bf16.shape[1]
    row = lambda i: (i, 0)
    return pl.pallas_call(
        _inproj_kernel,
        grid=(m // ROW_TILE,),
        in_specs=[pl.BlockSpec((ROW_TILE, D_MODEL), row), _resident((1, D_MODEL)), _resident((D_MODEL, in_cols))],
        out_specs=[pl.BlockSpec((ROW_TILE, A_COLS), row), pl.BlockSpec((ROW_TILE, B_COLS), row),
                   pl.BlockSpec((ROW_TILE, G_COLS), row)],
        out_shape=[jax.ShapeDtypeStruct((m, A_COLS), F32), jax.ShapeDtypeStruct((m, B_COLS), F32),
                   jax.ShapeDtypeStruct((m, G_COLS), F32)],
        compiler_params=_params(dimension_semantics=("parallel",)),
        name="in_proj",
    )(x, _row_vec(norm_g), w_in_bf16)


def _softmax_pv(scores, values):
    m = scores[0].max(axis=-1, keepdims=True)
    for s in scores[1:]:
        m = jnp.maximum(m, s.max(axis=-1, keepdims=True))
    acc, den = None, None
    for s, v in zip(scores, values):
        p = jnp.exp(s - m)
        d = p.sum(axis=-1, keepdims=True)
        o = _dot(p.astype(BF16), v)
        acc = o if acc is None else acc + o
        den = d if den is None else den + d
    return acc / den


def _attn_prompt_kernel(q_ref, kp_ref, kc_ref, vp_ref, vc_ref, bias_ref, o_ref):
    first = pl.program_id(1) == 0
    scale = HEAD_DIM ** -0.5
    for h in range(HEADS):
        sl = slice(h * HEAD_DIM, (h + 1) * HEAD_DIM)
        q = (q_ref[:, sl] * scale).astype(BF16)
        s_prev = _dot_nt(q, kp_ref[:, sl].astype(BF16)) + bias_ref[h, :, :ATT_TILE]
        s_cur = _dot_nt(q, kc_ref[:, sl].astype(BF16)) + bias_ref[h, :, ATT_TILE:]
        s_prev = jnp.where(first, NEG, s_prev)
        o_ref[:, sl] = _softmax_pv([s_prev, s_cur], [vp_ref[:, sl].astype(BF16), vc_ref[:, sl].astype(BF16)])


def _prompt_bias_table(rel_bias):
    near = BAND_KEYS - (REL_CLIP // CHUNK + 1) * CHUNK
    dist = jnp.arange(CHUNK)[:, None] + BAND_CHUNKS * CHUNK - jnp.arange(near, BAND_KEYS)[None, :]
    varying = rel_bias[:, jnp.clip(dist, -REL_CLIP, REL_CLIP) + REL_CLIP]
    far = jnp.broadcast_to(rel_bias[:, -1][:, None, None], (HEADS, CHUNK, near))
    window = jnp.concatenate([far, varying], axis=2).astype(F32)
    rows = [jnp.pad(window, ((0, 0), (0, 0), (c * CHUNK, 2 * ATT_TILE - BAND_KEYS - c * CHUNK)), constant_values=NEG)
            for c in range(BAND_CHUNKS)]
    return jnp.concatenate(rows, axis=1)


def _attn_prompt(za, rel_bias, batch, seq):
    nb = seq // ATT_TILE
    blk = (ATT_TILE, WIDTH)
    cur = lambda col: (lambda b, i: (b * nb + i, col))
    prev = lambda col: (lambda b, i: (b * nb + jnp.maximum(i - 1, 0), col))
    return pl.pallas_call(
        _attn_prompt_kernel,
        grid=(batch, nb),
        in_specs=[pl.BlockSpec(blk, cur(0)), pl.BlockSpec(blk, prev(1)), pl.BlockSpec(blk, cur(1)),
                  pl.BlockSpec(blk, prev(2)), pl.BlockSpec(blk, cur(2)),
                  _resident((HEADS, ATT_TILE, 2 * ATT_TILE))],
        out_specs=pl.BlockSpec(blk, cur(0)),
        out_shape=jax.ShapeDtypeStruct((batch * seq, WIDTH), F32),
        compiler_params=_params(dimension_semantics=("parallel", "arbitrary")),
        name="attn_prompt",
    )(za, za, za, za, za, _prompt_bias_table(rel_bias))


def _attn_sample_kernel(q_ref, kn_ref, vn_ref, kc_ref, vc_ref, bc_ref, bn_ref, o_ref):
    scale = HEAD_DIM ** -0.5
    for h in range(HEADS):
        sl = slice(h * HEAD_DIM, (h + 1) * HEAD_DIM)
        q = (q_ref[:, sl] * scale).astype(BF16)
        s_cache = _dot_nt(q, kc_ref[0, h].astype(BF16)) + bc_ref[h]
        s_new = _dot_nt(q, kn_ref[:, sl].astype(BF16)) + bn_ref[h]
        o_ref[:, sl] = _softmax_pv([s_cache, s_new], [vc_ref[0, h].astype(BF16), vn_ref[:, sl].astype(BF16)])


def _attn_sample(za, k_cache, v_cache, rel_bias, seq):
    nb, _, past, _ = k_cache.shape
    near = max(past - REL_CLIP, 0)
    dist = jnp.arange(seq)[:, None] + past - jnp.arange(near, past + seq)[None, :]
    far = jnp.broadcast_to(rel_bias[:, -1][:, None, None], (HEADS, seq, near))
    bias = jnp.concatenate([far, rel_bias[:, jnp.clip(dist, -REL_CLIP, REL_CLIP) + REL_CLIP]], axis=2).astype(F32)
    blk = (seq, WIDTH)
    rows = lambda col: (lambda b: (b, col))
    cache = pl.BlockSpec((1, HEADS, past, HEAD_DIM), lambda b: (b, 0, 0, 0))
    return pl.pallas_call(
        _attn_sample_kernel,
        grid=(nb,),
        in_specs=[pl.BlockSpec(blk, rows(0)), pl.BlockSpec(blk, rows(1)), pl.BlockSpec(blk, rows(2)), cache, cache,
                  _resident((HEADS, seq, past)), _resident((HEADS, seq, seq))],
        out_specs=pl.BlockSpec(blk, rows(0)),
        out_shape=jax.ShapeDtypeStruct((nb * seq, WIDTH), F32),
        compiler_params=_params(dimension_semantics=("parallel",)),
        name="attn_sample",
    )(za, za, za, k_cache, v_cache, bias[:, :, :past], bias[:, :, past:])


def _softplus(x):
    return jnp.maximum(x, 0.0) + jnp.log(1.0 + jnp.exp(-jnp.abs(x)))


def _rwkv_prep_kernel(zb_ref, prev_ref, shift_ref, mu_ref, wd0_ref, wdu_ref, a0_ref, wau_ref, wgu_ref, kk_ref, ka_ref,
                      rk_ref, gsum_ref, r_o, w_o, k_o, v_o, a_o, b_o, g_o, bonus_o, *, seq):
    i = pl.program_id(0)
    zb = zb_ref[...]
    tm = zb.shape[0]
    row = lax.broadcasted_iota(jnp.int32, zb.shape, 0)
    prev = jnp.where(row == 0, prev_ref[7:8, :], pltpu.roll(zb, 1, 0))
    ngrp = tm // SHIFT_GROUP
    shift = jnp.broadcast_to(shift_ref[...][:, None, :], (ngrp, SHIFT_GROUP, B_COLS)).reshape(tm, B_COLS)
    prev = jnp.where(lax.rem(i * tm + row, seq) == 0, shift, prev)
    zm = zb + (prev - zb) * mu_ref[...]
    r = zm[:, 0:WIDTH]
    k = zm[:, WIDTH:2 * WIDTH]
    v = zm[:, 2 * WIDTH:3 * WIDTH]
    lora_in = zm[:, 3 * WIDTH:3 * WIDTH + DECAY_RANK + AAA_RANK]
    gate_in = zm[:, 3 * WIDTH + DECAY_RANK + AAA_RANK:]
    w_log = -_softplus(-(wd0_ref[...] + _dot(jnp.tanh(lora_in), wdu_ref[...], precision=HIGHEST))) - 0.5
    decay = jnp.exp(-jnp.exp(w_log))
    a = _sigmoid(a0_ref[...] + _dot(lora_in, wau_ref[...], precision=HIGHEST))
    g = _dot(_sigmoid(gate_in), wgu_ref[...], precision=HIGHEST)
    kk = k * kk_ref[...]
    kk = kk / jnp.maximum(jnp.sqrt(_split_sums(kk * kk, gsum_ref[...])), 1e-12)
    kmod = k * (1.0 + (a - 1.0) * ka_ref[...])
    for ref, val in ((r_o, r), (w_o, decay), (k_o, kmod), (v_o, v), (a_o, -kk), (b_o, kk * a)):
        for pair in range(PAIRS):
            ref[pair] = val[:, pair * PAIR_W:(pair + 1) * PAIR_W].reshape(ref.shape[1:])
    g_o[...] = g
    bonus_o[...] = _split_sums(r * kmod * rk_ref[...], gsum_ref[...]) * v


def _head_sum_matrix():
    h = jnp.arange(WIDTH) // HEAD_DIM
    return jnp.where(h[:, None] == h[None, :], 1.0, 0.0).astype(BF16)


def _scan_layout(batch, seq):
    steps = min(seq, SCAN_TILE)
    shape = (PAIRS, seq // steps, batch, steps, PAIR_W)
    if seq >= ROW_TILE:
        per_seq = seq // ROW_TILE
        block = (PAIRS, ROW_TILE // steps, 1, steps, PAIR_W)
        index = lambda i: (0, i % per_seq, i // per_seq, 0, 0)
    else:
        block = (PAIRS, 1, ROW_TILE // seq, steps, PAIR_W)
        index = lambda i: (0, 0, i, 0, 0)
    return steps, shape, pl.BlockSpec(block, index)


def _rwkv_prep(zb, shift0, w, batch, seq):
    m = zb.shape[0]
    tm = ROW_TILE
    row = lambda i: (i, 0)
    groups_per_seq = seq // SHIFT_GROUP
    shift_rows = jnp.zeros((batch, groups_per_seq, B_COLS), F32).at[:, 0].set(shift0.reshape(batch, B_COLS))
    _, scan_shape, scan_spec = _scan_layout(batch, seq)
    flat = jax.ShapeDtypeStruct((m, WIDTH), F32)
    return pl.pallas_call(
        functools.partial(_rwkv_prep_kernel, seq=seq),
        grid=(m // tm,),
        in_specs=[pl.BlockSpec((tm, B_COLS), row),
                  pl.BlockSpec((8, B_COLS), lambda i: (jnp.maximum(i * (tm // 8) - 1, 0), 0)),
                  pl.BlockSpec((tm // SHIFT_GROUP, B_COLS), row),
                  _resident((1, B_COLS)), _resident((1, WIDTH)), _resident((DECAY_RANK + AAA_RANK, WIDTH)),
                  _resident((1, WIDTH)), _resident((DECAY_RANK + AAA_RANK, WIDTH)), _resident((GATE_RANK, WIDTH)),
                  _resident((1, WIDTH)), _resident((1, WIDTH)), _resident((1, WIDTH)), _resident((WIDTH, WIDTH))],
        out_specs=[scan_spec] * 6 + [pl.BlockSpec((tm, WIDTH), row)] * 2,
        out_shape=[jax.ShapeDtypeStruct(scan_shape, F32)] * 6 + [flat] * 2,
        compiler_params=_params(dimension_semantics=("parallel",)),
        name="rwkv_prep",
    )(zb, zb, shift_rows.reshape(m // SHIFT_GROUP, B_COLS), w["shift_mu"], w["w_decay0"], w["w_decay_up"], w["a0"],
      w["w_a_up"], w["w_g_up"], w["k_k"], w["k_a"], w["r_k"], w["head_sum"])


def _scan_kernel(r_ref, w_ref, k_ref, v_ref, a_ref, b_ref, s0_ref, y_ref, st_ref, s_scr, *, steps):
    tb = pl.program_id(1)
    chains = [(b, p) for b in range(s_scr.shape[0]) for p in range(PAIRS)]

    @pl.when(tb == 0)
    def _():
        s_scr[...] = s0_ref[...]

    lane = lax.broadcasted_iota(jnp.int32, (HEAD_DIM, PAIR_W), 1)
    sub = lax.broadcasted_iota(jnp.int32, (HEAD_DIM, PAIR_W), 0)
    lo = lane < HEAD_DIM
    diag = (lane & (HEAD_DIM - 1)) == sub
    sub8 = lax.broadcasted_iota(jnp.int32, (8, PAIR_W), 0)

    def head_sums(x):
        s_lo = jnp.sum(jnp.where(lo, x, 0.0), axis=1, keepdims=True)
        s_hi = jnp.sum(jnp.where(lo, 0.0, x), axis=1, keepdims=True)
        return jnp.where(lo, s_lo, s_hi)

    def row_of(ref, chain, t):
        b, p = chain
        tile8, j = t
        return ref[p, 0, b, pl.ds(pl.multiple_of(tile8 * 8, 8), 8), :][j:j + 1, :]

    same_head = ((lax.broadcasted_iota(jnp.int32, (PAIR_W, PAIR_W), 0) < HEAD_DIM)
                 == (lax.broadcasted_iota(jnp.int32, (PAIR_W, PAIR_W), 1) < HEAD_DIM))
    head_ones = jnp.where(same_head, 1.0, 0.0).astype(BF16)

    def head_sums_mxu(x):
        hi = x.astype(BF16)
        lo = (x - hi.astype(F32)).astype(BF16)
        sums = _dot(jnp.concatenate([hi, lo], axis=0), head_ones)
        return sums[:HEAD_DIM] + sums[HEAD_DIM:]

    def emit_y(chain, t, s):
        b, p = chain
        y_col = head_sums_mxu(s * row_of(r_ref, chain, t))
        y_ref[p, 0, b, pl.ds(t[0] * 8 + t[1], 1), :] = jnp.sum(jnp.where(diag, y_col, 0.0), axis=0, keepdims=True)

    def update(chain, t, s):
        sums = head_sums_mxu if chain[1] else head_sums
        sa = sums(s * row_of(a_ref, chain, t))
        v_col = head_sums(jnp.where(diag, row_of(v_ref, chain, t), 0.0))
        return s * row_of(w_ref, chain, t) + sa * row_of(b_ref, chain, t) + v_col * row_of(k_ref, chain, t)

    def step(tile8, j):
        prev = (tile8, j - 1) if j else (tile8 - 1, 7)
        for b, p in chains:
            s = s_scr[b, p]
            if not (isinstance(tile8, int) and tile8 == 0 and j == 0):
                emit_y((b, p), prev, s)
            s_scr[b, p] = update((b, p), (tile8, j), s)

    def trip(tile8, carry):
        for j in range(8):
            step(tile8, j)
        return carry

    trip(0, 0)
    lax.fori_loop(1, steps // 8, trip, 0)
    for b, p in chains:
        emit_y((b, p), (steps // 8 - 1, 7), s_scr[b, p])

    @pl.when(tb == pl.num_programs(1) - 1)
    def _():
        st_ref[...] = s_scr[...]


def _pair_state(s):
    b = s.shape[0]
    return s.reshape(b, PAIRS, 2, HEAD_DIM, HEAD_DIM).transpose(0, 1, 3, 2, 4).reshape(b, PAIRS, HEAD_DIM, PAIR_W)


def _unpair_state(s):
    b = s.shape[0]
    return s.reshape(b, PAIRS, HEAD_DIM, 2, HEAD_DIM).transpose(0, 1, 3, 2, 4).reshape(b, HEADS, HEAD_DIM, HEAD_DIM)


def _wkv_scan(r, w, k, v, a, b, s0, batch, seq):
    steps, shape, _ = _scan_layout(batch, seq)
    bg = SCAN_BATCH
    blk = pl.BlockSpec((PAIRS, 1, bg, steps, PAIR_W), lambda bi, ti: (0, ti, bi, 0, 0))
    state = pl.BlockSpec((bg, PAIRS, HEAD_DIM, PAIR_W), lambda bi, ti: (bi, 0, 0, 0))
    y, st = pl.pallas_call(
        functools.partial(_scan_kernel, steps=steps),
        grid=(batch // bg, seq // steps),
        in_specs=[blk] * 6 + [state],
        out_specs=[blk, state],
        out_shape=[jax.ShapeDtypeStruct(shape, F32), jax.ShapeDtypeStruct((batch, PAIRS, HEAD_DIM, PAIR_W), F32)],
        scratch_shapes=[pltpu.VMEM((bg, PAIRS, HEAD_DIM, PAIR_W), F32)],
        compiler_params=_params(dimension_semantics=("parallel", "arbitrary")),
        name="wkv_scan",
    )(r, w, k, v, a, b, _pair_state(s0.astype(F32)))
    return y, _unpair_state(st)


def _postmix_kernel(x_ref, oa_ref, y_ref, bonus_ref, g_ref, zg_ref, lng_ref, lnb_ref, gsum_ref, wpa_ref, wpb_ref,
                    wout_ref, n2g_ref, wch_ref, wcl_ref, x1_ref, hn_ref, eidx_ref, gate_ref, sc_scr, eidx_scr):
    tm = x_ref.shape[0]
    y = jnp.concatenate([y_ref[pair].reshape(tm, PAIR_W) for pair in range(PAIRS)], axis=1)
    mu = _split_sums(y, gsum_ref[...]) * (1.0 / HEAD_DIM)
    d = y - mu
    var = _split_sums(d * d, gsum_ref[...]) * (1.0 / HEAD_DIM)
    yn = d * lax.rsqrt(var + LNX_EPS) * lng_ref[...] + lnb_ref[...]
    ob = (yn + bonus_ref[...]) * g_ref[...]
    pa = _dot(oa_ref[...].astype(BF16), wpa_ref[...])
    pb = _dot(ob.astype(BF16), wpb_ref[...])
    merged = _sigmoid(zg_ref[:, :D_MODEL]) * pa + _sigmoid(zg_ref[:, D_MODEL:]) * pb
    x1 = x_ref[...] + _dot(merged.astype(BF16), wout_ref[...])
    hn = x1 * lax.rsqrt(jnp.mean(x1 * x1, axis=-1, keepdims=True) + NORM_EPS) * n2g_ref[...]
    for s in range(ROW_SHAPE[0]):
        cols = slice(s * ROW_SHAPE[1], (s + 1) * ROW_SHAPE[1])
        x1_ref[:, s, :] = x1[:, cols]
        hn_ref[:, s, :] = hn[:, cols]
    hh = hn.astype(BF16)
    hl = (hn - hh.astype(F32)).astype(BF16)
    sc_scr[...] = _dot_nt(wch_ref[...], hh) + (_dot_nt(wcl_ref[...], hh) + _dot_nt(wch_ref[...], hl))
    _retrieve(sc_scr, eidx_scr, gate_ref)
    eidx_ref[...] = eidx_scr[...].T


def _post_mix(x, o_a, y, bonus, g, zg, w, batch, seq):
    m = x.shape[0]
    tm = ROW_TILE
    row = lambda i: (i, 0)
    nsc = w["score_hi"].shape[0]
    wide = pl.BlockSpec((tm, D_MODEL), row)
    half = pl.BlockSpec((tm, WIDTH), row)
    tiles = pl.BlockSpec((tm, *ROW_SHAPE), lambda i: (i, 0, 0))
    picks = pl.BlockSpec((N_SEL, tm), lambda i: (0, i))
    _, _, scan_spec = _scan_layout(batch, seq)
    return pl.pallas_call(
        _postmix_kernel,
        grid=(m // tm,),
        in_specs=[wide, half, scan_spec, half, half, pl.BlockSpec((tm, G_COLS), row),
                  _resident((1, WIDTH)), _resident((1, WIDTH)), _resident((WIDTH, WIDTH)),
                  _resident((WIDTH, D_MODEL)), _resident((WIDTH, D_MODEL)), _resident((D_MODEL, D_MODEL)),
                  _resident((1, D_MODEL)), _resident((nsc, D_MODEL)), _resident((nsc, D_MODEL))],
        out_specs=[tiles, tiles, pl.BlockSpec((tm, N_SEL), row), picks],
        out_shape=[jax.ShapeDtypeStruct((m, *ROW_SHAPE), F32), jax.ShapeDtypeStruct((m, *ROW_SHAPE), F32),
                   jax.ShapeDtypeStruct((m, N_SEL), jnp.int32), jax.ShapeDtypeStruct((N_SEL, m), F32)],
        scratch_shapes=[pltpu.VMEM((nsc, tm), F32), pltpu.VMEM((N_SEL, tm), jnp.int32)],
        compiler_params=_params(dimension_semantics=("parallel",)),
        name="post_mix",
    )(x, o_a, y, bonus, g, zg, w["lnx_g"], w["lnx_b"], w["head_sum"], w["w_proj_a"], w["w_proj_b"], w["w_out"],
      w["norm2_g"], w["score_hi"], w["score_lo"])


def _score_weight_kernel(wq_ref, sk_ref, o_ref):
    for c in range(2):
        wq = wq_ref[:, c * P_HALF:(c + 1) * P_HALF]
        o_ref[c * N_KEYS:(c + 1) * N_KEYS, :] = _dot_nt(sk_ref[0, c], wq, precision=HIGHEST)


def _score_weights(w_query, sub_keys):
    return pl.pallas_call(
        _score_weight_kernel,
        grid=(P_HEADS,),
        in_specs=[pl.BlockSpec((D_MODEL, 2 * P_HALF), lambda h: (0, h)),
                  pl.BlockSpec((1, 2, N_KEYS, P_HALF), lambda h: (h, 0, 0, 0))],
        out_specs=pl.BlockSpec((2 * N_KEYS, D_MODEL), lambda h: (h, 0)),
        out_shape=jax.ShapeDtypeStruct((P_HEADS * 2 * N_KEYS, D_MODEL), F32),
        compiler_params=_params(dimension_semantics=("parallel",)),
        name="score_weights",
    )(w_query, sub_keys)


N_CAND = P_TOPK + 7 * 8 + 8


PAYLOAD_BITS = 17


def _extract_topk(vals, ids, n):
    top_v, top_i = [], []
    big = jnp.int32(2 ** 30)
    for _ in range(n):
        m = jnp.max(vals, axis=0, keepdims=True)
        pick = jnp.min(jnp.where(vals == m, ids, big), axis=0, keepdims=True)
        top_v.append(m)
        top_i.append(pick)
        vals = jnp.where(ids == pick, -jnp.inf, vals)
    return top_v, top_i


def _retrieve(sc_ref, eidx_ref, gate_ref):
    lanes = sc_ref.shape[1]
    key_id = lax.broadcasted_iota(jnp.int32, (N_KEYS, lanes), 0)
    sub16 = lax.broadcasted_iota(jnp.int32, (P_TOPK, lanes), 0)
    sub8 = lax.broadcasted_iota(jnp.int32, (8, lanes), 0)
    cand_id = jnp.concatenate([sub16] + [a * P_TOPK + sub8 for a in range(1, 8)] + [(8 + sub8) * P_TOPK], axis=0)

    def head(h, carry):
        base = pl.multiple_of(h * 2 * N_KEYS, 2 * N_KEYS)
        v0, i0 = _extract_topk(sc_ref[pl.ds(base, N_KEYS), :], key_id, P_TOPK)
        v1, i1 = _extract_topk(sc_ref[pl.ds(base + N_KEYS, N_KEYS), :], key_id, P_TOPK)
        v1_16, i1_16 = jnp.concatenate(v1, axis=0), jnp.concatenate(i1, axis=0)
        v1_8, i1_8 = jnp.concatenate(v1[:8], axis=0), jnp.concatenate(i1[:8], axis=0)
        cand = jnp.concatenate([v0[0] + v1_16] + [v0[a] + v1_8 for a in range(1, 8)]
                               + [jnp.concatenate(v0[8:], axis=0) + v1[0]], axis=0)
        cidx = jnp.concatenate([i0[0] * N_KEYS + i1_16] + [i0[a] * N_KEYS + i1_8 for a in range(1, 8)]
                               + [jnp.concatenate(i0[8:], axis=0) * N_KEYS + i1[0]], axis=0) * TILE_WORDS
        fv, fe = _extract_topk(cand, (cand_id << PAYLOAD_BITS) | cidx, P_TOPK)
        fv = jnp.concatenate(fv, axis=0)
        e = jnp.exp(fv - fv[0:1])
        out = pl.multiple_of(h * P_TOPK, P_TOPK)
        gate_ref[pl.ds(out, P_TOPK), :] = e / jnp.sum(e, axis=0, keepdims=True)
        eidx_ref[pl.ds(out, P_TOPK), :] = jnp.concatenate(fe, axis=0) & ((1 << PAYLOAD_BITS) - 1)
        return carry

    lax.fori_loop(0, P_HEADS, head, 0)


def _tile_table_kernel(t_ref, o_ref):
    rows = t_ref.shape[0]
    lanes = ROW_SHAPE[1]
    for r in range(TILE_WORDS):
        lo = t_ref[:, (2 * r) * lanes:(2 * r + 1) * lanes].astype(BF16).astype(F32)
        hi = t_ref[:, (2 * r + 1) * lanes:(2 * r + 2) * lanes].astype(BF16).astype(F32)
        word = (pltpu.bitcast(lo, jnp.uint32) >> 16) | (pltpu.bitcast(hi, jnp.uint32) & jnp.uint32(0xFFFF0000))
        o_ref[pl.ds(r, rows, stride=TILE_WORDS), :] = word


def _tile_table(t):
    e = t.shape[0]
    rows = ROW_TILE
    return pl.pallas_call(
        _tile_table_kernel,
        grid=(e // rows,),
        in_specs=[pl.BlockSpec((rows, D_MODEL), lambda i: (i, 0))],
        out_specs=pl.BlockSpec((rows * TILE_WORDS, ROW_SHAPE[1]), lambda i: (i, 0)),
        out_shape=jax.ShapeDtypeStruct((e * TILE_WORDS, ROW_SHAPE[1]), jnp.uint32),
        compiler_params=_params(dimension_semantics=("parallel",)),
        name="tile_table",
    )(t)


def _expert_row(tbl_ref, word_row):
    words = tbl_ref[pl.ds(pl.multiple_of(word_row, TILE_WORDS), TILE_WORDS), :]
    return pltpu.bitcast(words, BF16).astype(F32)


def _sublane_fold(x, y, step, mask):
    if step == 4:
        return jnp.where(mask, x, y) + pltpu.roll(jnp.where(mask, y, x), 4, 0)
    return jnp.where(mask, x + pltpu.roll(x, 8 - step, 0), y + pltpu.roll(y, step, 0))


def _peer_u_kernel(idx_ref, hn_ref, gate_ref, tbl_ref, coef_ref, part_scr, hid_scr):
    toks = hn_ref.shape[0]
    sub = lax.broadcasted_iota(jnp.int32, ROW_SHAPE, 0)
    m4, m2, m1 = sub < 4, (sub & 3) < 2, (sub & 1) == 0
    lane_tok = lax.broadcasted_iota(jnp.int32, (N_SEL, toks), 1)
    part_scr[...] = jnp.zeros_like(part_scr)
    hid_scr[...] = jnp.zeros_like(hid_scr)

    def finish(t, slot):
        hid_scr[...] = jnp.where(lane_tok == t, jnp.sum(part_scr[slot], axis=1, keepdims=True), hid_scr[...])

    def token(t, carry):
        slot = t % 2
        x = hn_ref[t]
        finish(t - 1, 1 - slot)
        for h in range(N_SEL // 8):
            prod = [_expert_row(tbl_ref, idx_ref[t * N_SEL + h * 8 + j]) * x for j in range(8)]
            z = [_sublane_fold(prod[a], prod[a + 4], 4, m4) for a in (0, 2, 1, 3)]
            w0 = _sublane_fold(z[0], z[1], 2, m2)
            w1 = _sublane_fold(z[2], z[3], 2, m2)
            part_scr[slot, h * 8:(h + 1) * 8, :] = _sublane_fold(w0, w1, 1, m1)
        return carry

    lax.fori_loop(0, toks, token, 0)
    finish(toks - 1, (toks - 1) % 2)
    hid = hid_scr[...]
    coef_ref[...] = gate_ref[...] * (0.5 * hid * (1.0 + lax.erf(hid * (2.0 ** -0.5))))


def _peer_u(eidx_flat, hn3, gate, table):
    m = hn3.shape[0]
    tt = TOK_TILE
    return pl.pallas_call(
        _peer_u_kernel,
        grid=(m // tt,),
        in_specs=[pl.BlockSpec((tt * N_SEL,), lambda i: (i,), memory_space=pltpu.SMEM),
                  pl.BlockSpec((tt, *ROW_SHAPE), lambda i: (i, 0, 0)),
                  pl.BlockSpec((N_SEL, tt), lambda i: (0, i)),
                  _resident(table.shape)],
        out_specs=pl.BlockSpec((N_SEL, tt), lambda i: (0, i)),
        out_shape=jax.ShapeDtypeStruct((N_SEL, m), F32),
        scratch_shapes=[pltpu.VMEM((2, N_SEL, ROW_SHAPE[1]), F32), pltpu.VMEM((N_SEL, tt), F32)],
        compiler_params=_params(dimension_semantics=("parallel",)),
        name="peer_u",
    )(eidx_flat, hn3, gate, table)


def _peer_v_kernel(idx_ref, coef_ref, x_ref, tbl_ref, g_ref, y_ref, splat_scr, x2_scr):
    toks = x_ref.shape[0]
    n_acc = 3
    lane_tok = lax.broadcasted_iota(jnp.int32, (N_SEL, toks), 1)

    def splat(t):
        col = jnp.sum(jnp.where(lane_tok == t, coef_ref[...], 0.0), axis=1, keepdims=True)
        return jnp.broadcast_to(col, (N_SEL, ROW_SHAPE[1]))

    splat_scr[0] = splat(0)

    def token(t, carry):
        slot = t % 2
        nxt = splat(jnp.minimum(t + 1, toks - 1))
        acc = [jnp.zeros(ROW_SHAPE, F32) for _ in range(n_acc)]
        for h in range(N_SEL // 8):
            coef = splat_scr[slot, h * 8:(h + 1) * 8, :]
            for j in range(8):
                row = _expert_row(tbl_ref, idx_ref[t * N_SEL + h * 8 + j])
                acc[j % n_acc] = acc[j % n_acc] + coef[j:j + 1, :] * row
        x2_scr[t] = x_ref[t] + ((acc[0] + acc[1]) + acc[2])
        splat_scr[1 - slot] = nxt
        return carry

    lax.fori_loop(0, toks, token, 0)
    x2 = jnp.concatenate([x2_scr[:, s, :] for s in range(ROW_SHAPE[0])], axis=1)
    ms = jnp.mean(x2 * x2, axis=-1, keepdims=True)
    y_ref[...] = x2 * lax.rsqrt(ms + NORM_EPS) * g_ref[...]


def _peer_v(eidx_flat, coef, x3, table, normf_g):
    m = x3.shape[0]
    tt = TOK_TILE
    tok = pl.BlockSpec((tt, *ROW_SHAPE), lambda i: (i, 0, 0))
    return pl.pallas_call(
        _peer_v_kernel,
        grid=(m // tt,),
        in_specs=[pl.BlockSpec((tt * N_SEL,), lambda i: (i,), memory_space=pltpu.SMEM),
                  pl.BlockSpec((N_SEL, tt), lambda i: (0, i)), tok, _resident(table.shape), _resident((1, D_MODEL))],
        out_specs=pl.BlockSpec((tt, D_MODEL), lambda i: (i, 0)),
        out_shape=jax.ShapeDtypeStruct((m, D_MODEL), F32),
        scratch_shapes=[pltpu.VMEM((2, N_SEL, ROW_SHAPE[1]), F32), pltpu.VMEM((tt, *ROW_SHAPE), F32)],
        compiler_params=_params(dimension_semantics=("parallel",)),
        name="peer_v",
    )(eidx_flat, coef, x3, table, normf_g)


def _stream_step(x, k_cache, v_cache, wkv0, shift0, w):
    batch, seq, _ = x.shape
    m = batch * seq
    assert m % ROW_TILE == 0 and m % TOK_TILE == 0 and batch % SCAN_BATCH == 0 and seq % SHIFT_GROUP == 0
    assert (seq % ROW_TILE == 0 and seq % SCAN_TILE == 0) or ROW_TILE % seq == 0
    za, zb, zg = _in_proj(x.reshape(m, D_MODEL), w["norm1_g"], w["w_in"])
    if k_cache is None:
        assert seq % ATT_TILE == 0
        o_a = _attn_prompt(za, w["rel_bias"], batch, seq)
    else:
        o_a = _attn_sample(za, k_cache, v_cache, w["rel_bias"], seq)
    r, dec, k, v, a, b, g, bonus = _rwkv_prep(zb, shift0, w, batch, seq)
    y, wkv = _wkv_scan(r, dec, k, v, a, b, wkv0, batch, seq)
    x1, hn, eidx, gate = _post_mix(x.reshape(m, D_MODEL), o_a, y, bonus, g, zg, w, batch, seq)
    eidx_flat = eidx.reshape(-1)
    coef = _peer_u(eidx_flat, hn, gate, w["expert_u"])
    out = _peer_v(eidx_flat, coef, x1, w["expert_v"], w["normf_g"])

    keep = min(BAND_CHUNKS * CHUNK, seq) if k_cache is None else seq
    zk = za.reshape(batch, seq, A_COLS)[:, seq - keep:]
    heads = lambda t: t.reshape(batch, keep, HEADS, HEAD_DIM).transpose(0, 2, 1, 3)
    return (out.reshape(batch, seq, D_MODEL), heads(zk[..., WIDTH:2 * WIDTH]), heads(zk[..., 2 * WIDTH:]), wkv,
            zb.reshape(batch, seq, B_COLS)[:, -1:])


def kernel(x_prompt, x_sample, cache_attn_k, cache_attn_v, state_wkv, state_shift, norm1_g, w_in, rel_bias, shift_mu,
           w_decay0, w_decay_up, a0, w_a_up, w_g_up, k_k, k_a, r_k, lnx_g, lnx_b, w_proj_a, w_proj_b, w_out, norm2_g,
           w_query, sub_keys, expert_u, expert_v, normf_g):
    assert norm1_g.shape[0] == 1, "single-layer step"
    zeros = jnp.zeros((DECAY_RANK, WIDTH), F32)
    score_w = _score_weights(w_query[0], sub_keys[0])
    score_hi = score_w.astype(BF16)
    w = dict(
        norm1_g=norm1_g[0], w_in=w_in[0].astype(BF16), rel_bias=rel_bias[0],
        shift_mu=_row_vec(shift_mu[0]), w_decay0=_row_vec(w_decay0[0]), a0=_row_vec(a0[0]),
        w_decay_up=jnp.concatenate([w_decay_up[0], zeros], axis=0), w_a_up=jnp.concatenate([zeros, w_a_up[0]], axis=0),
        w_g_up=w_g_up[0], k_k=_row_vec(k_k[0]), k_a=_row_vec(k_a[0]), r_k=_row_vec(r_k[0]),
        head_sum=_head_sum_matrix(),
        lnx_g=_row_vec(lnx_g[0]), lnx_b=_row_vec(lnx_b[0]),
        w_proj_a=w_proj_a[0].astype(BF16), w_proj_b=w_proj_b[0].astype(BF16), w_out=w_out[0].astype(BF16),
        norm2_g=_row_vec(norm2_g[0]), score_hi=score_hi, score_lo=(score_w - score_hi.astype(F32)).astype(BF16),
        expert_u=_tile_table(expert_u[0]), expert_v=_tile_table(expert_v[0]),
        normf_g=_row_vec(normf_g))

    batch = x_prompt.shape[0]
    yp, kp, vp, wp, sp = _stream_step(x_prompt, None, None, jnp.zeros((batch, HEADS, HEAD_DIM, HEAD_DIM), F32),
                                      jnp.zeros((batch, 1, B_COLS), F32), w)
    ys, ks, vs, ws, ss = _stream_step(x_sample, cache_attn_k[0], cache_attn_v[0], state_wkv[0], state_shift[0], w)
    return (yp, ys, kp[None], vp[None], wp[None], sp[None], ks[None], vs[None], ws[None], ss[None])
```
